```python
import jax, jax.numpy as jnp
from jax import lax
import numpy as np

D_MODEL = 1024
BATCH = 1
SEQ = 16384
DEPTH = 4

GRID_W = 64
CTX_LEN = 256
N_MIXERS = 4
CTX_READERS = (0, 2)
N_PER_MIXER = tuple((DEPTH - k + N_MIXERS - 1) // N_MIXERS for k in range(N_MIXERS))
N_MOD = 9
D_FF = 2816
EPS = 1e-6
Q_BLOCK = 128
MLA_HEADS = 8
MLA_Q_LORA = 384
MLA_KV_LORA = 256
MLA_NOPE = 128
MLA_ROPE = 64
MLA_V = 128
MLA_SCALE = (MLA_NOPE + MLA_ROPE) ** -0.5
ROPE_BASE = 10000.0
ROPE_PAIRS_PER_AXIS = MLA_ROPE // 4
POOL_WINDOWS = (2, 4, 8, 16)
POOL_GROUPS = 4
POOL_GROUP = D_MODEL // POOL_GROUPS
NA_HEADS = 16
NA_HEAD_DIM = D_MODEL // NA_HEADS
NA_ROWS = 8
NA_COLS = 16
NA_SCALE = NA_HEAD_DIM ** -0.5
CONV_WIDTH = 3

kernel_name = "hybrid_interleaved_diffusion_trunk"


def rms_norm(x):
    xf = x.astype(jnp.float32)
    return (xf * lax.rsqrt(jnp.mean(xf * xf, axis=-1, keepdims=True) + EPS)).astype(x.dtype)


def modulate(x, shift, scale):
    return rms_norm(x) * (1 + scale) + shift


def ada_params(cond, w, b):
    m = jax.nn.silu(cond) @ w + b
    return jnp.split(m[:, None, :], N_MOD, axis=-1)


def swiglu(h, w_in, w_out):
    g, u = jnp.split(h @ w_in, 2, axis=-1)
    return (jax.nn.silu(g) * u) @ w_out


def axial_rope_angles(T):
    t = jnp.arange(T)
    row = (t // GRID_W).astype(jnp.float32)
    col = (t % GRID_W).astype(jnp.float32)
    freqs = ROPE_BASE ** (-jnp.arange(ROPE_PAIRS_PER_AXIS, dtype=jnp.float32) / ROPE_PAIRS_PER_AXIS)
    return jnp.concatenate([row[:, None] * freqs, col[:, None] * freqs], axis=-1)


def rope_2d(x, ang):
    shape = (1, ang.shape[0]) + (1,) * (x.ndim - 3) + (ang.shape[1],)
    cos = jnp.cos(ang).reshape(shape)
    sin = jnp.sin(ang).reshape(shape)
    xp = x.astype(jnp.float32).reshape(x.shape[:-1] + (-1, 2))
    x0, x1 = xp[..., 0], xp[..., 1]
    out = jnp.stack([x0 * cos - x1 * sin, x0 * sin + x1 * cos], axis=-1)
    return out.reshape(x.shape).astype(x.dtype)


def mla_attention(qn, qr, kn, kr, v):
    B, T, H, _ = qn.shape
    blk = min(Q_BLOCK, T)
    nb = T // blk

    def to_blocks(a):
        return jnp.moveaxis(a.reshape((B, nb, blk) + a.shape[2:]), 1, 0)

    def one_block(qs):
        qn_b, qr_b = qs
        s = (jnp.einsum('bqhd,bkhd->bhqk', qn_b, kn) + jnp.einsum('bqhr,bkr->bhqk', qr_b, kr)) * MLA_SCALE
        p = jax.nn.softmax(s.astype(jnp.float32), axis=-1).astype(v.dtype)
        return jnp.einsum('bhqk,bkhd->bqhd', p, v)

    o = lax.map(one_block, (to_blocks(qn), to_blocks(qr)))
    return jnp.moveaxis(o, 0, 1).reshape(B, T, H * MLA_V)


def mla_mixer(hx, hc, ang, ctx_out, w_dq, g_dq, w_uq, w_dkv, g_dkv, w_uk, w_uv, g_qn, g_qr, g_kn, g_kr, w_o):
    def queries(h):
        cq = rms_norm(h @ w_dq) * g_dq
        q = jnp.einsum('btr,rhd->bthd', cq, w_uq)
        return rms_norm(q[..., :MLA_NOPE]) * g_qn, rms_norm(q[..., MLA_NOPE:]) * g_qr

    def keys_values(h):
        kv = h @ w_dkv
        ckv = rms_norm(kv[..., :MLA_KV_LORA]) * g_dkv
        kn = rms_norm(jnp.einsum('btr,rhd->bthd', ckv, w_uk)) * g_kn
        kr = rms_norm(kv[..., MLA_KV_LORA:]) * g_kr
        v = jnp.einsum('btr,rhd->bthd', ckv, w_uv)
        return kn, kr, v

    qn, qr = queries(hx)
    qr = rope_2d(qr, ang)
    kn, kr, v = keys_values(hx)
    kr = rope_2d(kr, ang)
    kn_c, kr_c, v_c = keys_values(hc)
    o = mla_attention(qn, qr,
                      jnp.concatenate([kn_c, kn], axis=1),
                      jnp.concatenate([kr_c, kr], axis=1),
                      jnp.concatenate([v_c, v], axis=1))
    yx = o @ w_o
    yc = None
    if ctx_out:
        qn_c, qr_c = queries(hc)
        yc = mla_attention(qn_c, qr_c, kn_c, kr_c, v_c) @ w_o
    return yx, yc


def pool_mixer(h, w_pool, scale):
    B, T, D = h.shape
    hg = h.reshape(B, T, POOL_GROUPS, POOL_GROUP)
    hf = hg.astype(jnp.float32)
    cs = jnp.concatenate([jnp.zeros((B, 1, POOL_GROUPS, POOL_GROUP), jnp.float32), jnp.cumsum(hf, axis=1)], axis=1)
    t = jnp.arange(T)[:, None]
    half = jnp.array(POOL_WINDOWS, dtype=jnp.int32)[None, :] // 2
    lo = jnp.clip(t - half, 0, T)
    hi = jnp.clip(t + half, 0, T)
    g_idx = jnp.arange(POOL_GROUPS)
    win_sum = cs[:, hi, g_idx] - cs[:, lo, g_idx]
    mean = win_sum / (hi - lo).astype(jnp.float32)[None, :, :, None]
    y = (mean - hf).astype(h.dtype)
    return jnp.einsum('btgc,gcd->btgd', y, w_pool).reshape(B, T, D) * scale


def dense_attention(q, k, v, scale):
    s = jnp.einsum('bqhd,bkhd->bhqk', q, k) * scale
    p = jax.nn.softmax(s.astype(jnp.float32), axis=-1).astype(v.dtype)
    return jnp.einsum('bhqk,bkhd->bqhd', p, v)


def na_mixer(hx, hc, ctx_out, w_qkv, g_q, g_k, rpb, w_o):
    B, S, D = hx.shape
    rows = S // GRID_W
    kr_win = min(NA_ROWS, rows)
    w = w_qkv.reshape(D, 3, NA_HEADS, NA_HEAD_DIM)
    pl = jnp.einsum('btd,dnhe->btnhe', hx, w)
    q = rms_norm(pl[:, :, 0]) * g_q
    k = rms_norm(pl[:, :, 1]) * g_k
    v = pl[:, :, 2]
    pc = jnp.einsum('btd,dnhe->btnhe', hc, w[:, 1:])
    k_c = rms_norm(pc[:, :, 0]) * g_k
    v_c = pc[:, :, 1]

    qg = q.reshape(B, rows, GRID_W, NA_HEADS, NA_HEAD_DIM)
    kg = k.reshape(B, rows, GRID_W, NA_HEADS, NA_HEAD_DIM)
    vg = v.reshape(B, rows, GRID_W, NA_HEADS, NA_HEAD_DIM)
    cols = jnp.arange(GRID_W)
    col_start = jnp.clip(cols - NA_COLS // 2, 0, GRID_W - NA_COLS)
    col_idx = col_start[:, None] + jnp.arange(NA_COLS)[None, :]
    dc_idx = col_idx - cols[:, None] + (NA_COLS - 1)
    n_loc = kr_win * NA_COLS

    def one_row(r):
        rs = jnp.clip(r - kr_win // 2, 0, rows - kr_win)
        q_r = lax.dynamic_index_in_dim(qg, r, axis=1, keepdims=False)
        k_sel = lax.dynamic_slice_in_dim(kg, rs, kr_win, axis=1)[:, :, col_idx]
        v_sel = lax.dynamic_slice_in_dim(vg, rs, kr_win, axis=1)[:, :, col_idx]
        dr_idx = rs + jnp.arange(kr_win) - r + (NA_ROWS - 1)
        bias = rpb[:, dr_idx[:, None, None], dc_idx[None, :, :]].transpose(0, 2, 1, 3)
        s_loc = jnp.einsum('bqhd,biqjhd->bhqij', q_r, k_sel) * NA_SCALE + bias
        s_ctx = jnp.einsum('bqhd,bkhd->bhqk', q_r, k_c) * NA_SCALE
        s = jnp.concatenate([s_loc.reshape(B, NA_HEADS, GRID_W, n_loc), s_ctx], axis=-1)
        p = jax.nn.softmax(s.astype(jnp.float32), axis=-1).astype(v.dtype)
        p_loc = p[..., :n_loc].reshape(B, NA_HEADS, GRID_W, kr_win, NA_COLS)
        return (jnp.einsum('bhqij,biqjhd->bqhd', p_loc, v_sel)
                + jnp.einsum('bhqk,bkhd->bqhd', p[..., n_loc:], v_c))

    o = lax.map(one_row, jnp.arange(rows))
    yx = jnp.moveaxis(o, 0, 1).reshape(B, S, D) @ w_o
    yc = None
    if ctx_out:
        q_c = rms_norm(jnp.einsum('btd,dhe->bthe', hc, w[:, 0])) * g_q
        yc = dense_attention(q_c, k_c, v_c, NA_SCALE).reshape(hc.shape) @ w_o
    return yx, yc


def conv_mixer(h, w_in, w_conv, w_out):
    b_gate, c_gate, u = jnp.split(h @ w_in, 3, axis=-1)
    z = lax.conv_general_dilated(c_gate * u, w_conv[:, None, :], window_strides=(1,),
                                 padding=((CONV_WIDTH // 2, CONV_WIDTH // 2),),
                                 dimension_numbers=('NWC', 'WIO', 'NWC'),
                                 feature_group_count=h.shape[-1])
    return (b_gate * z) @ w_out


def setup_inputs(seed: int = 0) -> dict:
    key = jax.random.key(seed)
    ks = iter(jax.random.split(key, 40))
    D = D_MODEL
    nA, nB, nC, nD = N_PER_MIXER

    def nrm(shape, scale):
        return scale * jax.random.normal(next(ks), shape, jnp.float32)

    def gain(shape):
        return 1.0 + 0.02 * jax.random.normal(next(ks), shape, jnp.float32)

    return {
        "x": nrm((BATCH, SEQ, D), 1.0),
        "c": nrm((BATCH, D), 1.0),
        "ctx": nrm((BATCH, CTX_LEN, D), 1.0),
        "c_ctx": nrm((D,), 1.0),
        "mod_w": nrm((DEPTH, D, N_MOD * D), 0.5 * D ** -0.5),
        "mod_b": nrm((DEPTH, N_MOD * D), 0.01),
        "ffn_w_in": nrm((DEPTH, 2, D, 2 * D_FF), D ** -0.5),
        "ffn_w_out": nrm((DEPTH, 2, D_FF, D), D_FF ** -0.5),
        "mla_w_dq": nrm((nA, D, MLA_Q_LORA), D ** -0.5),
        "mla_g_dq": gain((nA, MLA_Q_LORA)),
        "mla_w_uq": nrm((nA, MLA_Q_LORA, MLA_HEADS, MLA_NOPE + MLA_ROPE), MLA_Q_LORA ** -0.5),
        "mla_w_dkv": nrm((nA, D, MLA_KV_LORA + MLA_ROPE), D ** -0.5),
        "mla_g_dkv": gain((nA, MLA_KV_LORA)),
        "mla_w_uk": nrm((nA, MLA_KV_LORA, MLA_HEADS, MLA_NOPE), MLA_KV_LORA ** -0.5),
        "mla_w_uv": nrm((nA, MLA_KV_LORA, MLA_HEADS, MLA_V), MLA_KV_LORA ** -0.5),
        "mla_g_qn": gain((nA, MLA_NOPE)),
        "mla_g_qr": gain((nA, MLA_ROPE)),
        "mla_g_kn": gain((nA, MLA_NOPE)),
        "mla_g_kr": gain((nA, MLA_ROPE)),
        "mla_w_o": nrm((nA, MLA_HEADS * MLA_V, D), (MLA_HEADS * MLA_V) ** -0.5),
        "pool_w": nrm((nB, POOL_GROUPS, POOL_GROUP, POOL_GROUP), POOL_GROUP ** -0.5),
        "pool_scale": 1.0 + 0.1 * jax.random.normal(next(ks), (nB, D), jnp.float32),
        "na_w_qkv": nrm((nC, D, 3 * NA_HEADS * NA_HEAD_DIM), D ** -0.5),
        "na_g_q": gain((nC, NA_HEAD_DIM)),
        "na_g_k": gain((nC, NA_HEAD_DIM)),
        "na_rpb": nrm((nC, NA_HEADS, 2 * NA_ROWS - 1, 2 * NA_COLS - 1), 0.1),
        "na_w_o": nrm((nC, NA_HEADS * NA_HEAD_DIM, D), (NA_HEADS * NA_HEAD_DIM) ** -0.5),
        "conv_w_in": nrm((nD, D, 3 * D), D ** -0.5),
        "conv_w": nrm((nD, CONV_WIDTH, D), CONV_WIDTH ** -0.5),
        "conv_w_out": nrm((nD, D, D), D ** -0.5),
    }


def reference(x, c, ctx, c_ctx, mod_w, mod_b, ffn_w_in, ffn_w_out,
              mla_w_dq, mla_g_dq, mla_w_uq, mla_w_dkv, mla_g_dkv, mla_w_uk, mla_w_uv,
              mla_g_qn, mla_g_qr, mla_g_kn, mla_g_kr, mla_w_o,
              pool_w, pool_scale,
              na_w_qkv, na_g_q, na_g_k, na_rpb, na_w_o,
              conv_w_in, conv_w, conv_w_out):
    B, S, _ = x.shape
    ang = axial_rope_angles(S)
    h_ctx = ctx
    for i in range(DEPTH):
        kind = i % N_MIXERS
        j = i // N_MIXERS
        reads_ctx = kind in CTX_READERS
        ctx_after = any((l % N_MIXERS) in CTX_READERS for l in range(i + 1, DEPTH))
        ctx_live = reads_ctx or ctx_after
        mx = ada_params(c, mod_w[i], mod_b[i])
        x = x + 0.5 * mx[2] * swiglu(modulate(x, mx[0], mx[1]), ffn_w_in[i, 0], ffn_w_out[i, 0])
        hx = modulate(x, mx[3], mx[4])
        hc = None
        if ctx_live:
            mc = ada_params(c_ctx[None, :], mod_w[i], mod_b[i])
            h_ctx = h_ctx + 0.5 * mc[2] * swiglu(modulate(h_ctx, mc[0], mc[1]), ffn_w_in[i, 0], ffn_w_out[i, 0])
            hc = modulate(h_ctx, mc[3], mc[4])
        if kind == 0:
            yx, yc = mla_mixer(hx, hc, ang, ctx_after, mla_w_dq[j], mla_g_dq[j], mla_w_uq[j], mla_w_dkv[j],
                               mla_g_dkv[j], mla_w_uk[j], mla_w_uv[j], mla_g_qn[j], mla_g_qr[j],
                               mla_g_kn[j], mla_g_kr[j], mla_w_o[j])
        elif kind == 1:
            yx = pool_mixer(hx, pool_w[j], pool_scale[j])
            yc = pool_mixer(hc, pool_w[j], pool_scale[j]) if ctx_after else None
        elif kind == 2:
            yx, yc = na_mixer(hx, hc, ctx_after, na_w_qkv[j], na_g_q[j], na_g_k[j], na_rpb[j], na_w_o[j])
        else:
            yx = conv_mixer(hx, conv_w_in[j], conv_w[j], conv_w_out[j])
            yc = conv_mixer(hc, conv_w_in[j], conv_w[j], conv_w_out[j]) if ctx_after else None
        x = x + mx[5] * yx
        x = x + 0.5 * mx[8] * swiglu(modulate(x, mx[6], mx[7]), ffn_w_in[i, 1], ffn_w_out[i, 1])
        if ctx_after:
            h_ctx = h_ctx + mc[5] * yc
            h_ctx = h_ctx + 0.5 * mc[8] * swiglu(modulate(h_ctx, mc[6], mc[7]), ffn_w_in[i, 1], ffn_w_out[i, 1])
    return x
```

```python
import functools

import jax
import jax.numpy as jnp
import numpy as np
from jax import lax
from jax.experimental import pallas as pl
from jax.experimental.pallas import tpu as pltpu

F32 = jnp.float32
BF16 = jnp.bfloat16

D_MODEL = 1024
DEPTH = 4
GRID_W = 64
N_MOD = 9
D_FF = 2816
EPS = 1e-6
MLA_HEADS = 8
MLA_Q_LORA = 384
MLA_KV_LORA = 256
MLA_NOPE = 128
MLA_ROPE = 64
MLA_V = 128
MLA_SCALE = (MLA_NOPE + MLA_ROPE) ** -0.5
ROPE_BASE = 10000.0
POOL_WINDOWS = (2, 4, 8, 16)
POOL_GROUP = D_MODEL // len(POOL_WINDOWS)
NA_HEADS = 16
NA_HEAD_DIM = D_MODEL // NA_HEADS
NA_ROWS = 8
NA_COLS = 16
NA_SCALE = NA_HEAD_DIM ** -0.5

LANES = 128
MXU_COLS = 256
VMEM_LIMIT = 56 * 1024 * 1024
ROW_TILE = 512
FF_CHUNK = MXU_COLS
POOL_HALO = 8
CONV_HALO = 16
NA_BLOCK_ROWS = 8
NEG_BIG = -1e30

_NT = (((1,), (1,)), ((), ()))


def _params(n_axes):
    return pltpu.CompilerParams(dimension_semantics=("arbitrary",) * n_axes, vmem_limit_bytes=VMEM_LIMIT)


def _const_spec(shape):
    nd = len(shape)
    return pl.BlockSpec(shape, lambda *_: (0,) * nd, pipeline_mode=pl.Buffered(1))


def _rms(x):
    return x * lax.rsqrt(jnp.mean(x * x, axis=-1, keepdims=True) + EPS)


def _modulate(x, mod_ref, row):
    shift = mod_ref[row:row + 1, :]
    scale = mod_ref[row + 1:row + 2, :]
    return _rms(x) * (1.0 + scale) + shift


def _dot(a, b):
    return jnp.dot(a, b, preferred_element_type=F32)


def _ada_kernel(cond_ref, w_ref, b_ref, o_ref):
    cnd = cond_ref[...]
    s = (cnd * (1.0 / (1.0 + jnp.exp(-cnd)))).astype(BF16)
    o_ref[...] = _dot(s, w_ref[...].astype(BF16)) + b_ref[...]


def _ada_params(cond, mod_w, mod_b):
    depth, d, n = mod_w.shape
    tn = n // 8
    return pl.pallas_call(
        _ada_kernel,
        grid=(depth, n // tn),
        in_specs=[
            pl.BlockSpec((8, d), lambda i, j: (0, 0)),
            pl.BlockSpec((None, d, tn), lambda i, j: (i, 0, j)),
            pl.BlockSpec((None, 1, tn), lambda i, j: (i, 0, j)),
        ],
        out_specs=pl.BlockSpec((None, 8, tn), lambda i, j: (i, 0, j)),
        out_shape=jax.ShapeDtypeStruct((depth, 8, n), F32),
        compiler_params=_params(2),
        name="ada_params",
    )(cond, mod_w, mod_b.reshape(depth, 1, n))


def _ffn_kernel(has_pro, mod_row, *refs):
    if has_pro:
        x_ref, y_ref, wp_ref, gp_ref, mod_ref, wi_ref, wo_ref, o_ref, a_ref = refs
    else:
        x_ref, mod_ref, wi_ref, wo_ref, o_ref, a_ref = refs
    x = x_ref[...]
    if has_pro:
        x = x + gp_ref[...] * _dot(y_ref[...], wp_ref[...])
    h = _modulate(x, mod_ref, mod_row).astype(BF16)
    for j in range(D_FF // FF_CHUNK):
        g = _dot(h, wi_ref[:, j * FF_CHUNK:(j + 1) * FF_CHUNK])
        u = _dot(h, wi_ref[:, D_FF + j * FF_CHUNK:D_FF + (j + 1) * FF_CHUNK])
        a_ref[:, j * FF_CHUNK:(j + 1) * FF_CHUNK] = (g * (1.0 / (1.0 + jnp.exp(-g))) * u).astype(BF16)
    gate = mod_ref[mod_row + 2:mod_row + 3, :]
    o_ref[...] = x + (0.5 * gate) * _dot(a_ref[...], wo_ref[...])


def _ffn(x, mod, mod_row, w_in, w_out, pro=None):
    t, d = x.shape
    tm = min(ROW_TILE, t)
    row = lambda i: (i, 0)
    in_specs = [pl.BlockSpec((tm, d), row)]
    args = [x]
    if pro is not None:
        y, w_p, g_p = pro
        in_specs += [pl.BlockSpec((tm, d), row), _const_spec(w_p.shape), _const_spec(g_p.shape)]
        args += [y, w_p, g_p]
    in_specs += [_const_spec(mod.shape), _const_spec(w_in.shape), _const_spec(w_out.shape)]
    args += [mod, w_in, w_out]
    return pl.pallas_call(
        functools.partial(_ffn_kernel, pro is not None, mod_row),
        grid=(t // tm,),
        in_specs=in_specs,
        out_specs=pl.BlockSpec((tm, d), row),
        out_shape=jax.ShapeDtypeStruct((t, d), F32),
        scratch_shapes=[pltpu.VMEM((tm, D_FF), BF16)],
        compiler_params=_params(1),
        name="ffn_pro" if pro is not None else "ffn",
    )(*args)


def _rope_pair(r2, gain, cs, first_half):
    ms = jnp.sum(jnp.where(first_half, r2 * r2, 0.0), axis=-1, keepdims=True) * (1.0 / MLA_ROPE)
    t = r2 * lax.rsqrt(ms + EPS) * gain * cs
    return t + pltpu.roll(t, MLA_ROPE, axis=1)


def _mla_proj_kernel(x_ref, mod_ref, cs_ref, wdq_ref, gdq_ref, wuq_ref, gq_ref, wdkv_ref, gdkv_ref, wuk_ref,
                     gk_ref, wuvt_ref, q_ref, k_ref, vt_ref):
    h = _modulate(x_ref[...], mod_ref, 3).astype(BF16)
    cs = cs_ref[...]
    first_half = lax.broadcasted_iota(jnp.int32, (1, LANES), 1) < MLA_ROPE
    cq = (_rms(_dot(h, wdq_ref[...])) * gdq_ref[...]).astype(BF16)
    q = _dot(cq, wuq_ref[...])
    for hd in range(MLA_HEADS):
        c0 = hd * 2 * LANES
        qn = q[:, c0:c0 + LANES]
        q_ref[hd, :, 0:LANES] = (_rms(qn) * gq_ref[:, c0:c0 + LANES]).astype(BF16)
        rot = _rope_pair(q[:, c0 + LANES:c0 + 2 * LANES], gq_ref[:, c0 + LANES:c0 + 2 * LANES], cs, first_half)
        q_ref[hd, :, LANES:2 * LANES] = rot.astype(BF16)
    kv = _dot(h, wdkv_ref[...])
    ckv = (_rms(kv[:, :MLA_KV_LORA]) * gdkv_ref[...]).astype(BF16)
    rot = _rope_pair(kv[:, MLA_KV_LORA:], gk_ref[:, LANES:2 * LANES], cs, first_half)
    kr = jnp.where(first_half, rot, 0.0).astype(BF16)
    kn = _dot(ckv, wuk_ref[...])
    vt = lax.dot_general(wuvt_ref[...], ckv, _NT, preferred_element_type=F32)
    for hd in range(MLA_HEADS):
        blk = kn[:, hd * LANES:(hd + 1) * LANES]
        k_ref[hd, :, 0:LANES] = (_rms(blk) * gk_ref[:, 0:LANES]).astype(BF16)
        k_ref[hd, :, LANES:2 * LANES] = kr
        vt_ref[hd] = vt[hd * MLA_V:(hd + 1) * MLA_V, :].astype(BF16)


def _mla_proj(x, mod, cs, w):
    t, d = x.shape
    tm = min(ROW_TILE, t)
    consts = [w["w_dq"], w["g_dq"], w["w_uq"], w["g_q"], w["w_dkv"], w["g_dkv"], w["w_uk"], w["g_k"], w["w_uvt"]]
    return pl.pallas_call(
        _mla_proj_kernel,
        grid=(t // tm,),
        in_specs=[pl.BlockSpec((tm, d), lambda i: (i, 0)), _const_spec(mod.shape),
                  pl.BlockSpec((tm, LANES), lambda i: (i, 0))] + [_const_spec(a.shape) for a in consts],
        out_specs=[
            pl.BlockSpec((MLA_HEADS, tm, 2 * LANES), lambda i: (0, i, 0)),
            pl.BlockSpec((MLA_HEADS, tm, 2 * LANES), lambda i: (0, i, 0)),
            pl.BlockSpec((MLA_HEADS, None, MLA_V, tm), lambda i: (0, i, 0, 0)),
        ],
        out_shape=[
            jax.ShapeDtypeStruct((MLA_HEADS, t, 2 * LANES), BF16),
            jax.ShapeDtypeStruct((MLA_HEADS, t, 2 * LANES), BF16),
            jax.ShapeDtypeStruct((MLA_HEADS, t // tm, MLA_V, tm), BF16),
        ],
        compiler_params=_params(1),
        name="mla_proj",
    )(x, mod, cs, *consts)


def _mla_attn_kernel(n_main, sub, tv, has_extra, *refs):
    if has_extra:
        q_ref, k_ref, vt_ref, ke_ref, vte_ref, o_ref, acc_ref = refs
    else:
        q_ref, k_ref, vt_ref, o_ref, acc_ref = refs
    q = q_ref[...]
    tq = q.shape[0]
    tk = sub * tv
    acc_ref[...] = jnp.zeros_like(acc_ref)

    def update(carry, k, vts):
        m, l = carry
        s = lax.dot_general(k, q, _NT, preferred_element_type=F32)
        m_new = jnp.maximum(m, jnp.max(s, axis=0, keepdims=True))
        alpha = jnp.exp(m - m_new)
        p = jnp.exp(s - m_new)
        l = alpha * l + jnp.sum(p, axis=0, keepdims=True)
        pb = p.astype(BF16)
        pv = _dot(vts[0], pb[0:vts[0].shape[1], :])
        for c in range(1, len(vts)):
            pv = pv + _dot(vts[c], pb[c * tv:(c + 1) * tv, :])
        acc_ref[...] = alpha * acc_ref[...] + pv
        return m_new, l

    def body(j, carry):
        k = k_ref[pl.ds(pl.multiple_of(j * tk, tk), tk), :]
        return update(carry, k, [vt_ref[j * sub + c] for c in range(sub)])

    carry = (jnp.full((1, tq), NEG_BIG, F32), jnp.zeros((1, tq), F32))
    carry = lax.fori_loop(0, n_main, body, carry)
    if has_extra:
        carry = update(carry, ke_ref[...], [vte_ref[...]])
    o_ref[...] = (acc_ref[...] * (1.0 / carry[1])).T.astype(BF16)


def _mla_attn(q, k, vt, extra=None):
    nh, tq_all, dk = q.shape
    tk_all = k.shape[1]
    n_v, tv = vt.shape[1], vt.shape[3]
    sub = 2 if n_v % 2 == 0 else 1
    tq = min(ROW_TILE, tq_all)
    in_specs = [
        pl.BlockSpec((None, tq, dk), lambda h, i: (h, i, 0)),
        pl.BlockSpec((None, tk_all, dk), lambda h, i: (h, 0, 0)),
        pl.BlockSpec((None, n_v, MLA_V, tv), lambda h, i: (h, 0, 0, 0)),
    ]
    args = [q, k, vt]
    if extra is not None:
        k_e, vt_e = extra
        in_specs += [pl.BlockSpec((None,) + k_e.shape[1:], lambda h, i: (h, 0, 0)),
                     pl.BlockSpec((None, None) + vt_e.shape[2:], lambda h, i: (h, 0, 0, 0))]
        args += [k_e, vt_e]
    return pl.pallas_call(
        functools.partial(_mla_attn_kernel, n_v // sub, sub, tv, extra is not None),
        grid=(nh, tq_all // tq),
        in_specs=in_specs,
        out_specs=pl.BlockSpec((tq, MLA_V), lambda h, i: (i, h)),
        out_shape=jax.ShapeDtypeStruct((tq_all, nh * MLA_V), BF16),
        scratch_shapes=[pltpu.VMEM((MLA_V, tq), F32)],
        compiler_params=_params(2),
        name="mla_attn",
    )(*args)


def _pool_kernel(t_total, x_ref, xp_ref, xn_ref, mod_ref, o_ref, ext_ref):
    tm = x_ref.shape[0]
    base = pl.program_id(0) * tm
    hc = _modulate(x_ref[...], mod_ref, 3)
    halo_rows = lax.broadcasted_iota(jnp.int32, (POOL_HALO, 1), 0)
    hp = jnp.where(base - POOL_HALO + halo_rows >= 0, _modulate(xp_ref[...], mod_ref, 3), 0.0)
    hn = jnp.where(base + tm + halo_rows < t_total, _modulate(xn_ref[...], mod_ref, 3), 0.0)
    ext_ref[0:POOL_HALO, :] = hp
    ext_ref[POOL_HALO:POOL_HALO + tm, :] = hc
    ext_ref[POOL_HALO + tm:, :] = hn
    tok = base + lax.broadcasted_iota(jnp.int32, (tm, 1), 0)
    for g, win in enumerate(POOL_WINDOWS):
        half = win // 2
        cols = slice(g * POOL_GROUP, (g + 1) * POOL_GROUP)
        acc = ext_ref[POOL_HALO - half:POOL_HALO - half + tm, cols]
        for j in range(-half + 1, half):
            acc = acc + ext_ref[POOL_HALO + j:POOL_HALO + j + tm, cols]
        cnt = (jnp.minimum(tok + half, t_total) - jnp.maximum(tok - half, 0)).astype(F32)
        o_ref[:, cols] = (acc / cnt - hc[:, cols]).astype(BF16)


def _pool(x, mod):
    t, d = x.shape
    tm = min(ROW_TILE, t)
    per = tm // POOL_HALO
    last = t // POOL_HALO - 1
    return pl.pallas_call(
        functools.partial(_pool_kernel, t),
        grid=(t // tm,),
        in_specs=[
            pl.BlockSpec((tm, d), lambda i: (i, 0)),
            pl.BlockSpec((POOL_HALO, d), lambda i: (jnp.maximum(i * per - 1, 0), 0)),
            pl.BlockSpec((POOL_HALO, d), lambda i: (jnp.minimum((i + 1) * per, last), 0)),
            _const_spec(mod.shape),
        ],
        out_specs=pl.BlockSpec((tm, d), lambda i: (i, 0)),
        out_shape=jax.ShapeDtypeStruct((t, d), BF16),
        scratch_shapes=[pltpu.VMEM((tm + 2 * POOL_HALO, d), F32)],
        compiler_params=_params(1),
        name="pool",
    )(x, x, x, mod)


def _conv_kernel(t_total, x_ref, xp_ref, xn_ref, mod_ref, wb_ref, wcu_ref, wconv_ref, o_ref, ext_ref, cu_ref):
    tm, d = x_ref.shape
    base = pl.program_id(0) * tm
    ext_ref[0:CONV_HALO, :] = _modulate(xp_ref[...], mod_ref, 3).astype(BF16)
    ext_ref[CONV_HALO:CONV_HALO + tm, :] = _modulate(x_ref[...], mod_ref, 3).astype(BF16)
    ext_ref[CONV_HALO + tm:, :] = _modulate(xn_ref[...], mod_ref, 3).astype(BF16)
    ext = ext_ref[...]
    cu = _dot(ext, wcu_ref[:, 0:d]) * _dot(ext, wcu_ref[:, d:2 * d])
    tok = base - CONV_HALO + lax.broadcasted_iota(jnp.int32, (tm + 2 * CONV_HALO, 1), 0)
    cu_ref[...] = jnp.where((tok >= 0) & (tok < t_total), cu, 0.0)
    z = (wconv_ref[0:1, :] * cu_ref[CONV_HALO - 1:CONV_HALO - 1 + tm, :]
         + wconv_ref[1:2, :] * cu_ref[CONV_HALO:CONV_HALO + tm, :]
         + wconv_ref[2:3, :] * cu_ref[CONV_HALO + 1:CONV_HALO + 1 + tm, :])
    b = _dot(ext_ref[CONV_HALO:CONV_HALO + tm, :], wb_ref[...])
    o_ref[...] = (b * z).astype(BF16)


def _conv(x, mod, w_b, w_cu, w_conv):
    t, d = x.shape
    tm = min(ROW_TILE, t)
    per = tm // CONV_HALO
    last = t // CONV_HALO - 1
    return pl.pallas_call(
        functools.partial(_conv_kernel, t),
        grid=(t // tm,),
        in_specs=[
            pl.BlockSpec((tm, d), lambda i: (i, 0)),
            pl.BlockSpec((CONV_HALO, d), lambda i: (jnp.maximum(i * per - 1, 0), 0)),
            pl.BlockSpec((CONV_HALO, d), lambda i: (jnp.minimum((i + 1) * per, last), 0)),
            _const_spec(mod.shape), _const_spec(w_b.shape), _const_spec(w_cu.shape), _const_spec(w_conv.shape),
        ],
        out_specs=pl.BlockSpec((tm, d), lambda i: (i, 0)),
        out_shape=jax.ShapeDtypeStruct((t, d), BF16),
        scratch_shapes=[pltpu.VMEM((tm + 2 * CONV_HALO, d), BF16), pltpu.VMEM((tm + 2 * CONV_HALO, d), F32)],
        compiler_params=_params(1),
        name="conv",
    )(x, x, x, mod, w_b, w_cu, w_conv)


def _head_rms(v, first_half):
    outs = []
    for c in range(v.shape[1] // LANES):
        blk = v[:, c * LANES:(c + 1) * LANES]
        sq = blk * blk
        tot = jnp.sum(sq, axis=-1, keepdims=True)
        lo = jnp.sum(jnp.where(first_half, sq, 0.0), axis=-1, keepdims=True)
        ms = jnp.where(first_half, lo, tot - lo) * (1.0 / NA_HEAD_DIM)
        outs.append(blk * lax.rsqrt(ms + EPS))
    return outs


def _na_proj_kernel(x_ref, mod_ref, w_ref, gq_ref, gk_ref, q_ref, k_ref, v_ref):
    d = x_ref.shape[1]
    h = _modulate(x_ref[...], mod_ref, 3).astype(BF16)
    first_half = lax.broadcasted_iota(jnp.int32, (1, LANES), 1) < NA_HEAD_DIM
    for c, blk in enumerate(_head_rms(_dot(h, w_ref[:, 0:d]), first_half)):
        q_ref[:, c * LANES:(c + 1) * LANES] = (blk * gq_ref[...]).astype(BF16)
    for c, blk in enumerate(_head_rms(_dot(h, w_ref[:, d:2 * d]), first_half)):
        k_ref[:, c * LANES:(c + 1) * LANES] = (blk * gk_ref[...]).astype(BF16)
    v_ref[...] = _dot(h, w_ref[:, 2 * d:3 * d]).astype(BF16)


def _na_proj(x, mod, w_qkv, g_q2, g_k2):
    t, d = x.shape
    tm = min(ROW_TILE, t)
    spec = pl.BlockSpec((tm, d), lambda i: (i, 0))
    return pl.pallas_call(
        _na_proj_kernel,
        grid=(t // tm,),
        in_specs=[spec, _const_spec(mod.shape), _const_spec(w_qkv.shape), _const_spec(g_q2.shape),
                  _const_spec(g_k2.shape)],
        out_specs=[spec, spec, spec],
        out_shape=[jax.ShapeDtypeStruct((t, d), BF16)] * 3,
        compiler_params=_params(1),
        name="na_proj",
    )(x, mod, w_qkv, g_q2, g_k2)


def _na_attn_kernel(n_rows, q_ref, kp_ref, kc_ref, kn_ref, vp_ref, vc_ref, vn_ref, kx_ref, vx_ref, bias_ref, o_ref,
                    kbuf_ref, vbuf_ref):
    blk = NA_BLOCK_ROWS * GRID_W
    band = NA_ROWS * GRID_W
    i = pl.program_id(0)
    kbuf_ref[0:blk, :] = kp_ref[...]
    kbuf_ref[blk:2 * blk, :] = kc_ref[...]
    kbuf_ref[2 * blk:3 * blk, :] = kn_ref[...]
    vbuf_ref[0:blk, :] = vp_ref[...]
    vbuf_ref[blk:2 * blk, :] = vc_ref[...]
    vbuf_ref[2 * blk:3 * blk, :] = vn_ref[...]
    first_half = lax.broadcasted_iota(jnp.int32, (1, LANES), 1) < NA_HEAD_DIM

    def row_body(j, _):
        r = i * NA_BLOCK_ROWS + j
        rs = jnp.clip(r - NA_ROWS // 2, 0, n_rows - NA_ROWS)
        case = r - rs
        off = pl.multiple_of((rs - (i - 1) * NA_BLOCK_ROWS) * GRID_W, GRID_W)
        qrow = q_ref[pl.ds(pl.multiple_of(j * GRID_W, GRID_W), GRID_W), :]
        for p in range(NA_HEADS // 2):
            cols = slice(p * LANES, (p + 1) * LANES)
            qp = qrow[:, cols]
            kb = kbuf_ref[pl.ds(off, band), cols]
            vb = vbuf_ref[pl.ds(off, band), cols]
            kx = kx_ref[:, cols]
            vx = vx_ref[:, cols]
            halves = []
            for sel in range(2):
                qh = jnp.where(first_half if sel == 0 else jnp.logical_not(first_half), qp, jnp.zeros_like(qp))
                s_loc = lax.dot_general(qh, kb, _NT, preferred_element_type=F32) + bias_ref[case, 2 * p + sel]
                s_ctx = lax.dot_general(qh, kx, _NT, preferred_element_type=F32)
                m = jnp.maximum(jnp.max(s_loc, axis=-1, keepdims=True), jnp.max(s_ctx, axis=-1, keepdims=True))
                p_loc = jnp.exp(s_loc - m)
                p_ctx = jnp.exp(s_ctx - m)
                denom = jnp.sum(p_loc, axis=-1, keepdims=True) + jnp.sum(p_ctx, axis=-1, keepdims=True)
                o = _dot(p_loc.astype(BF16), vb) + _dot(p_ctx.astype(BF16), vx)
                halves.append(o * (1.0 / denom))
            o_ref[pl.ds(pl.multiple_of(j * GRID_W, GRID_W), GRID_W), cols] = jnp.where(
                first_half, halves[0], halves[1]).astype(BF16)
        return 0

    lax.fori_loop(0, NA_BLOCK_ROWS, row_body, 0)


def _na_attn(q, k, v, k_ctx, v_ctx, bias):
    s, d = q.shape
    blk = NA_BLOCK_ROWS * GRID_W
    n_blk = s // blk
    cur = pl.BlockSpec((blk, d), lambda i: (i, 0))
    prev = pl.BlockSpec((blk, d), lambda i: (jnp.maximum(i - 1, 0), 0))
    nxt = pl.BlockSpec((blk, d), lambda i: (jnp.minimum(i + 1, n_blk - 1), 0))
    return pl.pallas_call(
        functools.partial(_na_attn_kernel, s // GRID_W),
        grid=(n_blk,),
        in_specs=[cur, prev, cur, nxt, prev, cur, nxt, _const_spec(k_ctx.shape), _const_spec(v_ctx.shape),
                  _const_spec(bias.shape)],
        out_specs=cur,
        out_shape=jax.ShapeDtypeStruct((s, d), BF16),
        scratch_shapes=[pltpu.VMEM((3 * blk, d), BF16), pltpu.VMEM((3 * blk, d), BF16)],
        compiler_params=_params(1),
        name="na_attn",
    )(q, k, k, k, v, v, v, k_ctx, v_ctx, bias)


def _na_bias_tables(rpb):
    cols = np.arange(GRID_W)
    col_start = np.clip(cols - NA_COLS // 2, 0, GRID_W - NA_COLS)
    kc = np.arange(GRID_W)
    valid = (kc[None, :] >= col_start[:, None]) & (kc[None, :] < col_start[:, None] + NA_COLS)
    dc = np.clip(kc[None, :] - cols[:, None] + (NA_COLS - 1), 0, 2 * NA_COLS - 2)
    dr = (np.arange(NA_ROWS)[None, :] - np.arange(NA_ROWS)[:, None]) + (NA_ROWS - 1)
    tab = rpb[:, dr[:, :, None, None], dc[None, None, :, :]]
    tab = jnp.where(valid[None, None, None], tab, NEG_BIG)
    tab = jnp.transpose(tab, (1, 0, 3, 2, 4))
    return tab.reshape(NA_ROWS, rpb.shape[0], GRID_W, NA_ROWS * GRID_W)


def _rope_tables(t):
    pos = jnp.arange(t)
    row = (pos // GRID_W).astype(F32)
    col = (pos % GRID_W).astype(F32)
    n = MLA_ROPE // 4
    freqs = ROPE_BASE ** (-jnp.arange(n, dtype=F32) / n)
    ang = jnp.concatenate([row[:, None] * freqs, col[:, None] * freqs], axis=-1)
    cos, sin = jnp.cos(ang), jnp.sin(ang)
    return jnp.concatenate([cos, cos, sin, sin], axis=-1)


_HALF_SPLIT = np.concatenate([np.arange(0, MLA_ROPE, 2), np.arange(1, MLA_ROPE, 2)])


def _rope_cols(w):
    hs = w[..., _HALF_SPLIT]
    return jnp.concatenate([hs, -hs[..., MLA_ROPE // 2:], hs[..., :MLA_ROPE // 2]], axis=-1)


def _rope_gain(g):
    hs = g[_HALF_SPLIT]
    return jnp.concatenate([hs, hs[MLA_ROPE // 2:], hs[:MLA_ROPE // 2]])


def _mla_weights(w_dq, g_dq, w_uq, w_dkv, g_dkv, w_uk, w_uv, g_qn, g_qr, g_kn, g_kr):
    w_uq_ext = jnp.concatenate([w_uq[..., :MLA_NOPE], _rope_cols(w_uq[..., MLA_NOPE:])], axis=-1)
    g_q = jnp.tile(jnp.concatenate([g_qn, _rope_gain(g_qr)]) * MLA_SCALE, MLA_HEADS)
    return {
        "w_dq": w_dq.astype(BF16),
        "g_dq": g_dq[None, :],
        "w_uq": w_uq_ext.reshape(MLA_Q_LORA, -1).astype(BF16),
        "g_q": g_q[None, :],
        "w_dkv": jnp.concatenate([w_dkv[:, :MLA_KV_LORA], _rope_cols(w_dkv[:, MLA_KV_LORA:])], axis=-1).astype(BF16),
        "g_dkv": g_dkv[None, :],
        "w_uk": w_uk.reshape(MLA_KV_LORA, -1).astype(BF16),
        "g_k": jnp.concatenate([g_kn, _rope_gain(g_kr)])[None, :],
        "w_uvt": w_uv.reshape(MLA_KV_LORA, -1).T.astype(BF16),
    }


def _block_diag(w):
    g, c, _ = w.shape
    out = jnp.zeros((g * c, g * c), w.dtype)
    for i in range(g):
        out = out.at[i * c:(i + 1) * c, i * c:(i + 1) * c].set(w[i])
    return out


def kernel(x, c, ctx, c_ctx, mod_w, mod_b, ffn_w_in, ffn_w_out, mla_w_dq, mla_g_dq, mla_w_uq, mla_w_dkv, mla_g_dkv, mla_w_uk, mla_w_uv, mla_g_qn, mla_g_qr, mla_g_kn, mla_g_kr, mla_w_o, pool_w, pool_scale, na_w_qkv, na_g_q, na_g_k, na_rpb, na_w_o, conv_w_in, conv_w, conv_w_out):
    assert x.shape[0] == 1 and x.shape[2] == D_MODEL and x.shape[1] % (NA_BLOCK_ROWS * GRID_W) == 0
    s = x.shape[1]
    d = D_MODEL
    xs = x[0]
    hc = ctx[0]
    n_ctx = hc.shape[0]

    cond = jnp.zeros((8, d), F32).at[0].set(c[0]).at[1].set(c_ctx)
    mods = _ada_params(cond, mod_w, mod_b)
    w_in = ffn_w_in.astype(BF16)
    w_out = ffn_w_out.astype(BF16)

    mx = mods[0, 0].reshape(N_MOD, d)
    mc = mods[0, 1].reshape(N_MOD, d)
    xs = _ffn(xs, mx, 0, w_in[0, 0], w_out[0, 0])
    hc = _ffn(hc, mc, 0, w_in[0, 0], w_out[0, 0])
    mw = _mla_weights(mla_w_dq[0], mla_g_dq[0], mla_w_uq[0], mla_w_dkv[0], mla_g_dkv[0], mla_w_uk[0], mla_w_uv[0],
                      mla_g_qn[0], mla_g_qr[0], mla_g_kn[0], mla_g_kr[0])
    no_rope = jnp.concatenate([jnp.ones((n_ctx, LANES // 2), F32), jnp.zeros((n_ctx, LANES // 2), F32)], axis=-1)
    q_x, k_x, vt_x = _mla_proj(xs, mx, _rope_tables(s), mw)
    q_c, k_c, vt_c = _mla_proj(hc, mc, no_rope, mw)
    o_x = _mla_attn(q_x, k_x, vt_x, extra=(k_c, vt_c))
    o_c = _mla_attn(q_c, k_c, vt_c)
    w_o = mla_w_o[0].astype(BF16)
    xs = _ffn(xs, mx, 6, w_in[0, 1], w_out[0, 1], pro=(o_x, w_o, mx[5:6]))
    hc = _ffn(hc, mc, 6, w_in[0, 1], w_out[0, 1], pro=(o_c, w_o, mc[5:6]))

    mx = mods[1, 0].reshape(N_MOD, d)
    mc = mods[1, 1].reshape(N_MOD, d)
    xs = _ffn(xs, mx, 0, w_in[1, 0], w_out[1, 0])
    hc = _ffn(hc, mc, 0, w_in[1, 0], w_out[1, 0])
    w_p = _block_diag(pool_w[0]).astype(BF16)
    xs = _ffn(xs, mx, 6, w_in[1, 1], w_out[1, 1], pro=(_pool(xs, mx), w_p, mx[5:6] * pool_scale[0][None, :]))
    hc = _ffn(hc, mc, 6, w_in[1, 1], w_out[1, 1], pro=(_pool(hc, mc), w_p, mc[5:6] * pool_scale[0][None, :]))

    mx = mods[2, 0].reshape(N_MOD, d)
    mc = mods[2, 1].reshape(N_MOD, d)
    xs = _ffn(xs, mx, 0, w_in[2, 0], w_out[2, 0])
    hc = _ffn(hc, mc, 0, w_in[2, 0], w_out[2, 0])
    w_qkv = na_w_qkv[0].astype(BF16)
    g_q2 = jnp.tile(na_g_q[0] * NA_SCALE, 2)[None, :]
    g_k2 = jnp.tile(na_g_k[0], 2)[None, :]
    q_n, k_n, v_n = _na_proj(xs, mx, w_qkv, g_q2, g_k2)
    _, k_nc, v_nc = _na_proj(hc, mc, w_qkv, g_q2, g_k2)
    o_n = _na_attn(q_n, k_n, v_n, k_nc, v_nc, _na_bias_tables(na_rpb[0]))
    xs = _ffn(xs, mx, 6, w_in[2, 1], w_out[2, 1], pro=(o_n, na_w_o[0].astype(BF16), mx[5:6]))

    mx = mods[3, 0].reshape(N_MOD, d)
    xs = _ffn(xs, mx, 0, w_in[3, 0], w_out[3, 0])
    w_ci = conv_w_in[0].astype(BF16)
    y_c = _conv(xs, mx, w_ci[:, :d], w_ci[:, d:], conv_w[0])
    xs = _ffn(xs, mx, 6, w_in[3, 1], w_out[3, 1], pro=(y_c, conv_w_out[0].astype(BF16), mx[5:6]))
    return xs[None]
```

```python
import functools

import jax
import jax.numpy as jnp
import numpy as np
from jax import lax
from jax.experimental import pallas as pl
from jax.experimental.pallas import tpu as pltpu

F32 = jnp.float32
BF16 = jnp.bfloat16

D_MODEL = 1024
DEPTH = 4
GRID_W = 64
N_MOD = 9
D_FF = 2816
EPS = 1e-6
MLA_HEADS = 8
MLA_Q_LORA = 384
MLA_KV_LORA = 256
MLA_NOPE = 128
MLA_ROPE = 64
MLA_V = 128
MLA_SCALE = (MLA_NOPE + MLA_ROPE) ** -0.5
ROPE_BASE = 10000.0
POOL_WINDOWS = (2, 4, 8, 16)
POOL_GROUP = D_MODEL // len(POOL_WINDOWS)
NA_HEADS = 16
NA_HEAD_DIM = D_MODEL // NA_HEADS
NA_ROWS = 8
NA_COLS = 16
NA_SCALE = NA_HEAD_DIM ** -0.5

LANES = 128
MXU_COLS = 256
VMEM_LIMIT = 56 * 1024 * 1024
ROW_TILE = 512
MLA_Q_TILE = 1024
FF_CHUNK = MXU_COLS
POOL_HALO = 8
CONV_HALO = 16
NA_BLOCK_ROWS = 8
NEG_BIG = -1e30
LOG2_E = 1.4426950408889634

_NT = (((1,), (1,)), ((), ()))


def _params(n_axes):
    return pltpu.CompilerParams(dimension_semantics=("arbitrary",) * n_axes, vmem_limit_bytes=VMEM_LIMIT)


def _const_spec(shape):
    nd = len(shape)
    return pl.BlockSpec(shape, lambda *_: (0,) * nd, pipeline_mode=pl.Buffered(1))


def _rms(x):
    return x * lax.rsqrt(jnp.mean(x * x, axis=-1, keepdims=True) + EPS)


def _modulate(x, mod_ref, row):
    shift = mod_ref[row:row + 1, :]
    scale = mod_ref[row + 1:row + 2, :]
    return _rms(x) * (1.0 + scale) + shift


def _dot(a, b):
    return jnp.dot(a, b, preferred_element_type=F32)


def _ada_kernel(cond_ref, w_ref, b_ref, o_ref):
    cnd = cond_ref[...]
    s = (cnd * (1.0 / (1.0 + jnp.exp(-cnd)))).astype(BF16)
    o_ref[...] = _dot(s, w_ref[...].astype(BF16)) + b_ref[...]


def _ada_params(cond, mod_w, mod_b):
    depth, d, n = mod_w.shape
    tn = n // 8
    return pl.pallas_call(
        _ada_kernel,
        grid=(depth, n // tn),
        in_specs=[
            pl.BlockSpec((8, d), lambda i, j: (0, 0)),
            pl.BlockSpec((None, d, tn), lambda i, j: (i, 0, j)),
            pl.BlockSpec((None, 1, tn), lambda i, j: (i, 0, j)),
        ],
        out_specs=pl.BlockSpec((None, 8, tn), lambda i, j: (i, 0, j)),
        out_shape=jax.ShapeDtypeStruct((depth, 8, n), F32),
        compiler_params=_params(2),
        name="ada_params",
    )(cond, mod_w, mod_b.reshape(depth, 1, n))


def _ffn_kernel(has_pro, mod_row, *refs):
    if has_pro:
        x_ref, y_ref, wp_ref, gp_ref, mod_ref, wi_ref, wo_ref, o_ref, a_ref = refs
    else:
        x_ref, mod_ref, wi_ref, wo_ref, o_ref, a_ref = refs
    x = x_ref[...]
    if has_pro:
        x = x + gp_ref[...] * _dot(y_ref[...], wp_ref[...])
    h = _modulate(x, mod_ref, mod_row).astype(BF16)
    for j in range(D_FF // FF_CHUNK):
        g = _dot(h, wi_ref[:, j * FF_CHUNK:(j + 1) * FF_CHUNK])
        u = _dot(h, wi_ref[:, D_FF + j * FF_CHUNK:D_FF + (j + 1) * FF_CHUNK])
        a_ref[:, j * FF_CHUNK:(j + 1) * FF_CHUNK] = (g * (1.0 / (1.0 + jnp.exp(-g))) * u).astype(BF16)
    gate = mod_ref[mod_row + 2:mod_row + 3, :]
    o_ref[...] = x + (0.5 * gate) * _dot(a_ref[...], wo_ref[...])


def _ffn(x, mod, mod_row, w_in, w_out, pro=None):
    t, d = x.shape
    tm = min(ROW_TILE, t)
    row = lambda i: (i, 0)
    in_specs = [pl.BlockSpec((tm, d), row)]
    args = [x]
    if pro is not None:
        y, w_p, g_p = pro
        in_specs += [pl.BlockSpec((tm, d), row), _const_spec(w_p.shape), _const_spec(g_p.shape)]
        args += [y, w_p, g_p]
    in_specs += [_const_spec(mod.shape), _const_spec(w_in.shape), _const_spec(w_out.shape)]
    args += [mod, w_in, w_out]
    return pl.pallas_call(
        functools.partial(_ffn_kernel, pro is not None, mod_row),
        grid=(t // tm,),
        in_specs=in_specs,
        out_specs=pl.BlockSpec((tm, d), row),
        out_shape=jax.ShapeDtypeStruct((t, d), F32),
        scratch_shapes=[pltpu.VMEM((tm, D_FF), BF16)],
        compiler_params=_params(1),
        name="ffn_pro" if pro is not None else "ffn",
    )(*args)


def _rope_pair(r2, gain, cs, first_half):
    ms = jnp.sum(jnp.where(first_half, r2 * r2, 0.0), axis=-1, keepdims=True) * (1.0 / MLA_ROPE)
    t = r2 * lax.rsqrt(ms + EPS) * gain * cs
    return t + pltpu.roll(t, MLA_ROPE, axis=1)


def _mla_proj_kernel(x_ref, mod_ref, cs_ref, wdq_ref, gdq_ref, wuq_ref, gq_ref, wdkv_ref, gdkv_ref, wuk_ref,
                     gk_ref, wuvt_ref, q_ref, k_ref, vt_ref):
    h = _modulate(x_ref[...], mod_ref, 3).astype(BF16)
    cs = cs_ref[...]
    first_half = lax.broadcasted_iota(jnp.int32, (1, LANES), 1) < MLA_ROPE
    cq = (_rms(_dot(h, wdq_ref[...])) * gdq_ref[...]).astype(BF16)
    q = _dot(cq, wuq_ref[...])
    for hd in range(MLA_HEADS):
        c0 = hd * 2 * LANES
        qn = q[:, c0:c0 + LANES]
        q_ref[hd, :, 0:LANES] = (_rms(qn) * gq_ref[:, c0:c0 + LANES]).astype(BF16)
        rot = _rope_pair(q[:, c0 + LANES:c0 + 2 * LANES], gq_ref[:, c0 + LANES:c0 + 2 * LANES], cs, first_half)
        q_ref[hd, :, LANES:2 * LANES] = rot.astype(BF16)
    kv = _dot(h, wdkv_ref[...])
    ckv = (_rms(kv[:, :MLA_KV_LORA]) * gdkv_ref[...]).astype(BF16)
    rot = _rope_pair(kv[:, MLA_KV_LORA:], gk_ref[:, LANES:2 * LANES], cs, first_half)
    kr = jnp.where(first_half, rot, 0.0).astype(BF16)
    kn = _dot(ckv, wuk_ref[...])
    vt = lax.dot_general(wuvt_ref[...], ckv, _NT, preferred_element_type=F32)
    for hd in range(MLA_HEADS):
        blk = kn[:, hd * LANES:(hd + 1) * LANES]
        k_ref[hd, :, 0:LANES] = (_rms(blk) * gk_ref[:, 0:LANES]).astype(BF16)
        k_ref[hd, :, LANES:2 * LANES] = kr
        vt_ref[hd] = vt[hd * MLA_V:(hd + 1) * MLA_V, :].astype(BF16)


def _mla_proj(x, mod, cs, w):
    t, d = x.shape
    tm = min(ROW_TILE, t)
    consts = [w["w_dq"], w["g_dq"], w["w_uq"], w["g_q"], w["w_dkv"], w["g_dkv"], w["w_uk"], w["g_k"], w["w_uvt"]]
    return pl.pallas_call(
        _mla_proj_kernel,
        grid=(t // tm,),
        in_specs=[pl.BlockSpec((tm, d), lambda i: (i, 0)), _const_spec(mod.shape),
                  pl.BlockSpec((tm, LANES), lambda i: (i, 0))] + [_const_spec(a.shape) for a in consts],
        out_specs=[
            pl.BlockSpec((MLA_HEADS, tm, 2 * LANES), lambda i: (0, i, 0)),
            pl.BlockSpec((MLA_HEADS, tm, 2 * LANES), lambda i: (0, i, 0)),
            pl.BlockSpec((MLA_HEADS, None, MLA_V, tm), lambda i: (0, i, 0, 0)),
        ],
        out_shape=[
            jax.ShapeDtypeStruct((MLA_HEADS, t, 2 * LANES), BF16),
            jax.ShapeDtypeStruct((MLA_HEADS, t, 2 * LANES), BF16),
            jax.ShapeDtypeStruct((MLA_HEADS, t // tm, MLA_V, tm), BF16),
        ],
        compiler_params=_params(1),
        name="mla_proj",
    )(x, mod, cs, *consts)


def _mla_attn_kernel(n_main, sub, tv, has_extra, *refs):
    if has_extra:
        q_ref, k_ref, vt_ref, ke_ref, vte_ref, o_ref, acc_ref, s_ref = refs
    else:
        q_ref, k_ref, vt_ref, o_ref, acc_ref, s_ref = refs
    q = q_ref[...]
    tq = q.shape[0]
    tk = sub * tv
    acc_ref[...] = jnp.zeros_like(acc_ref)

    def scores(j):
        k = k_ref[pl.ds(pl.multiple_of(j * tk, tk), tk), :]
        return lax.dot_general(k, q, _NT, preferred_element_type=F32)

    def main_vts(j):
        return [vt_ref[j * sub + c] for c in range(sub)]

    def update(carry, s, vts):
        m, l = carry
        m_new = jnp.maximum(m, jnp.max(s, axis=0, keepdims=True))
        alpha = jnp.exp2(m - m_new)
        p = jnp.exp2(s - m_new)
        l = alpha * l + jnp.sum(p, axis=0, keepdims=True)
        pb = p.astype(BF16)
        pv = _dot(vts[0], pb[0:vts[0].shape[1], :])
        for c in range(1, len(vts)):
            pv = pv + _dot(vts[c], pb[c * tv:(c + 1) * tv, :])
        acc_ref[...] = alpha * acc_ref[...] + pv
        return m_new, l

    carry = (jnp.full((1, tq), NEG_BIG, F32), jnp.zeros((1, tq), F32))
    s_ref[0] = scores(0)
    if n_main > 1:
        assert n_main % 2 == 0

        def body(jj, carry):
            j = 2 * jj
            s_ref[1] = scores(j + 1)
            carry = update(carry, s_ref[0], main_vts(j))
            s_ref[0] = scores(j + 2)
            return update(carry, s_ref[1], main_vts(j + 1))

        carry = lax.fori_loop(0, n_main // 2 - 1, body, carry)
        s_ref[1] = scores(n_main - 1)
        carry = update(carry, s_ref[0], main_vts(n_main - 2))
        carry = update(carry, s_ref[1], main_vts(n_main - 1))
    else:
        carry = update(carry, s_ref[0], main_vts(0))
    if has_extra:
        s_e = lax.dot_general(ke_ref[...], q, _NT, preferred_element_type=F32)
        carry = update(carry, s_e, [vte_ref[...]])
    o_ref[...] = (acc_ref[...] * (1.0 / carry[1])).T.astype(BF16)


def _mla_attn(q, k, vt, extra=None):
    nh, tq_all, dk = q.shape
    tk_all = k.shape[1]
    n_v, tv = vt.shape[1], vt.shape[3]
    sub = 2 if n_v % 2 == 0 else 1
    tq = min(MLA_Q_TILE, tq_all)
    in_specs = [
        pl.BlockSpec((None, tq, dk), lambda h, i: (h, i, 0)),
        pl.BlockSpec((None, tk_all, dk), lambda h, i: (h, 0, 0)),
        pl.BlockSpec((None, n_v, MLA_V, tv), lambda h, i: (h, 0, 0, 0)),
    ]
    args = [q, k, vt]
    if extra is not None:
        k_e, vt_e = extra
        in_specs += [pl.BlockSpec((None,) + k_e.shape[1:], lambda h, i: (h, 0, 0)),
                     pl.BlockSpec((None, None) + vt_e.shape[2:], lambda h, i: (h, 0, 0, 0))]
        args += [k_e, vt_e]
    return pl.pallas_call(
        functools.partial(_mla_attn_kernel, n_v // sub, sub, tv, extra is not None),
        grid=(nh, tq_all // tq),
        in_specs=in_specs,
        out_specs=pl.BlockSpec((tq, MLA_V), lambda h, i: (i, h)),
        out_shape=jax.ShapeDtypeStruct((tq_all, nh * MLA_V), BF16),
        scratch_shapes=[pltpu.VMEM((MLA_V, tq), F32), pltpu.VMEM((2, sub * tv, tq), F32)],
        compiler_params=_params(2),
        name="mla_attn",
    )(*args)


def _pool_kernel(t_total, x_ref, xp_ref, xn_ref, mod_ref, o_ref, ext_ref):
    tm = x_ref.shape[0]
    base = pl.program_id(0) * tm
    hc = _modulate(x_ref[...], mod_ref, 3)
    halo_rows = lax.broadcasted_iota(jnp.int32, (POOL_HALO, 1), 0)
    hp = jnp.where(base - POOL_HALO + halo_rows >= 0, _modulate(xp_ref[...], mod_ref, 3), 0.0)
    hn = jnp.where(base + tm + halo_rows < t_total, _modulate(xn_ref[...], mod_ref, 3), 0.0)
    ext_ref[0:POOL_HALO, :] = hp
    ext_ref[POOL_HALO:POOL_HALO + tm, :] = hc
    ext_ref[POOL_HALO + tm:, :] = hn
    tok = base + lax.broadcasted_iota(jnp.int32, (tm, 1), 0)
    for g, win in enumerate(POOL_WINDOWS):
        half = win // 2
        cols = slice(g * POOL_GROUP, (g + 1) * POOL_GROUP)
        acc = ext_ref[POOL_HALO - half:POOL_HALO - half + tm, cols]
        for j in range(-half + 1, half):
            acc = acc + ext_ref[POOL_HALO + j:POOL_HALO + j + tm, cols]
        cnt = (jnp.minimum(tok + half, t_total) - jnp.maximum(tok - half, 0)).astype(F32)
        o_ref[:, cols] = (acc / cnt - hc[:, cols]).astype(BF16)


def _pool(x, mod):
    t, d = x.shape
    tm = min(ROW_TILE, t)
    per = tm // POOL_HALO
    last = t // POOL_HALO - 1
    return pl.pallas_call(
        functools.partial(_pool_kernel, t),
        grid=(t // tm,),
        in_specs=[
            pl.BlockSpec((tm, d), lambda i: (i, 0)),
            pl.BlockSpec((POOL_HALO, d), lambda i: (jnp.maximum(i * per - 1, 0), 0)),
            pl.BlockSpec((POOL_HALO, d), lambda i: (jnp.minimum((i + 1) * per, last), 0)),
            _const_spec(mod.shape),
        ],
        out_specs=pl.BlockSpec((tm, d), lambda i: (i, 0)),
        out_shape=jax.ShapeDtypeStruct((t, d), BF16),
        scratch_shapes=[pltpu.VMEM((tm + 2 * POOL_HALO, d), F32)],
        compiler_params=_params(1),
        name="pool",
    )(x, x, x, mod)


def _conv_kernel(t_total, x_ref, xp_ref, xn_ref, mod_ref, wb_ref, wcu_ref, wconv_ref, o_ref, ext_ref, cu_ref):
    tm, d = x_ref.shape
    base = pl.program_id(0) * tm
    ext_ref[0:CONV_HALO, :] = _modulate(xp_ref[...], mod_ref, 3).astype(BF16)
    ext_ref[CONV_HALO:CONV_HALO + tm, :] = _modulate(x_ref[...], mod_ref, 3).astype(BF16)
    ext_ref[CONV_HALO + tm:, :] = _modulate(xn_ref[...], mod_ref, 3).astype(BF16)
    ext = ext_ref[...]
    cu = _dot(ext, wcu_ref[:, 0:d]) * _dot(ext, wcu_ref[:, d:2 * d])
    tok = base - CONV_HALO + lax.broadcasted_iota(jnp.int32, (tm + 2 * CONV_HALO, 1), 0)
    cu_ref[...] = jnp.where((tok >= 0) & (tok < t_total), cu, 0.0)
    z = (wconv_ref[0:1, :] * cu_ref[CONV_HALO - 1:CONV_HALO - 1 + tm, :]
         + wconv_ref[1:2, :] * cu_ref[CONV_HALO:CONV_HALO + tm, :]
         + wconv_ref[2:3, :] * cu_ref[CONV_HALO + 1:CONV_HALO + 1 + tm, :])
    b = _dot(ext_ref[CONV_HALO:CONV_HALO + tm, :], wb_ref[...])
    o_ref[...] = (b * z).astype(BF16)


def _conv(x, mod, w_b, w_cu, w_conv):
    t, d = x.shape
    tm = min(ROW_TILE, t)
    per = tm // CONV_HALO
    last = t // CONV_HALO - 1
    return pl.pallas_call(
        functools.partial(_conv_kernel, t),
        grid=(t // tm,),
        in_specs=[
            pl.BlockSpec((tm, d), lambda i: (i, 0)),
            pl.BlockSpec((CONV_HALO, d), lambda i: (jnp.maximum(i * per - 1, 0), 0)),
            pl.BlockSpec((CONV_HALO, d), lambda i: (jnp.minimum((i + 1) * per, last), 0)),
            _const_spec(mod.shape), _const_spec(w_b.shape), _const_spec(w_cu.shape), _const_spec(w_conv.shape),
        ],
        out_specs=pl.BlockSpec((tm, d), lambda i: (i, 0)),
        out_shape=jax.ShapeDtypeStruct((t, d), BF16),
        scratch_shapes=[pltpu.VMEM((tm + 2 * CONV_HALO, d), BF16), pltpu.VMEM((tm + 2 * CONV_HALO, d), F32)],
        compiler_params=_params(1),
        name="conv",
    )(x, x, x, mod, w_b, w_cu, w_conv)


def _head_rms(v, first_half):
    outs = []
    for c in range(v.shape[1] // LANES):
        blk = v[:, c * LANES:(c + 1) * LANES]
        sq = blk * blk
        tot = jnp.sum(sq, axis=-1, keepdims=True)
        lo = jnp.sum(jnp.where(first_half, sq, 0.0), axis=-1, keepdims=True)
        ms = jnp.where(first_half, lo, tot - lo) * (1.0 / NA_HEAD_DIM)
        outs.append(blk * lax.rsqrt(ms + EPS))
    return outs


def _na_proj_kernel(x_ref, mod_ref, w_ref, gq_ref, gk_ref, q_ref, k_ref, v_ref):
    d = x_ref.shape[1]
    h = _modulate(x_ref[...], mod_ref, 3).astype(BF16)
    first_half = lax.broadcasted_iota(jnp.int32, (1, LANES), 1) < NA_HEAD_DIM
    for c, blk in enumerate(_head_rms(_dot(h, w_ref[:, 0:d]), first_half)):
        q_ref[:, c * LANES:(c + 1) * LANES] = (blk * gq_ref[...]).astype(BF16)
    for c, blk in enumerate(_head_rms(_dot(h, w_ref[:, d:2 * d]), first_half)):
        k_ref[:, c * LANES:(c + 1) * LANES] = (blk * gk_ref[...]).astype(BF16)
    v_ref[...] = _dot(h, w_ref[:, 2 * d:3 * d]).astype(BF16)


def _na_proj(x, mod, w_qkv, g_q2, g_k2):
    t, d = x.shape
    tm = min(ROW_TILE, t)
    spec = pl.BlockSpec((tm, d), lambda i: (i, 0))
    return pl.pallas_call(
        _na_proj_kernel,
        grid=(t // tm,),
        in_specs=[spec, _const_spec(mod.shape), _const_spec(w_qkv.shape), _const_spec(g_q2.shape),
                  _const_spec(g_k2.shape)],
        out_specs=[spec, spec, spec],
        out_shape=[jax.ShapeDtypeStruct((t, d), BF16)] * 3,
        compiler_params=_params(1),
        name="na_proj",
    )(x, mod, w_qkv, g_q2, g_k2)


def _na_attn_kernel(n_rows, q_ref, kp_ref, kc_ref, kn_ref, vp_ref, vc_ref, vn_ref, kx_ref, vx_ref, bias_ref, o_ref,
                    kbuf_ref, vbuf_ref):
    blk = NA_BLOCK_ROWS * GRID_W
    band = NA_ROWS * GRID_W
    i = pl.program_id(0)
    kbuf_ref[0:blk, :] = kp_ref[...]
    kbuf_ref[blk:2 * blk, :] = kc_ref[...]
    kbuf_ref[2 * blk:3 * blk, :] = kn_ref[...]
    vbuf_ref[0:blk, :] = vp_ref[...]
    vbuf_ref[blk:2 * blk, :] = vc_ref[...]
    vbuf_ref[2 * blk:3 * blk, :] = vn_ref[...]
    first_half = lax.broadcasted_iota(jnp.int32, (1, LANES), 1) < NA_HEAD_DIM

    def row_body(j, _):
        r = i * NA_BLOCK_ROWS + j
        rs = jnp.clip(r - NA_ROWS // 2, 0, n_rows - NA_ROWS)
        case = r - rs
        off = pl.multiple_of((rs - (i - 1) * NA_BLOCK_ROWS) * GRID_W, GRID_W)
        qrow = q_ref[pl.ds(pl.multiple_of(j * GRID_W, GRID_W), GRID_W), :]
        for p in range(NA_HEADS // 2):
            cols = slice(p * LANES, (p + 1) * LANES)
            qp = qrow[:, cols]
            kb = kbuf_ref[pl.ds(off, band), cols]
            vb = vbuf_ref[pl.ds(off, band), cols]
            kx = kx_ref[:, cols]
            vx = vx_ref[:, cols]
            halves = []
            for sel in range(2):
                qh = jnp.where(first_half if sel == 0 else jnp.logical_not(first_half), qp, jnp.zeros_like(qp))
                s_loc = lax.dot_general(qh, kb, _NT, preferred_element_type=F32) + bias_ref[case, 2 * p + sel]
                s_ctx = lax.dot_general(qh, kx, _NT, preferred_element_type=F32)
                m = jnp.maximum(jnp.max(s_loc, axis=-1, keepdims=True), jnp.max(s_ctx, axis=-1, keepdims=True))
                p_loc = jnp.exp(s_loc - m)
                p_ctx = jnp.exp(s_ctx - m)
                denom = jnp.sum(p_loc, axis=-1, keepdims=True) + jnp.sum(p_ctx, axis=-1, keepdims=True)
                o = _dot(p_loc.astype(BF16), vb) + _dot(p_ctx.astype(BF16), vx)
                halves.append(o * (1.0 / denom))
            o_ref[pl.ds(pl.multiple_of(j * GRID_W, GRID_W), GRID_W), cols] = jnp.where(
                first_half, halves[0], halves[1]).astype(BF16)
        return 0

    lax.fori_loop(0, NA_BLOCK_ROWS, row_body, 0)


def _na_attn(q, k, v, k_ctx, v_ctx, bias):
    s, d = q.shape
    blk = NA_BLOCK_ROWS * GRID_W
    n_blk = s // blk
    cur = pl.BlockSpec((blk, d), lambda i: (i, 0))
    prev = pl.BlockSpec((blk, d), lambda i: (jnp.maximum(i - 1, 0), 0))
    nxt = pl.BlockSpec((blk, d), lambda i: (jnp.minimum(i + 1, n_blk - 1), 0))
    return pl.pallas_call(
        functools.partial(_na_attn_kernel, s // GRID_W),
        grid=(n_blk,),
        in_specs=[cur, prev, cur, nxt, prev, cur, nxt, _const_spec(k_ctx.shape), _const_spec(v_ctx.shape),
                  _const_spec(bias.shape)],
        out_specs=cur,
        out_shape=jax.ShapeDtypeStruct((s, d), BF16),
        scratch_shapes=[pltpu.VMEM((3 * blk, d), BF16), pltpu.VMEM((3 * blk, d), BF16)],
        compiler_params=_params(1),
        name="na_attn",
    )(q, k, k, k, v, v, v, k_ctx, v_ctx, bias)


def _na_bias_tables(rpb):
    cols = np.arange(GRID_W)
    col_start = np.clip(cols - NA_COLS // 2, 0, GRID_W - NA_COLS)
    kc = np.arange(GRID_W)
    valid = (kc[None, :] >= col_start[:, None]) & (kc[None, :] < col_start[:, None] + NA_COLS)
    pad = GRID_W - NA_COLS
    padded = jnp.pad(rpb, ((0, 0), (0, 0), (pad, pad)))
    toeplitz = jnp.stack([padded[:, :, GRID_W - 1 - c:2 * GRID_W - 1 - c] for c in range(GRID_W)], axis=2)
    toeplitz = jnp.where(valid[None, None], toeplitz, NEG_BIG)
    tab = jnp.stack([toeplitz[:, NA_ROWS - 1 - case:2 * NA_ROWS - 1 - case] for case in range(NA_ROWS)], axis=0)
    tab = jnp.transpose(tab, (0, 1, 3, 2, 4))
    return tab.reshape(NA_ROWS, rpb.shape[0], GRID_W, NA_ROWS * GRID_W)


def _rope_tables(t):
    pos = jnp.arange(t)
    row = (pos // GRID_W).astype(F32)
    col = (pos % GRID_W).astype(F32)
    n = MLA_ROPE // 4
    freqs = ROPE_BASE ** (-jnp.arange(n, dtype=F32) / n)
    ang = jnp.concatenate([row[:, None] * freqs, col[:, None] * freqs], axis=-1)
    cos, sin = jnp.cos(ang), jnp.sin(ang)
    return jnp.concatenate([cos, cos, sin, sin], axis=-1)


_HALF_SPLIT = np.concatenate([np.arange(0, MLA_ROPE, 2), np.arange(1, MLA_ROPE, 2)])


def _rope_cols(w):
    hs = w[..., _HALF_SPLIT]
    return jnp.concatenate([hs, -hs[..., MLA_ROPE // 2:], hs[..., :MLA_ROPE // 2]], axis=-1)


def _rope_gain(g):
    hs = g[_HALF_SPLIT]
    return jnp.concatenate([hs, hs[MLA_ROPE // 2:], hs[:MLA_ROPE // 2]])


def _mla_weights(w_dq, g_dq, w_uq, w_dkv, g_dkv, w_uk, w_uv, g_qn, g_qr, g_kn, g_kr):
    w_uq_ext = jnp.concatenate([w_uq[..., :MLA_NOPE], _rope_cols(w_uq[..., MLA_NOPE:])], axis=-1)
    g_q = jnp.tile(jnp.concatenate([g_qn, _rope_gain(g_qr)]) * (MLA_SCALE * LOG2_E), MLA_HEADS)
    return {
        "w_dq": w_dq.astype(BF16),
        "g_dq": g_dq[None, :],
        "w_uq": w_uq_ext.reshape(MLA_Q_LORA, -1).astype(BF16),
        "g_q": g_q[None, :],
        "w_dkv": jnp.concatenate([w_dkv[:, :MLA_KV_LORA], _rope_cols(w_dkv[:, MLA_KV_LORA:])], axis=-1).astype(BF16),
        "g_dkv": g_dkv[None, :],
        "w_uk": w_uk.reshape(MLA_KV_LORA, -1).astype(BF16),
        "g_k": jnp.concatenate([g_kn, _rope_gain(g_kr)])[None, :],
        "w_uvt": w_uv.reshape(MLA_KV_LORA, -1).T.astype(BF16),
    }


def _block_diag(w):
    g, c, _ = w.shape
    out = jnp.zeros((g * c, g * c), w.dtype)
    for i in range(g):
        out = out.at[i * c:(i + 1) * c, i * c:(i + 1) * c].set(w[i])
    return out


def kernel(x, c, ctx, c_ctx, mod_w, mod_b, ffn_w_in, ffn_w_out, mla_w_dq, mla_g_dq, mla_w_uq, mla_w_dkv, mla_g_dkv, mla_w_uk, mla_w_uv, mla_g_qn, mla_g_qr, mla_g_kn, mla_g_kr, mla_w_o, pool_w, pool_scale, na_w_qkv, na_g_q, na_g_k, na_rpb, na_w_o, conv_w_in, conv_w, conv_w_out):
    assert x.shape[0] == 1 and x.shape[2] == D_MODEL and x.shape[1] % (NA_BLOCK_ROWS * GRID_W) == 0
    s = x.shape[1]
    d = D_MODEL
    xs = x[0]
    hc = ctx[0]
    n_ctx = hc.shape[0]

    cond = jnp.zeros((8, d), F32).at[0].set(c[0]).at[1].set(c_ctx)
    mods = _ada_params(cond, mod_w, mod_b)
    w_in = ffn_w_in.astype(BF16)
    w_out = ffn_w_out.astype(BF16)

    mx = mods[0, 0].reshape(N_MOD, d)
    mc = mods[0, 1].reshape(N_MOD, d)
    xs = _ffn(xs, mx, 0, w_in[0, 0], w_out[0, 0])
    hc = _ffn(hc, mc, 0, w_in[0, 0], w_out[0, 0])
    mw = _mla_weights(mla_w_dq[0], mla_g_dq[0], mla_w_uq[0], mla_w_dkv[0], mla_g_dkv[0], mla_w_uk[0], mla_w_uv[0],
                      mla_g_qn[0], mla_g_qr[0], mla_g_kn[0], mla_g_kr[0])
    no_rope = jnp.concatenate([jnp.ones((n_ctx, LANES // 2), F32), jnp.zeros((n_ctx, LANES // 2), F32)], axis=-1)
    q_x, k_x, vt_x = _mla_proj(xs, mx, _rope_tables(s), mw)
    q_c, k_c, vt_c = _mla_proj(hc, mc, no_rope, mw)
    o_x = _mla_attn(q_x, k_x, vt_x, extra=(k_c, vt_c))
    o_c = _mla_attn(q_c, k_c, vt_c)
    w_o = mla_w_o[0].astype(BF16)
    xs = _ffn(xs, mx, 6, w_in[0, 1], w_out[0, 1], pro=(o_x, w_o, mx[5:6]))
    hc = _ffn(hc, mc, 6, w_in[0, 1], w_out[0, 1], pro=(o_c, w_o, mc[5:6]))

    mx = mods[1, 0].reshape(N_MOD, d)
    mc = mods[1, 1].reshape(N_MOD, d)
    xs = _ffn(xs, mx, 0, w_in[1, 0], w_out[1, 0])
    hc = _ffn(hc, mc, 0, w_in[1, 0], w_out[1, 0])
    w_p = _block_diag(pool_w[0]).astype(BF16)
    xs = _ffn(xs, mx, 6, w_in[1, 1], w_out[1, 1], pro=(_pool(xs, mx), w_p, mx[5:6] * pool_scale[0][None, :]))
    hc = _ffn(hc, mc, 6, w_in[1, 1], w_out[1, 1], pro=(_pool(hc, mc), w_p, mc[5:6] * pool_scale[0][None, :]))

    mx = mods[2, 0].reshape(N_MOD, d)
    mc = mods[2, 1].reshape(N_MOD, d)
    xs = _ffn(xs, mx, 0, w_in[2, 0], w_out[2, 0])
    hc = _ffn(hc, mc, 0, w_in[2, 0], w_out[2, 0])
    w_qkv = na_w_qkv[0].astype(BF16)
    g_q2 = jnp.tile(na_g_q[0] * NA_SCALE, 2)[None, :]
    g_k2 = jnp.tile(na_g_k[0], 2)[None, :]
    q_n, k_n, v_n = _na_proj(xs, mx, w_qkv, g_q2, g_k2)
    _, k_nc, v_nc = _na_proj(hc, mc, w_qkv, g_q2, g_k2)
    o_n = _na_attn(q_n, k_n, v_n, k_nc, v_nc, _na_bias_tables(na_rpb[0]))
    xs = _ffn(xs, mx, 6, w_in[2, 1], w_out[2, 1], pro=(o_n, na_w_o[0].astype(BF16), mx[5:6]))

    mx = mods[3, 0].reshape(N_MOD, d)
    xs = _ffn(xs, mx, 0, w_in[3, 0], w_out[3, 0])
    w_ci = conv_w_in[0].astype(BF16)
    y_c = _conv(xs, mx, w_ci[:, :d], w_ci[:, d:], conv_w[0])
    xs = _ffn(xs, mx, 6, w_in[3, 1], w_out[3, 1], pro=(y_c, conv_w_out[0].astype(BF16), mx[5:6]))
    return xs[None]
```

```python
import functools

import jax
import jax.numpy as jnp
import numpy as np
from jax import lax
from jax.experimental import pallas as pl
from jax.experimental.pallas import tpu as pltpu

F32 = jnp.float32
BF16 = jnp.bfloat16

D_MODEL = 1024
DEPTH = 4
GRID_W = 64
N_MOD = 9
D_FF = 2816
EPS = 1e-6
MLA_HEADS = 8
MLA_Q_LORA = 384
MLA_KV_LORA = 256
MLA_NOPE = 128
MLA_ROPE = 64
MLA_V = 128
MLA_SCALE = (MLA_NOPE + MLA_ROPE) ** -0.5
ROPE_BASE = 10000.0
POOL_WINDOWS = (2, 4, 8, 16)
POOL_GROUP = D_MODEL // len(POOL_WINDOWS)
NA_HEADS = 16
NA_HEAD_DIM = D_MODEL // NA_HEADS
NA_ROWS = 8
NA_COLS = 16
NA_SCALE = NA_HEAD_DIM ** -0.5

LANES = 128
MXU_COLS = 256
VMEM_LIMIT = 56 * 1024 * 1024
ROW_TILE = 512
MLA_Q_TILE = 1024
FF_CHUNK = MXU_COLS
POOL_HALO = 8
CONV_HALO = 16
NA_Q_ROWS = 4
NEG_BIG = -1e30
LOG2_E = 1.4426950408889634

_NT = (((1,), (1,)), ((), ()))


def _params(n_axes, flags=None):
    return pltpu.CompilerParams(dimension_semantics=("arbitrary",) * n_axes, vmem_limit_bytes=VMEM_LIMIT, flags=flags)


def _const_spec(shape):
    nd = len(shape)
    return pl.BlockSpec(shape, lambda *_: (0,) * nd, pipeline_mode=pl.Buffered(1))


def _rms(x):
    return x * lax.rsqrt(jnp.mean(x * x, axis=-1, keepdims=True) + EPS)


def _modulate(x, mod_ref, row):
    shift = mod_ref[row:row + 1, :]
    scale = mod_ref[row + 1:row + 2, :]
    return _rms(x) * (1.0 + scale) + shift


def _dot(a, b):
    return jnp.dot(a, b, preferred_element_type=F32)


def _ada_kernel(cond_ref, w_ref, b_ref, o_ref):
    cnd = cond_ref[...]
    s = (cnd * (1.0 / (1.0 + jnp.exp(-cnd)))).astype(BF16)
    o_ref[...] = _dot(s, w_ref[...].astype(BF16)) + b_ref[...]


def _ada_params(cond, mod_w, mod_b):
    depth, d, n = mod_w.shape
    tn = n // 8
    return pl.pallas_call(
        _ada_kernel,
        grid=(depth, n // tn),
        in_specs=[
            pl.BlockSpec((8, d), lambda i, j: (0, 0)),
            pl.BlockSpec((None, d, tn), lambda i, j: (i, 0, j)),
            pl.BlockSpec((None, 1, tn), lambda i, j: (i, 0, j)),
        ],
        out_specs=pl.BlockSpec((None, 8, tn), lambda i, j: (i, 0, j)),
        out_shape=jax.ShapeDtypeStruct((depth, 8, n), F32),
        compiler_params=_params(2),
        name="ada_params",
    )(cond, mod_w, mod_b.reshape(depth, 1, n))


def _ffn_kernel(has_pro, mod_row, *refs):
    if has_pro:
        x_ref, y_ref, wp_ref, gp_ref, mod_ref, wi_ref, wo_ref, o_ref, a_ref = refs
    else:
        x_ref, mod_ref, wi_ref, wo_ref, o_ref, a_ref = refs
    x = x_ref[...]
    if has_pro:
        x = x + gp_ref[...] * _dot(y_ref[...], wp_ref[...])
    h = _modulate(x, mod_ref, mod_row).astype(BF16)
    for j in range(D_FF // FF_CHUNK):
        g = _dot(h, wi_ref[:, j * FF_CHUNK:(j + 1) * FF_CHUNK])
        u = _dot(h, wi_ref[:, D_FF + j * FF_CHUNK:D_FF + (j + 1) * FF_CHUNK])
        a_ref[:, j * FF_CHUNK:(j + 1) * FF_CHUNK] = (g * (1.0 / (1.0 + jnp.exp(-g))) * u).astype(BF16)
    gate = mod_ref[mod_row + 2:mod_row + 3, :]
    o_ref[...] = x + (0.5 * gate) * _dot(a_ref[...], wo_ref[...])


def _slab_spec(shape, lead):
    rest = len(shape) - len(lead)
    return pl.BlockSpec((None,) * len(lead) + tuple(shape[len(lead):]), lambda *_: tuple(lead) + (0,) * rest,
                        pipeline_mode=pl.Buffered(1))


def _ffn(x, mod, mod_row, w_in, w_out, which, pro=None):
    t, d = x.shape
    tm = min(ROW_TILE, t)
    row = lambda i: (i, 0)
    in_specs = [pl.BlockSpec((tm, d), row)]
    args = [x]
    if pro is not None:
        y, w_p, g_p = pro
        in_specs += [pl.BlockSpec((tm, d), row), _const_spec(w_p.shape), _const_spec(g_p.shape)]
        args += [y, w_p, g_p]
    in_specs += [_const_spec(mod.shape), _slab_spec(w_in.shape, which), _slab_spec(w_out.shape, which)]
    args += [mod, w_in, w_out]
    return pl.pallas_call(
        functools.partial(_ffn_kernel, pro is not None, mod_row),
        grid=(t // tm,),
        in_specs=in_specs,
        out_specs=pl.BlockSpec((tm, d), row),
        out_shape=jax.ShapeDtypeStruct((t, d), F32),
        scratch_shapes=[pltpu.VMEM((tm, D_FF), BF16)],
        compiler_params=_params(1),
        name="ffn_pro" if pro is not None else "ffn",
    )(*args)


def _rope_pair(r2, gain, cs, first_half):
    ms = jnp.sum(jnp.where(first_half, r2 * r2, 0.0), axis=-1, keepdims=True) * (1.0 / MLA_ROPE)
    t = r2 * lax.rsqrt(ms + EPS) * gain * cs
    return t + pltpu.roll(t, MLA_ROPE, axis=1)


def _mla_proj_kernel(x_ref, mod_ref, cs_ref, wdq_ref, gdq_ref, wuq_ref, gq_ref, wdkv_ref, gdkv_ref, wuk_ref,
                     gk_ref, wuvt_ref, q_ref, k_ref, vt_ref):
    h = _modulate(x_ref[...], mod_ref, 3).astype(BF16)
    cs = cs_ref[...]
    first_half = lax.broadcasted_iota(jnp.int32, (1, LANES), 1) < MLA_ROPE
    cq = (_rms(_dot(h, wdq_ref[...])) * gdq_ref[...]).astype(BF16)
    q = _dot(cq, wuq_ref[...])
    for hd in range(MLA_HEADS):
        c0 = hd * 2 * LANES
        qn = q[:, c0:c0 + LANES]
        q_ref[hd, :, 0:LANES] = (_rms(qn) * gq_ref[:, c0:c0 + LANES]).astype(BF16)
        rot = _rope_pair(q[:, c0 + LANES:c0 + 2 * LANES], gq_ref[:, c0 + LANES:c0 + 2 * LANES], cs, first_half)
        q_ref[hd, :, LANES:2 * LANES] = rot.astype(BF16)
    kv = _dot(h, wdkv_ref[...])
    ckv = (_rms(kv[:, :MLA_KV_LORA]) * gdkv_ref[...]).astype(BF16)
    rot = _rope_pair(kv[:, MLA_KV_LORA:], gk_ref[:, LANES:2 * LANES], cs, first_half)
    kr = jnp.where(first_half, rot, 0.0).astype(BF16)
    kn = _dot(ckv, wuk_ref[...])
    vt = lax.dot_general(wuvt_ref[...], ckv, _NT, preferred_element_type=F32)
    for hd in range(MLA_HEADS):
        blk = kn[:, hd * LANES:(hd + 1) * LANES]
        k_ref[hd, :, 0:LANES] = (_rms(blk) * gk_ref[:, 0:LANES]).astype(BF16)
        k_ref[hd, :, LANES:2 * LANES] = kr
        vt_ref[hd] = vt[hd * MLA_V:(hd + 1) * MLA_V, :].astype(BF16)


def _mla_proj(x, mod, cs, w):
    t, d = x.shape
    tm = min(ROW_TILE, t)
    consts = [w["w_dq"], w["g_dq"], w["w_uq"], w["g_q"], w["w_dkv"], w["g_dkv"], w["w_uk"], w["g_k"], w["w_uvt"]]
    return pl.pallas_call(
        _mla_proj_kernel,
        grid=(t // tm,),
        in_specs=[pl.BlockSpec((tm, d), lambda i: (i, 0)), _const_spec(mod.shape),
                  pl.BlockSpec((tm, LANES), lambda i: (i, 0))] + [_const_spec(a.shape) for a in consts],
        out_specs=[
            pl.BlockSpec((MLA_HEADS, tm, 2 * LANES), lambda i: (0, i, 0)),
            pl.BlockSpec((MLA_HEADS, tm, 2 * LANES), lambda i: (0, i, 0)),
            pl.BlockSpec((MLA_HEADS, None, MLA_V, tm), lambda i: (0, i, 0, 0)),
        ],
        out_shape=[
            jax.ShapeDtypeStruct((MLA_HEADS, t, 2 * LANES), BF16),
            jax.ShapeDtypeStruct((MLA_HEADS, t, 2 * LANES), BF16),
            jax.ShapeDtypeStruct((MLA_HEADS, t // tm, MLA_V, tm), BF16),
        ],
        compiler_params=_params(1),
        name="mla_proj",
    )(x, mod, cs, *consts)


def _mla_attn_kernel(n_main, sub, tv, has_extra, *refs):
    if has_extra:
        q_ref, k_ref, vt_ref, ke_ref, vte_ref, o_ref, acc_ref, s_ref = refs
    else:
        q_ref, k_ref, vt_ref, o_ref, acc_ref, s_ref = refs
    q = q_ref[...]
    tq = q.shape[0]
    tk = sub * tv
    acc_ref[...] = jnp.zeros_like(acc_ref)

    def scores(j):
        k = k_ref[pl.ds(pl.multiple_of(j * tk, tk), tk), :]
        return lax.dot_general(k, q, _NT, preferred_element_type=F32)

    def main_vts(j):
        return [vt_ref[j * sub + c] for c in range(sub)]

    def update(carry, s, vts):
        m, l = carry
        m_new = jnp.maximum(m, jnp.max(s, axis=0, keepdims=True))
        alpha = jnp.exp2(m - m_new)
        p = jnp.exp2(s - m_new)
        l = alpha * l + jnp.sum(p, axis=0, keepdims=True)
        pb = p.astype(BF16)
        pv = _dot(vts[0], pb[0:vts[0].shape[1], :])
        for c in range(1, len(vts)):
            pv = pv + _dot(vts[c], pb[c * tv:(c + 1) * tv, :])
        acc_ref[...] = alpha * acc_ref[...] + pv
        return m_new, l

    carry = (jnp.full((1, tq), NEG_BIG, F32), jnp.zeros((1, tq), F32))
    s_ref[0] = scores(0)
    if n_main > 1:
        assert n_main % 2 == 0

        def body(jj, carry):
            j = 2 * jj
            s_ref[1] = scores(j + 1)
            carry = update(carry, s_ref[0], main_vts(j))
            s_ref[0] = scores(j + 2)
            return update(carry, s_ref[1], main_vts(j + 1))

        carry = lax.fori_loop(0, n_main // 2 - 1, body, carry)
        s_ref[1] = scores(n_main - 1)
        carry = update(carry, s_ref[0], main_vts(n_main - 2))
        carry = update(carry, s_ref[1], main_vts(n_main - 1))
    else:
        carry = update(carry, s_ref[0], main_vts(0))
    if has_extra:
        s_e = lax.dot_general(ke_ref[...], q, _NT, preferred_element_type=F32)
        carry = update(carry, s_e, [vte_ref[...]])
    o_ref[...] = (acc_ref[...] * (1.0 / carry[1])).T.astype(BF16)


def _mla_attn(q, k, vt, extra=None):
    nh, tq_all, dk = q.shape
    tk_all = k.shape[1]
    n_v, tv = vt.shape[1], vt.shape[3]
    sub = 2 if n_v % 2 == 0 else 1
    tq = min(MLA_Q_TILE, tq_all)
    in_specs = [
        pl.BlockSpec((None, tq, dk), lambda h, i: (h, i, 0)),
        pl.BlockSpec((None, tk_all, dk), lambda h, i: (h, 0, 0)),
        pl.BlockSpec((None, n_v, MLA_V, tv), lambda h, i: (h, 0, 0, 0)),
    ]
    args = [q, k, vt]
    if extra is not None:
        k_e, vt_e = extra
        in_specs += [pl.BlockSpec((None,) + k_e.shape[1:], lambda h, i: (h, 0, 0)),
                     pl.BlockSpec((None, None) + vt_e.shape[2:], lambda h, i: (h, 0, 0, 0))]
        args += [k_e, vt_e]
    return pl.pallas_call(
        functools.partial(_mla_attn_kernel, n_v // sub, sub, tv, extra is not None),
        grid=(nh, tq_all // tq),
        in_specs=in_specs,
        out_specs=pl.BlockSpec((tq, MLA_V), lambda h, i: (i, h)),
        out_shape=jax.ShapeDtypeStruct((tq_all, nh * MLA_V), BF16),
        scratch_shapes=[pltpu.VMEM((MLA_V, tq), F32), pltpu.VMEM((2, sub * tv, tq), F32)],
        compiler_params=_params(2),
        name="mla_attn",
    )(*args)


def _pool_kernel(t_total, x_ref, xp_ref, xn_ref, mod_ref, o_ref, ext_ref):
    tm = x_ref.shape[0]
    base = pl.program_id(0) * tm
    hc = _modulate(x_ref[...], mod_ref, 3)
    halo_rows = lax.broadcasted_iota(jnp.int32, (POOL_HALO, 1), 0)
    hp = jnp.where(base - POOL_HALO + halo_rows >= 0, _modulate(xp_ref[...], mod_ref, 3), 0.0)
    hn = jnp.where(base + tm + halo_rows < t_total, _modulate(xn_ref[...], mod_ref, 3), 0.0)
    ext_ref[0:POOL_HALO, :] = hp
    ext_ref[POOL_HALO:POOL_HALO + tm, :] = hc
    ext_ref[POOL_HALO + tm:, :] = hn
    tok = base + lax.broadcasted_iota(jnp.int32, (tm, 1), 0)
    for g, win in enumerate(POOL_WINDOWS):
        half = win // 2
        cols = slice(g * POOL_GROUP, (g + 1) * POOL_GROUP)
        acc = ext_ref[POOL_HALO - half:POOL_HALO - half + tm, cols]
        for j in range(-half + 1, half):
            acc = acc + ext_ref[POOL_HALO + j:POOL_HALO + j + tm, cols]
        cnt = (jnp.minimum(tok + half, t_total) - jnp.maximum(tok - half, 0)).astype(F32)
        o_ref[:, cols] = (acc / cnt - hc[:, cols]).astype(BF16)


def _pool(x, mod):
    t, d = x.shape
    tm = min(ROW_TILE, t)
    per = tm // POOL_HALO
    last = t // POOL_HALO - 1
    return pl.pallas_call(
        functools.partial(_pool_kernel, t),
        grid=(t // tm,),
        in_specs=[
            pl.BlockSpec((tm, d), lambda i: (i, 0)),
            pl.BlockSpec((POOL_HALO, d), lambda i: (jnp.maximum(i * per - 1, 0), 0)),
            pl.BlockSpec((POOL_HALO, d), lambda i: (jnp.minimum((i + 1) * per, last), 0)),
            _const_spec(mod.shape),
        ],
        out_specs=pl.BlockSpec((tm, d), lambda i: (i, 0)),
        out_shape=jax.ShapeDtypeStruct((t, d), BF16),
        scratch_shapes=[pltpu.VMEM((tm + 2 * POOL_HALO, d), F32)],
        compiler_params=_params(1),
        name="pool",
    )(x, x, x, mod)


def _conv_kernel(t_total, x_ref, xp_ref, xn_ref, mod_ref, wb_ref, wcu_ref, wconv_ref, o_ref, ext_ref, cu_ref):
    tm, d = x_ref.shape
    base = pl.program_id(0) * tm
    ext_ref[0:CONV_HALO, :] = _modulate(xp_ref[...], mod_ref, 3).astype(BF16)
    ext_ref[CONV_HALO:CONV_HALO + tm, :] = _modulate(x_ref[...], mod_ref, 3).astype(BF16)
    ext_ref[CONV_HALO + tm:, :] = _modulate(xn_ref[...], mod_ref, 3).astype(BF16)
    ext = ext_ref[...]
    cu = _dot(ext, wcu_ref[:, 0:d]) * _dot(ext, wcu_ref[:, d:2 * d])
    tok = base - CONV_HALO + lax.broadcasted_iota(jnp.int32, (tm + 2 * CONV_HALO, 1), 0)
    cu_ref[...] = jnp.where((tok >= 0) & (tok < t_total), cu, 0.0)
    z = (wconv_ref[0:1, :] * cu_ref[CONV_HALO - 1:CONV_HALO - 1 + tm, :]
         + wconv_ref[1:2, :] * cu_ref[CONV_HALO:CONV_HALO + tm, :]
         + wconv_ref[2:3, :] * cu_ref[CONV_HALO + 1:CONV_HALO + 1 + tm, :])
    b = _dot(ext_ref[CONV_HALO:CONV_HALO + tm, :], wb_ref[...])
    o_ref[...] = (b * z).astype(BF16)


def _conv(x, mod, w_b, w_cu, w_conv):
    t, d = x.shape
    tm = min(ROW_TILE, t)
    per = tm // CONV_HALO
    last = t // CONV_HALO - 1
    return pl.pallas_call(
        functools.partial(_conv_kernel, t),
        grid=(t // tm,),
        in_specs=[
            pl.BlockSpec((tm, d), lambda i: (i, 0)),
            pl.BlockSpec((CONV_HALO, d), lambda i: (jnp.maximum(i * per - 1, 0), 0)),
            pl.BlockSpec((CONV_HALO, d), lambda i: (jnp.minimum((i + 1) * per, last), 0)),
            _const_spec(mod.shape), _const_spec(w_b.shape), _const_spec(w_cu.shape), _const_spec(w_conv.shape),
        ],
        out_specs=pl.BlockSpec((tm, d), lambda i: (i, 0)),
        out_shape=jax.ShapeDtypeStruct((t, d), BF16),
        scratch_shapes=[pltpu.VMEM((tm + 2 * CONV_HALO, d), BF16), pltpu.VMEM((tm + 2 * CONV_HALO, d), F32)],
        compiler_params=_params(1),
        name="conv",
    )(x, x, x, mod, w_b, w_cu, w_conv)


def _head_rms(v, first_half):
    outs = []
    for c in range(v.shape[1] // LANES):
        blk = v[:, c * LANES:(c + 1) * LANES]
        sq = blk * blk
        tot = jnp.sum(sq, axis=-1, keepdims=True)
        lo = jnp.sum(jnp.where(first_half, sq, 0.0), axis=-1, keepdims=True)
        ms = jnp.where(first_half, lo, tot - lo) * (1.0 / NA_HEAD_DIM)
        outs.append(blk * lax.rsqrt(ms + EPS))
    return outs


def _na_proj_kernel(x_ref, mod_ref, w_ref, gq_ref, gk_ref, q_ref, k_ref, v_ref):
    d = x_ref.shape[1]
    h = _modulate(x_ref[...], mod_ref, 3).astype(BF16)
    first_half = lax.broadcasted_iota(jnp.int32, (1, LANES), 1) < NA_HEAD_DIM
    for c, blk in enumerate(_head_rms(_dot(h, w_ref[:, 0:d]), first_half)):
        q_ref[:, c * LANES:(c + 1) * LANES] = (blk * gq_ref[...]).astype(BF16)
    for c, blk in enumerate(_head_rms(_dot(h, w_ref[:, d:2 * d]), first_half)):
        k_ref[:, c * LANES:(c + 1) * LANES] = (blk * gk_ref[...]).astype(BF16)
    v_ref[...] = _dot(h, w_ref[:, 2 * d:3 * d]).astype(BF16)


def _na_proj(x, mod, w_qkv, g_q2, g_k2):
    t, d = x.shape
    tm = min(ROW_TILE, t)
    spec = pl.BlockSpec((tm, d), lambda i: (i, 0))
    return pl.pallas_call(
        _na_proj_kernel,
        grid=(t // tm,),
        in_specs=[spec, _const_spec(mod.shape), _const_spec(w_qkv.shape), _const_spec(g_q2.shape),
                  _const_spec(g_k2.shape)],
        out_specs=[spec, spec, spec],
        out_shape=[jax.ShapeDtypeStruct((t, d), BF16)] * 3,
        compiler_params=_params(1),
        name="na_proj",
    )(x, mod, w_qkv, g_q2, g_k2)


def _na_attn_kernel(q_ref, k0_ref, k1_ref, k2_ref, v0_ref, v1_ref, v2_ref, kx_ref, vx_ref, bias_ref, o_ref):
    nq = q_ref.shape[0]
    first_half = lax.broadcasted_iota(jnp.int32, (1, LANES), 1) < NA_HEAD_DIM
    n_pairs = NA_HEADS // 2

    def scores(p):
        cols = slice(p * LANES, (p + 1) * LANES)
        qp = q_ref[:, cols]
        zero = jnp.zeros_like(qp)
        qs = jnp.concatenate([jnp.where(first_half, qp, zero), jnp.where(first_half, zero, qp)], axis=0)
        parts = []
        for t, k_ref in enumerate((k0_ref, k1_ref, k2_ref)):
            lanes = slice(t * nq, (t + 1) * nq)
            bias = jnp.concatenate([bias_ref[2 * p, :, lanes], bias_ref[2 * p + 1, :, lanes]], axis=0)
            parts.append(lax.dot_general(qs, k_ref[:, cols], _NT, preferred_element_type=F32) + bias)
        parts.append(lax.dot_general(qs, kx_ref[:, cols], _NT, preferred_element_type=F32))
        return parts

    def finish(p, parts):
        cols = slice(p * LANES, (p + 1) * LANES)
        m = jnp.max(jnp.maximum(jnp.maximum(parts[0], parts[1]), jnp.maximum(parts[2], parts[3])),
                    axis=-1, keepdims=True)
        e_sum = None
        o = None
        for s, v_ref in zip(parts, (v0_ref, v1_ref, v2_ref, vx_ref)):
            e = jnp.exp2(s - m)
            part_o = _dot(e.astype(BF16), v_ref[:, cols])
            e_sum = e if e_sum is None else e_sum + e
            o = part_o if o is None else o + part_o
        o = o * (1.0 / jnp.sum(e_sum, axis=-1, keepdims=True))
        o_ref[:, cols] = jnp.where(first_half, o[0:nq], o[nq:2 * nq]).astype(BF16)

    parts_next = scores(0)
    for p in range(n_pairs):
        parts = parts_next
        if p + 1 < n_pairs:
            parts_next = scores(p + 1)
        finish(p, parts)


def _na_attn(q, k, v, k_ctx, v_ctx, bias):
    s, d = q.shape
    chunk = NA_Q_ROWS * GRID_W
    n_steps = s // chunk

    def band(t):
        return pl.BlockSpec((chunk, d), lambda i: (jnp.clip(i - 1, 0, n_steps - 3) + t, 0))

    cur = pl.BlockSpec((chunk, d), lambda i: (i, 0))
    bias_spec = pl.BlockSpec((None,) + bias.shape[1:],
                             lambda i: (jnp.where(i == 0, 0, jnp.where(i == n_steps - 1, 2, 1)), 0, 0, 0),
                             pipeline_mode=pl.Buffered(1))
    return pl.pallas_call(
        _na_attn_kernel,
        grid=(n_steps,),
        in_specs=[cur, band(0), band(1), band(2), band(0), band(1), band(2), _const_spec(k_ctx.shape),
                  _const_spec(v_ctx.shape), bias_spec],
        out_specs=cur,
        out_shape=jax.ShapeDtypeStruct((s, d), BF16),
        compiler_params=_params(1),
        name="na_attn",
    )(q, k, k, k, v, v, v, k_ctx, v_ctx, bias)


def _na_band_pattern(step, n_rows):
    n_steps = n_rows // NA_Q_ROWS
    band0 = NA_Q_ROWS * int(np.clip(step - 1, 0, n_steps - 3))
    r = step * NA_Q_ROWS + np.arange(NA_Q_ROWS)[:, None]
    key_row = band0 + np.arange(3 * NA_Q_ROWS)[None, :]
    win0 = np.clip(r - NA_ROWS // 2, 0, n_rows - NA_ROWS)
    valid = (key_row >= win0) & (key_row < win0 + NA_ROWS)
    return np.where(valid, key_row - r + (NA_ROWS - 1), 0), valid


def _na_bias_tables(rpb, n_rows):
    n_steps = n_rows // NA_Q_ROWS
    assert n_steps >= 4
    patterns = [_na_band_pattern(t, n_rows) for t in range(n_steps)]
    for dr, valid in patterns[2:-1]:
        assert np.array_equal(dr, patterns[1][0]) and np.array_equal(valid, patterns[1][1])
    cols = np.arange(GRID_W)
    col_start = np.clip(cols - NA_COLS // 2, 0, GRID_W - NA_COLS)
    kc = np.arange(GRID_W)
    col_ok = (kc[None, :] >= col_start[:, None]) & (kc[None, :] < col_start[:, None] + NA_COLS)
    pad = GRID_W - NA_COLS
    padded = jnp.pad(rpb * LOG2_E, ((0, 0), (0, 0), (pad, pad)))
    toeplitz = jnp.stack([padded[:, :, GRID_W - 1 - c:2 * GRID_W - 1 - c] for c in range(GRID_W)], axis=2)
    toeplitz = jnp.where(col_ok[None, None], toeplitz, NEG_BIG)
    masked = jnp.full(toeplitz[:, 0].shape, NEG_BIG, F32)
    tables = []
    for dr, valid in (patterns[0], patterns[1], patterns[-1]):
        rows = [jnp.concatenate([toeplitz[:, dr[jq, i]] if valid[jq, i] else masked for i in range(dr.shape[1])],
                                axis=-1) for jq in range(NA_Q_ROWS)]
        tables.append(jnp.concatenate(rows, axis=-2))
    return jnp.stack(tables, axis=0)


def _rope_tables(t):
    pos = jnp.arange(t)
    row = (pos // GRID_W).astype(F32)
    col = (pos % GRID_W).astype(F32)
    n = MLA_ROPE // 4
    freqs = ROPE_BASE ** (-jnp.arange(n, dtype=F32) / n)
    ang = jnp.concatenate([row[:, None] * freqs, col[:, None] * freqs], axis=-1)
    cos, sin = jnp.cos(ang), jnp.sin(ang)
    return jnp.concatenate([cos, cos, sin, sin], axis=-1)


_HALF_SPLIT = np.concatenate([np.arange(0, MLA_ROPE, 2), np.arange(1, MLA_ROPE, 2)])


def _rope_cols(w):
    hs = w[..., _HALF_SPLIT]
    return jnp.concatenate([hs, -hs[..., MLA_ROPE // 2:], hs[..., :MLA_ROPE // 2]], axis=-1)


def _rope_gain(g):
    hs = g[_HALF_SPLIT]
    return jnp.concatenate([hs, hs[MLA_ROPE // 2:], hs[:MLA_ROPE // 2]])


def _mla_weights(w_dq, g_dq, w_uq, w_dkv, g_dkv, w_uk, w_uv, g_qn, g_qr, g_kn, g_kr):
    w_uq_ext = jnp.concatenate([w_uq[..., :MLA_NOPE], _rope_cols(w_uq[..., MLA_NOPE:])], axis=-1)
    g_q = jnp.tile(jnp.concatenate([g_qn, _rope_gain(g_qr)]) * (MLA_SCALE * LOG2_E), MLA_HEADS)
    return {
        "w_dq": w_dq.astype(BF16),
        "g_dq": g_dq[None, :],
        "w_uq": w_uq_ext.reshape(MLA_Q_LORA, -1).astype(BF16),
        "g_q": g_q[None, :],
        "w_dkv": jnp.concatenate([w_dkv[:, :MLA_KV_LORA], _rope_cols(w_dkv[:, MLA_KV_LORA:])], axis=-1).astype(BF16),
        "g_dkv": g_dkv[None, :],
        "w_uk": w_uk.reshape(MLA_KV_LORA, -1).astype(BF16),
        "g_k": jnp.concatenate([g_kn, _rope_gain(g_kr)])[None, :],
        "w_uvt": w_uv.reshape(MLA_KV_LORA, -1).T.astype(BF16),
    }


def _block_diag(w):
    g, c, _ = w.shape
    out = jnp.zeros((g * c, g * c), w.dtype)
    for i in range(g):
        out = out.at[i * c:(i + 1) * c, i * c:(i + 1) * c].set(w[i])
    return out


def kernel(x, c, ctx, c_ctx, mod_w, mod_b, ffn_w_in, ffn_w_out, mla_w_dq, mla_g_dq, mla_w_uq, mla_w_dkv, mla_g_dkv, mla_w_uk, mla_w_uv, mla_g_qn, mla_g_qr, mla_g_kn, mla_g_kr, mla_w_o, pool_w, pool_scale, na_w_qkv, na_g_q, na_g_k, na_rpb, na_w_o, conv_w_in, conv_w, conv_w_out):
    assert x.shape[0] == 1 and x.shape[2] == D_MODEL and x.shape[1] % ROW_TILE == 0
    s = x.shape[1]
    d = D_MODEL
    xs = x[0]
    hc = ctx[0]
    n_ctx = hc.shape[0]

    cond = jnp.zeros((8, d), F32).at[0].set(c[0]).at[1].set(c_ctx)
    mods = _ada_params(cond, mod_w, mod_b)
    w_in = ffn_w_in.astype(BF16)
    w_out = ffn_w_out.astype(BF16)

    mx = mods[0, 0].reshape(N_MOD, d)
    mc = mods[0, 1].reshape(N_MOD, d)
    xs = _ffn(xs, mx, 0, w_in, w_out, (0, 0))
    hc = _ffn(hc, mc, 0, w_in, w_out, (0, 0))
    mw = _mla_weights(mla_w_dq[0], mla_g_dq[0], mla_w_uq[0], mla_w_dkv[0], mla_g_dkv[0], mla_w_uk[0], mla_w_uv[0],
                      mla_g_qn[0], mla_g_qr[0], mla_g_kn[0], mla_g_kr[0])
    no_rope = jnp.concatenate([jnp.ones((n_ctx, LANES // 2), F32), jnp.zeros((n_ctx, LANES // 2), F32)], axis=-1)
    q_x, k_x, vt_x = _mla_proj(xs, mx, _rope_tables(s), mw)
    q_c, k_c, vt_c = _mla_proj(hc, mc, no_rope, mw)
    o_x = _mla_attn(q_x, k_x, vt_x, extra=(k_c, vt_c))
    o_c = _mla_attn(q_c, k_c, vt_c)
    w_o = mla_w_o[0].astype(BF16)
    xs = _ffn(xs, mx, 6, w_in, w_out, (0, 1), pro=(o_x, w_o, mx[5:6]))
    hc = _ffn(hc, mc, 6, w_in, w_out, (0, 1), pro=(o_c, w_o, mc[5:6]))

    mx = mods[1, 0].reshape(N_MOD, d)
    mc = mods[1, 1].reshape(N_MOD, d)
    xs = _ffn(xs, mx, 0, w_in, w_out, (1, 0))
    hc = _ffn(hc, mc, 0, w_in, w_out, (1, 0))
    w_p = _block_diag(pool_w[0]).astype(BF16)
    xs = _ffn(xs, mx, 6, w_in, w_out, (1, 1), pro=(_pool(xs, mx), w_p, mx[5:6] * pool_scale[0][None, :]))
    hc = _ffn(hc, mc, 6, w_in, w_out, (1, 1), pro=(_pool(hc, mc), w_p, mc[5:6] * pool_scale[0][None, :]))

    mx = mods[2, 0].reshape(N_MOD, d)
    mc = mods[2, 1].reshape(N_MOD, d)
    xs = _ffn(xs, mx, 0, w_in, w_out, (2, 0))
    hc = _ffn(hc, mc, 0, w_in, w_out, (2, 0))
    w_qkv = na_w_qkv[0].astype(BF16)
    g_q2 = jnp.tile(na_g_q[0] * (NA_SCALE * LOG2_E), 2)[None, :]
    g_k2 = jnp.tile(na_g_k[0], 2)[None, :]
    q_n, k_n, v_n = _na_proj(xs, mx, w_qkv, g_q2, g_k2)
    _, k_nc, v_nc = _na_proj(hc, mc, w_qkv, g_q2, g_k2)
    o_n = _na_attn(q_n, k_n, v_n, k_nc, v_nc, _na_bias_tables(na_rpb[0], s // GRID_W))
    xs = _ffn(xs, mx, 6, w_in, w_out, (2, 1), pro=(o_n, na_w_o[0].astype(BF16), mx[5:6]))

    mx = mods[3, 0].reshape(N_MOD, d)
    xs = _ffn(xs, mx, 0, w_in, w_out, (3, 0))
    w_ci = conv_w_in[0].astype(BF16)
    y_c = _conv(xs, mx, w_ci[:, :d], w_ci[:, d:], conv_w[0])
    xs = _ffn(xs, mx, 6, w_in, w_out, (3, 1), pro=(y_c, conv_w_out[0].astype(BF16), mx[5:6]))
    return xs[None]
```

```python
import functools

import jax
import jax.numpy as jnp
import numpy as np
from jax import lax
from jax.experimental import pallas as pl
from jax.experimental.pallas import tpu as pltpu

F32 = jnp.float32
BF16 = jnp.bfloat16

D_MODEL = 1024
DEPTH = 4
GRID_W = 64
N_MOD = 9
D_FF = 2816
EPS = 1e-6
MLA_HEADS = 8
MLA_Q_LORA = 384
MLA_KV_LORA = 256
MLA_NOPE = 128
MLA_ROPE = 64
MLA_V = 128
MLA_SCALE = (MLA_NOPE + MLA_ROPE) ** -0.5
ROPE_BASE = 10000.0
POOL_WINDOWS = (2, 4, 8, 16)
POOL_GROUP = D_MODEL // len(POOL_WINDOWS)
NA_HEADS = 16
NA_HEAD_DIM = D_MODEL // NA_HEADS
NA_ROWS = 8
NA_COLS = 16
NA_SCALE = NA_HEAD_DIM ** -0.5

LANES = 128
MXU_COLS = 256
VMEM_LIMIT = 56 * 1024 * 1024
ROW_TILE = 512
MLA_Q_TILE = 2048
FF_CHUNK = MXU_COLS
POOL_HALO = 8
CONV_HALO = 16
NA_Q_ROWS = 4
NEG_BIG = -1e30
LOG2_E = 1.4426950408889634

_NT = (((1,), (1,)), ((), ()))


def _params(n_axes, flags=None):
    return pltpu.CompilerParams(dimension_semantics=("arbitrary",) * n_axes, vmem_limit_bytes=VMEM_LIMIT, flags=flags)


def _const_spec(shape):
    nd = len(shape)
    return pl.BlockSpec(shape, lambda *_: (0,) * nd, pipeline_mode=pl.Buffered(1))


def _rms(x):
    return x * lax.rsqrt(jnp.mean(x * x, axis=-1, keepdims=True) + EPS)


def _modulate(x, mod_ref, row):
    shift = mod_ref[row:row + 1, :]
    scale = mod_ref[row + 1:row + 2, :]
    return _rms(x) * (1.0 + scale) + shift


def _dot(a, b):
    return jnp.dot(a, b, preferred_element_type=F32)


def _ada_kernel(cond_ref, w_ref, b_ref, o_ref):
    cnd = cond_ref[...]
    s = (cnd * (1.0 / (1.0 + jnp.exp(-cnd)))).astype(BF16)
    o_ref[...] = _dot(s, w_ref[...].astype(BF16)) + b_ref[...]


def _ada_params(cond, mod_w, mod_b):
    depth, d, n = mod_w.shape
    tn = n // 8
    return pl.pallas_call(
        _ada_kernel,
        grid=(depth, n // tn),
        in_specs=[
            pl.BlockSpec((8, d), lambda i, j: (0, 0)),
            pl.BlockSpec((None, d, tn), lambda i, j: (i, 0, j)),
            pl.BlockSpec((None, 1, tn), lambda i, j: (i, 0, j)),
        ],
        out_specs=pl.BlockSpec((None, 8, tn), lambda i, j: (i, 0, j)),
        out_shape=jax.ShapeDtypeStruct((depth, 8, n), F32),
        compiler_params=_params(2),
        name="ada_params",
    )(cond, mod_w, mod_b.reshape(depth, 1, n))


def _ffn_kernel(has_pro, mod_row, *refs):
    if has_pro:
        x_ref, y_ref, wp_ref, gp_ref, mod_ref, wi_ref, wo_ref, o_ref, a_ref = refs
    else:
        x_ref, mod_ref, wi_ref, wo_ref, o_ref, a_ref = refs
    x = x_ref[...]
    if has_pro:
        x = x + gp_ref[...] * _dot(y_ref[...], wp_ref[...])
    h = _modulate(x, mod_ref, mod_row).astype(BF16)
    for j in range(D_FF // FF_CHUNK):
        g = _dot(h, wi_ref[:, j * FF_CHUNK:(j + 1) * FF_CHUNK])
        u = _dot(h, wi_ref[:, D_FF + j * FF_CHUNK:D_FF + (j + 1) * FF_CHUNK])
        a_ref[:, j * FF_CHUNK:(j + 1) * FF_CHUNK] = (g * (1.0 / (1.0 + jnp.exp(-g))) * u).astype(BF16)
    gate = mod_ref[mod_row + 2:mod_row + 3, :]
    o_ref[...] = x + (0.5 * gate) * _dot(a_ref[...], wo_ref[...])


def _slab_spec(shape, lead):
    rest = len(shape) - len(lead)
    return pl.BlockSpec((None,) * len(lead) + tuple(shape[len(lead):]), lambda *_: tuple(lead) + (0,) * rest,
                        pipeline_mode=pl.Buffered(1))


def _ffn(x, mod, mod_row, w_in, w_out, which, pro=None):
    t, d = x.shape
    tm = min(ROW_TILE, t)
    row = lambda i: (i, 0)
    in_specs = [pl.BlockSpec((tm, d), row)]
    args = [x]
    if pro is not None:
        y, w_p, g_p = pro
        in_specs += [pl.BlockSpec((tm, d), row), _const_spec(w_p.shape), _const_spec(g_p.shape)]
        args += [y, w_p, g_p]
    in_specs += [_const_spec(mod.shape), _slab_spec(w_in.shape, which), _slab_spec(w_out.shape, which)]
    args += [mod, w_in, w_out]
    return pl.pallas_call(
        functools.partial(_ffn_kernel, pro is not None, mod_row),
        grid=(t // tm,),
        in_specs=in_specs,
        out_specs=pl.BlockSpec((tm, d), row),
        out_shape=jax.ShapeDtypeStruct((t, d), F32),
        scratch_shapes=[pltpu.VMEM((tm, D_FF), BF16)],
        compiler_params=_params(1),
        name="ffn_pro" if pro is not None else "ffn",
    )(*args)


def _rope_pair(r2, gain, cs, first_half):
    ms = jnp.sum(jnp.where(first_half, r2 * r2, 0.0), axis=-1, keepdims=True) * (1.0 / MLA_ROPE)
    t = r2 * lax.rsqrt(ms + EPS) * gain * cs
    return t + pltpu.roll(t, MLA_ROPE, axis=1)


def _mla_proj_kernel(x_ref, mod_ref, cs_ref, wdq_ref, gdq_ref, wuq_ref, gq_ref, wdkv_ref, gdkv_ref, wuk_ref,
                     gk_ref, wuvt_ref, q_ref, k_ref, vt_ref):
    h = _modulate(x_ref[...], mod_ref, 3).astype(BF16)
    cs = cs_ref[...]
    first_half = lax.broadcasted_iota(jnp.int32, (1, LANES), 1) < MLA_ROPE
    cq = (_rms(_dot(h, wdq_ref[...])) * gdq_ref[...]).astype(BF16)
    q = _dot(cq, wuq_ref[...])
    for hd in range(MLA_HEADS):
        c0 = hd * 2 * LANES
        qn = q[:, c0:c0 + LANES]
        q_ref[hd, :, 0:LANES] = (_rms(qn) * gq_ref[:, c0:c0 + LANES]).astype(BF16)
        rot = _rope_pair(q[:, c0 + LANES:c0 + 2 * LANES], gq_ref[:, c0 + LANES:c0 + 2 * LANES], cs, first_half)
        q_ref[hd, :, LANES:2 * LANES] = rot.astype(BF16)
    kv = _dot(h, wdkv_ref[...])
    ckv = (_rms(kv[:, :MLA_KV_LORA]) * gdkv_ref[...]).astype(BF16)
    rot = _rope_pair(kv[:, MLA_KV_LORA:], gk_ref[:, LANES:2 * LANES], cs, first_half)
    kr = jnp.where(first_half, rot, 0.0).astype(BF16)
    kn = _dot(ckv, wuk_ref[...])
    vt = lax.dot_general(wuvt_ref[...], ckv, _NT, preferred_element_type=F32)
    for hd in range(MLA_HEADS):
        blk = kn[:, hd * LANES:(hd + 1) * LANES]
        k_ref[hd, :, 0:LANES] = (_rms(blk) * gk_ref[:, 0:LANES]).astype(BF16)
        k_ref[hd, :, LANES:2 * LANES] = kr
        vt_ref[hd] = vt[hd * MLA_V:(hd + 1) * MLA_V, :].astype(BF16)


def _mla_proj(x, mod, cs, w):
    t, d = x.shape
    tm = min(ROW_TILE, t)
    consts = [w["w_dq"], w["g_dq"], w["w_uq"], w["g_q"], w["w_dkv"], w["g_dkv"], w["w_uk"], w["g_k"], w["w_uvt"]]
    return pl.pallas_call(
        _mla_proj_kernel,
        grid=(t // tm,),
        in_specs=[pl.BlockSpec((tm, d), lambda i: (i, 0)), _const_spec(mod.shape),
                  pl.BlockSpec((tm, LANES), lambda i: (i, 0))] + [_const_spec(a.shape) for a in consts],
        out_specs=[
            pl.BlockSpec((MLA_HEADS, tm, 2 * LANES), lambda i: (0, i, 0)),
            pl.BlockSpec((MLA_HEADS, tm, 2 * LANES), lambda i: (0, i, 0)),
            pl.BlockSpec((MLA_HEADS, None, MLA_V, tm), lambda i: (0, i, 0, 0)),
        ],
        out_shape=[
            jax.ShapeDtypeStruct((MLA_HEADS, t, 2 * LANES), BF16),
            jax.ShapeDtypeStruct((MLA_HEADS, t, 2 * LANES), BF16),
            jax.ShapeDtypeStruct((MLA_HEADS, t // tm, MLA_V, tm), BF16),
        ],
        compiler_params=_params(1),
        name="mla_proj",
    )(x, mod, cs, *consts)


def _mla_attn_kernel(n_main, sub, tv, has_extra, *refs):
    if has_extra:
        q_ref, k_ref, vt_ref, ke_ref, vte_ref, o_ref, acc_ref, s_ref = refs
    else:
        q_ref, k_ref, vt_ref, o_ref, acc_ref, s_ref = refs
    q = q_ref[...]
    tq = q.shape[0]
    tk = sub * tv
    acc_ref[...] = jnp.zeros_like(acc_ref)

    def scores(j):
        k = k_ref[pl.ds(pl.multiple_of(j * tk, tk), tk), :]
        return lax.dot_general(k, q, _NT, preferred_element_type=F32)

    def main_vts(j):
        return [vt_ref[j * sub + c] for c in range(sub)]

    def produce(slot, j):
        sc = scores(j)
        s_ref[slot] = sc
        return jnp.max(sc, axis=0, keepdims=True)

    def update(carry, s, s_max, vts):
        m, l = carry
        m_new = jnp.maximum(m, s_max)
        alpha = jnp.exp2(m - m_new)
        p = jnp.exp2(s - m_new)
        l = alpha * l + jnp.sum(p, axis=0, keepdims=True)
        pb = p.astype(BF16)
        pv = _dot(vts[0], pb[0:vts[0].shape[1], :])
        for c in range(1, len(vts)):
            pv = pv + _dot(vts[c], pb[c * tv:(c + 1) * tv, :])
        acc_ref[...] = alpha * acc_ref[...] + pv
        return m_new, l

    carry = (jnp.full((1, tq), NEG_BIG, F32), jnp.zeros((1, tq), F32))
    max0 = produce(0, 0)
    if n_main > 1:
        assert n_main % 2 == 0

        def body(jj, state):
            carry, max0 = state
            j = 2 * jj
            max1 = produce(1, j + 1)
            carry = update(carry, s_ref[0], max0, main_vts(j))
            max0 = produce(0, j + 2)
            return update(carry, s_ref[1], max1, main_vts(j + 1)), max0

        carry, max0 = lax.fori_loop(0, n_main // 2 - 1, body, (carry, max0))
        max1 = produce(1, n_main - 1)
        carry = update(carry, s_ref[0], max0, main_vts(n_main - 2))
        carry = update(carry, s_ref[1], max1, main_vts(n_main - 1))
    else:
        carry = update(carry, s_ref[0], max0, main_vts(0))
    if has_extra:
        s_e = lax.dot_general(ke_ref[...], q, _NT, preferred_element_type=F32)
        carry = update(carry, s_e, jnp.max(s_e, axis=0, keepdims=True), [vte_ref[...]])
    o_ref[...] = (acc_ref[...] * (1.0 / carry[1])).T.astype(BF16)


def _mla_attn(q, k, vt, extra=None):
    nh, tq_all, dk = q.shape
    tk_all = k.shape[1]
    n_v, tv = vt.shape[1], vt.shape[3]
    sub = 2 if n_v % 2 == 0 else 1
    tq = min(MLA_Q_TILE, tq_all)
    in_specs = [
        pl.BlockSpec((None, tq, dk), lambda h, i: (h, i, 0)),
        pl.BlockSpec((None, tk_all, dk), lambda h, i: (h, 0, 0), pipeline_mode=pl.Buffered(1)),
        pl.BlockSpec((None, n_v, MLA_V, tv), lambda h, i: (h, 0, 0, 0), pipeline_mode=pl.Buffered(1)),
    ]
    args = [q, k, vt]
    if extra is not None:
        k_e, vt_e = extra
        in_specs += [pl.BlockSpec((None,) + k_e.shape[1:], lambda h, i: (h, 0, 0)),
                     pl.BlockSpec((None, None) + vt_e.shape[2:], lambda h, i: (h, 0, 0, 0))]
        args += [k_e, vt_e]
    return pl.pallas_call(
        functools.partial(_mla_attn_kernel, n_v // sub, sub, tv, extra is not None),
        grid=(nh, tq_all // tq),
        in_specs=in_specs,
        out_specs=pl.BlockSpec((tq, MLA_V), lambda h, i: (i, h)),
        out_shape=jax.ShapeDtypeStruct((tq_all, nh * MLA_V), BF16),
        scratch_shapes=[pltpu.VMEM((MLA_V, tq), F32), pltpu.VMEM((2, sub * tv, tq), F32)],
        compiler_params=_params(2),
        name="mla_attn",
    )(*args)


def _pool_kernel(t_total, x_ref, xp_ref, xn_ref, mod_ref, o_ref, ext_ref):
    tm = x_ref.shape[0]
    base = pl.program_id(0) * tm
    hc = _modulate(x_ref[...], mod_ref, 3)
    halo_rows = lax.broadcasted_iota(jnp.int32, (POOL_HALO, 1), 0)
    hp = jnp.where(base - POOL_HALO + halo_rows >= 0, _modulate(xp_ref[...], mod_ref, 3), 0.0)
    hn = jnp.where(base + tm + halo_rows < t_total, _modulate(xn_ref[...], mod_ref, 3), 0.0)
    ext_ref[0:POOL_HALO, :] = hp
    ext_ref[POOL_HALO:POOL_HALO + tm, :] = hc
    ext_ref[POOL_HALO + tm:, :] = hn
    tok = base + lax.broadcasted_iota(jnp.int32, (tm, 1), 0)
    for g, win in enumerate(POOL_WINDOWS):
        half = win // 2
        cols = slice(g * POOL_GROUP, (g + 1) * POOL_GROUP)
        acc = ext_ref[POOL_HALO - half:POOL_HALO - half + tm, cols]
        for j in range(-half + 1, half):
            acc = acc + ext_ref[POOL_HALO + j:POOL_HALO + j + tm, cols]
        cnt = (jnp.minimum(tok + half, t_total) - jnp.maximum(tok - half, 0)).astype(F32)
        o_ref[:, cols] = (acc / cnt - hc[:, cols]).astype(BF16)


def _pool(x, mod):
    t, d = x.shape
    tm = min(ROW_TILE, t)
    per = tm // POOL_HALO
    last = t // POOL_HALO - 1
    return pl.pallas_call(
        functools.partial(_pool_kernel, t),
        grid=(t // tm,),
        in_specs=[
            pl.BlockSpec((tm, d), lambda i: (i, 0)),
            pl.BlockSpec((POOL_HALO, d), lambda i: (jnp.maximum(i * per - 1, 0), 0)),
            pl.BlockSpec((POOL_HALO, d), lambda i: (jnp.minimum((i + 1) * per, last), 0)),
            _const_spec(mod.shape),
        ],
        out_specs=pl.BlockSpec((tm, d), lambda i: (i, 0)),
        out_shape=jax.ShapeDtypeStruct((t, d), BF16),
        scratch_shapes=[pltpu.VMEM((tm + 2 * POOL_HALO, d), F32)],
        compiler_params=_params(1),
        name="pool",
    )(x, x, x, mod)


def _conv_kernel(t_total, x_ref, xp_ref, xn_ref, mod_ref, wb_ref, wcu_ref, wconv_ref, o_ref, ext_ref, cu_ref):
    tm, d = x_ref.shape
    base = pl.program_id(0) * tm
    ext_ref[0:CONV_HALO, :] = _modulate(xp_ref[...], mod_ref, 3).astype(BF16)
    ext_ref[CONV_HALO:CONV_HALO + tm, :] = _modulate(x_ref[...], mod_ref, 3).astype(BF16)
    ext_ref[CONV_HALO + tm:, :] = _modulate(xn_ref[...], mod_ref, 3).astype(BF16)
    ext = ext_ref[...]
    cu = _dot(ext, wcu_ref[:, 0:d]) * _dot(ext, wcu_ref[:, d:2 * d])
    tok = base - CONV_HALO + lax.broadcasted_iota(jnp.int32, (tm + 2 * CONV_HALO, 1), 0)
    cu_ref[...] = jnp.where((tok >= 0) & (tok < t_total), cu, 0.0)
    z = (wconv_ref[0:1, :] * cu_ref[CONV_HALO - 1:CONV_HALO - 1 + tm, :]
         + wconv_ref[1:2, :] * cu_ref[CONV_HALO:CONV_HALO + tm, :]
         + wconv_ref[2:3, :] * cu_ref[CONV_HALO + 1:CONV_HALO + 1 + tm, :])
    b = _dot(ext_ref[CONV_HALO:CONV_HALO + tm, :], wb_ref[...])
    o_ref[...] = (b * z).astype(BF16)


def _conv(x, mod, w_b, w_cu, w_conv):
    t, d = x.shape
    tm = min(ROW_TILE, t)
    per = tm // CONV_HALO
    last = t // CONV_HALO - 1
    return pl.pallas_call(
        functools.partial(_conv_kernel, t),
        grid=(t // tm,),
        in_specs=[
            pl.BlockSpec((tm, d), lambda i: (i, 0)),
            pl.BlockSpec((CONV_HALO, d), lambda i: (jnp.maximum(i * per - 1, 0), 0)),
            pl.BlockSpec((CONV_HALO, d), lambda i: (jnp.minimum((i + 1) * per, last), 0)),
            _const_spec(mod.shape), _const_spec(w_b.shape), _const_spec(w_cu.shape), _const_spec(w_conv.shape),
        ],
        out_specs=pl.BlockSpec((tm, d), lambda i: (i, 0)),
        out_shape=jax.ShapeDtypeStruct((t, d), BF16),
        scratch_shapes=[pltpu.VMEM((tm + 2 * CONV_HALO, d), BF16), pltpu.VMEM((tm + 2 * CONV_HALO, d), F32)],
        compiler_params=_params(1),
        name="conv",
    )(x, x, x, mod, w_b, w_cu, w_conv)


def _head_rms(v, first_half):
    outs = []
    for c in range(v.shape[1] // LANES):
        blk = v[:, c * LANES:(c + 1) * LANES]
        sq = blk * blk
        tot = jnp.sum(sq, axis=-1, keepdims=True)
        lo = jnp.sum(jnp.where(first_half, sq, 0.0), axis=-1, keepdims=True)
        ms = jnp.where(first_half, lo, tot - lo) * (1.0 / NA_HEAD_DIM)
        outs.append(blk * lax.rsqrt(ms + EPS))
    return outs


def _na_proj_kernel(x_ref, mod_ref, w_ref, gq_ref, gk_ref, q_ref, k_ref, v_ref):
    d = x_ref.shape[1]
    h = _modulate(x_ref[...], mod_ref, 3).astype(BF16)
    first_half = lax.broadcasted_iota(jnp.int32, (1, LANES), 1) < NA_HEAD_DIM
    for c, blk in enumerate(_head_rms(_dot(h, w_ref[:, 0:d]), first_half)):
        q_ref[:, c * LANES:(c + 1) * LANES] = (blk * gq_ref[...]).astype(BF16)
    for c, blk in enumerate(_head_rms(_dot(h, w_ref[:, d:2 * d]), first_half)):
        k_ref[:, c * LANES:(c + 1) * LANES] = (blk * gk_ref[...]).astype(BF16)
    v_ref[...] = _dot(h, w_ref[:, 2 * d:3 * d]).astype(BF16)


def _na_proj(x, mod, w_qkv, g_q2, g_k2):
    t, d = x.shape
    tm = min(ROW_TILE, t)
    spec = pl.BlockSpec((tm, d), lambda i: (i, 0))
    return pl.pallas_call(
        _na_proj_kernel,
        grid=(t // tm,),
        in_specs=[spec, _const_spec(mod.shape), _const_spec(w_qkv.shape), _const_spec(g_q2.shape),
                  _const_spec(g_k2.shape)],
        out_specs=[spec, spec, spec],
        out_shape=[jax.ShapeDtypeStruct((t, d), BF16)] * 3,
        compiler_params=_params(1),
        name="na_proj",
    )(x, mod, w_qkv, g_q2, g_k2)


def _na_attn_kernel(q_ref, k0_ref, k1_ref, k2_ref, v0_ref, v1_ref, v2_ref, kx_ref, vx_ref, bias_ref, o_ref):
    nq = q_ref.shape[0]
    first_half = lax.broadcasted_iota(jnp.int32, (1, LANES), 1) < NA_HEAD_DIM
    n_pairs = NA_HEADS // 2

    def scores(p):
        cols = slice(p * LANES, (p + 1) * LANES)
        qp = q_ref[:, cols]
        zero = jnp.zeros_like(qp)
        qs = jnp.concatenate([jnp.where(first_half, qp, zero), jnp.where(first_half, zero, qp)], axis=0)
        parts = []
        for t, k_ref in enumerate((k0_ref, k1_ref, k2_ref)):
            lanes = slice(t * nq, (t + 1) * nq)
            bias = jnp.concatenate([bias_ref[2 * p, :, lanes], bias_ref[2 * p + 1, :, lanes]], axis=0)
            parts.append(lax.dot_general(qs, k_ref[:, cols], _NT, preferred_element_type=F32) + bias)
        parts.append(lax.dot_general(qs, kx_ref[:, cols], _NT, preferred_element_type=F32))
        return parts

    def finish(p, parts):
        cols = slice(p * LANES, (p + 1) * LANES)
        m = jnp.max(jnp.maximum(jnp.maximum(parts[0], parts[1]), jnp.maximum(parts[2], parts[3])),
                    axis=-1, keepdims=True)
        e_sum = None
        o = None
        for s, v_ref in zip(parts, (v0_ref, v1_ref, v2_ref, vx_ref)):
            e = jnp.exp2(s - m)
            part_o = _dot(e.astype(BF16), v_ref[:, cols])
            e_sum = e if e_sum is None else e_sum + e
            o = part_o if o is None else o + part_o
        o = o * (1.0 / jnp.sum(e_sum, axis=-1, keepdims=True))
        o_ref[:, cols] = jnp.where(first_half, o[0:nq], o[nq:2 * nq]).astype(BF16)

    parts_next = scores(0)
    for p in range(n_pairs):
        parts = parts_next
        if p + 1 < n_pairs:
            parts_next = scores(p + 1)
        finish(p, parts)


def _na_attn(q, k, v, k_ctx, v_ctx, bias):
    s, d = q.shape
    chunk = NA_Q_ROWS * GRID_W
    n_steps = s // chunk

    def band(t):
        return pl.BlockSpec((chunk, d), lambda i: (jnp.clip(i - 1, 0, n_steps - 3) + t, 0))

    cur = pl.BlockSpec((chunk, d), lambda i: (i, 0))
    bias_spec = pl.BlockSpec((None,) + bias.shape[1:],
                             lambda i: (jnp.where(i == 0, 0, jnp.where(i == n_steps - 1, 2, 1)), 0, 0, 0),
                             pipeline_mode=pl.Buffered(1))
    return pl.pallas_call(
        _na_attn_kernel,
        grid=(n_steps,),
        in_specs=[cur, band(0), band(1), band(2), band(0), band(1), band(2), _const_spec(k_ctx.shape),
                  _const_spec(v_ctx.shape), bias_spec],
        out_specs=cur,
        out_shape=jax.ShapeDtypeStruct((s, d), BF16),
        compiler_params=_params(1),
        name="na_attn",
    )(q, k, k, k, v, v, v, k_ctx, v_ctx, bias)


def _na_band_pattern(step, n_rows):
    n_steps = n_rows // NA_Q_ROWS
    band0 = NA_Q_ROWS * int(np.clip(step - 1, 0, n_steps - 3))
    r = step * NA_Q_ROWS + np.arange(NA_Q_ROWS)[:, None]
    key_row = band0 + np.arange(3 * NA_Q_ROWS)[None, :]
    win0 = np.clip(r - NA_ROWS // 2, 0, n_rows - NA_ROWS)
    valid = (key_row >= win0) & (key_row < win0 + NA_ROWS)
    return np.where(valid, key_row - r + (NA_ROWS - 1), 0), valid


def _na_bias_tables(rpb, n_rows):
    n_steps = n_rows // NA_Q_ROWS
    assert n_steps >= 4
    patterns = [_na_band_pattern(t, n_rows) for t in range(n_steps)]
    for dr, valid in patterns[2:-1]:
        assert np.array_equal(dr, patterns[1][0]) and np.array_equal(valid, patterns[1][1])
    cols = np.arange(GRID_W)
    col_start = np.clip(cols - NA_COLS // 2, 0, GRID_W - NA_COLS)
    kc = np.arange(GRID_W)
    col_ok = (kc[None, :] >= col_start[:, None]) & (kc[None, :] < col_start[:, None] + NA_COLS)
    pad = GRID_W - NA_COLS
    padded = jnp.pad(rpb * LOG2_E, ((0, 0), (0, 0), (pad, pad)))
    toeplitz = jnp.stack([padded[:, :, GRID_W - 1 - c:2 * GRID_W - 1 - c] for c in range(GRID_W)], axis=2)
    toeplitz = jnp.where(col_ok[None, None], toeplitz, NEG_BIG)
    n_band = 3 * NA_Q_ROWS
    lead = NA_Q_ROWS
    n_dr = toeplitz.shape[1]
    strip = jnp.transpose(toeplitz, (0, 2, 1, 3)).reshape(rpb.shape[0], GRID_W, n_dr * GRID_W)
    strip = jnp.pad(strip, ((0, 0), (0, 0), (lead * GRID_W, lead * GRID_W)), constant_values=NEG_BIG)
    tables = []
    for dr, valid in (patterns[0], patterns[1], patterns[-1]):
        blocks = []
        for jq in range(NA_Q_ROWS):
            i0 = int(np.argmax(valid[jq]))
            start = int(dr[jq, i0]) - i0 + lead
            assert 0 <= start and start + n_band <= n_dr + 2 * lead
            assert all(dr[jq, i] == start - lead + i for i in range(n_band) if valid[jq, i])
            window = strip[:, :, start * GRID_W:(start + n_band) * GRID_W]
            row_ok = np.repeat(valid[jq], GRID_W)
            blocks.append(jnp.where(row_ok[None, None, :], window, NEG_BIG))
        tables.append(jnp.concatenate(blocks, axis=-2))
    return jnp.stack(tables, axis=0)


def _rope_tables(t):
    pos = jnp.arange(t)
    row = (pos // GRID_W).astype(F32)
    col = (pos % GRID_W).astype(F32)
    n = MLA_ROPE // 4
    freqs = ROPE_BASE ** (-jnp.arange(n, dtype=F32) / n)
    ang = jnp.concatenate([row[:, None] * freqs, col[:, None] * freqs], axis=-1)
    cos, sin = jnp.cos(ang), jnp.sin(ang)
    return jnp.concatenate([cos, cos, sin, sin], axis=-1)


_HALF_SPLIT = np.concatenate([np.arange(0, MLA_ROPE, 2), np.arange(1, MLA_ROPE, 2)])


def _rope_cols(w):
    hs = w[..., _HALF_SPLIT]
    return jnp.concatenate([hs, -hs[..., MLA_ROPE // 2:], hs[..., :MLA_ROPE // 2]], axis=-1)


def _rope_gain(g):
    hs = g[_HALF_SPLIT]
    return jnp.concatenate([hs, hs[MLA_ROPE // 2:], hs[:MLA_ROPE // 2]])


def _mla_weights(w_dq, g_dq, w_uq, w_dkv, g_dkv, w_uk, w_uv, g_qn, g_qr, g_kn, g_kr):
    w_uq_ext = jnp.concatenate([w_uq[..., :MLA_NOPE], _rope_cols(w_uq[..., MLA_NOPE:])], axis=-1)
    g_q = jnp.tile(jnp.concatenate([g_qn, _rope_gain(g_qr)]) * (MLA_SCALE * LOG2_E), MLA_HEADS)
    return {
        "w_dq": w_dq.astype(BF16),
        "g_dq": g_dq[None, :],
        "w_uq": w_uq_ext.reshape(MLA_Q_LORA, -1).astype(BF16),
        "g_q": g_q[None, :],
        "w_dkv": jnp.concatenate([w_dkv[:, :MLA_KV_LORA], _rope_cols(w_dkv[:, MLA_KV_LORA:])], axis=-1).astype(BF16),
        "g_dkv": g_dkv[None, :],
        "w_uk": w_uk.reshape(MLA_KV_LORA, -1).astype(BF16),
        "g_k": jnp.concatenate([g_kn, _rope_gain(g_kr)])[None, :],
        "w_uvt": w_uv.reshape(MLA_KV_LORA, -1).T.astype(BF16),
    }


def _block_diag(w):
    g, c, _ = w.shape
    out = jnp.zeros((g * c, g * c), w.dtype)
    for i in range(g):
        out = out.at[i * c:(i + 1) * c, i * c:(i + 1) * c].set(w[i])
    return out


def kernel(x, c, ctx, c_ctx, mod_w, mod_b, ffn_w_in, ffn_w_out, mla_w_dq, mla_g_dq, mla_w_uq, mla_w_dkv, mla_g_dkv, mla_w_uk, mla_w_uv, mla_g_qn, mla_g_qr, mla_g_kn, mla_g_kr, mla_w_o, pool_w, pool_scale, na_w_qkv, na_g_q, na_g_k, na_rpb, na_w_o, conv_w_in, conv_w, conv_w_out):
    assert x.shape[0] == 1 and x.shape[2] == D_MODEL and x.shape[1] % ROW_TILE == 0
    s = x.shape[1]
    d = D_MODEL
    xs = x[0]
    hc = ctx[0]
    n_ctx = hc.shape[0]

    cond = jnp.zeros((8, d), F32).at[0].set(c[0]).at[1].set(c_ctx)
    mods = _ada_params(cond, mod_w, mod_b)
    w_in = ffn_w_in.astype(BF16)
    w_out = ffn_w_out.astype(BF16)

    mx = mods[0, 0].reshape(N_MOD, d)
    mc = mods[0, 1].reshape(N_MOD, d)
    xs = _ffn(xs, mx, 0, w_in, w_out, (0, 0))
    hc = _ffn(hc, mc, 0, w_in, w_out, (0, 0))
    mw = _mla_weights(mla_w_dq[0], mla_g_dq[0], mla_w_uq[0], mla_w_dkv[0], mla_g_dkv[0], mla_w_uk[0], mla_w_uv[0],
                      mla_g_qn[0], mla_g_qr[0], mla_g_kn[0], mla_g_kr[0])
    no_rope = jnp.concatenate([jnp.ones((n_ctx, LANES // 2), F32), jnp.zeros((n_ctx, LANES // 2), F32)], axis=-1)
    q_x, k_x, vt_x = _mla_proj(xs, mx, _rope_tables(s), mw)
    q_c, k_c, vt_c = _mla_proj(hc, mc, no_rope, mw)
    o_x = _mla_attn(q_x, k_x, vt_x, extra=(k_c, vt_c))
    o_c = _mla_attn(q_c, k_c, vt_c)
    w_o = mla_w_o[0].astype(BF16)
    xs = _ffn(xs, mx, 6, w_in, w_out, (0, 1), pro=(o_x, w_o, mx[5:6]))
    hc = _ffn(hc, mc, 6, w_in, w_out, (0, 1), pro=(o_c, w_o, mc[5:6]))

    mx = mods[1, 0].reshape(N_MOD, d)
    mc = mods[1, 1].reshape(N_MOD, d)
    xs = _ffn(xs, mx, 0, w_in, w_out, (1, 0))
    hc = _ffn(hc, mc, 0, w_in, w_out, (1, 0))
    w_p = _block_diag(pool_w[0]).astype(BF16)
    xs = _ffn(xs, mx, 6, w_in, w_out, (1, 1), pro=(_pool(xs, mx), w_p, mx[5:6] * pool_scale[0][None, :]))
    hc = _ffn(hc, mc, 6, w_in, w_out, (1, 1), pro=(_pool(hc, mc), w_p, mc[5:6] * pool_scale[0][None, :]))

    mx = mods[2, 0].reshape(N_MOD, d)
    mc = mods[2, 1].reshape(N_MOD, d)
    xs = _ffn(xs, mx, 0, w_in, w_out, (2, 0))
    hc = _ffn(hc, mc, 0, w_in, w_out, (2, 0))
    w_qkv = na_w_qkv[0].astype(BF16)
    g_q2 = jnp.tile(na_g_q[0] * (NA_SCALE * LOG2_E), 2)[None, :]
    g_k2 = jnp.tile(na_g_k[0], 2)[None, :]
    q_n, k_n, v_n = _na_proj(xs, mx, w_qkv, g_q2, g_k2)
    _, k_nc, v_nc = _na_proj(hc, mc, w_qkv, g_q2, g_k2)
    o_n = _na_attn(q_n, k_n, v_n, k_nc, v_nc, _na_bias_tables(na_rpb[0], s // GRID_W))
    xs = _ffn(xs, mx, 6, w_in, w_out, (2, 1), pro=(o_n, na_w_o[0].astype(BF16), mx[5:6]))

    mx = mods[3, 0].reshape(N_MOD, d)
    xs = _ffn(xs, mx, 0, w_in, w_out, (3, 0))
    w_ci = conv_w_in[0].astype(BF16)
    y_c = _conv(xs, mx, w_ci[:, :d], w_ci[:, d:], conv_w[0])
    xs = _ffn(xs, mx, 6, w_in, w_out, (3, 1), pro=(y_c, conv_w_out[0].astype(BF16), mx[5:6]))
    return xs[None]
```

```python
import functools

import jax
import jax.numpy as jnp
import numpy as np
from jax import lax
from jax.experimental import pallas as pl
from jax.experimental.pallas import tpu as pltpu

F32 = jnp.float32
BF16 = jnp.bfloat16

D_MODEL = 1024
DEPTH = 4
GRID_W = 64
N_MOD = 9
D_FF = 2816
EPS = 1e-6
MLA_HEADS = 8
MLA_Q_LORA = 384
MLA_KV_LORA = 256
MLA_NOPE = 128
MLA_ROPE = 64
MLA_V = 128
MLA_SCALE = (MLA_NOPE + MLA_ROPE) ** -0.5
ROPE_BASE = 10000.0
POOL_WINDOWS = (2, 4, 8, 16)
POOL_GROUP = D_MODEL // len(POOL_WINDOWS)
NA_HEADS = 16
NA_HEAD_DIM = D_MODEL // NA_HEADS
NA_ROWS = 8
NA_COLS = 16
NA_SCALE = NA_HEAD_DIM ** -0.5

LANES = 128
MXU_COLS = 256
VMEM_LIMIT = 56 * 1024 * 1024
ROW_TILE = 512
MLA_Q_TILE = 2048
FF_CHUNK = MXU_COLS
POOL_HALO = 8
CONV_HALO = 16
NA_Q_ROWS = 4
NEG_BIG = -1e30
LOG2_E = 1.4426950408889634
SCORE_RANGE_LOG2 = 60.0
BOUND_SLACK = 1.01

_NT = (((1,), (1,)), ((), ()))


def _params(n_axes, flags=None):
    return pltpu.CompilerParams(dimension_semantics=("arbitrary",) * n_axes, vmem_limit_bytes=VMEM_LIMIT, flags=flags)


def _const_spec(shape):
    nd = len(shape)
    return pl.BlockSpec(shape, lambda *_: (0,) * nd, pipeline_mode=pl.Buffered(1))


def _rms(x):
    return x * lax.rsqrt(jnp.mean(x * x, axis=-1, keepdims=True) + EPS)


def _modulate(x, mod_ref, row):
    shift = mod_ref[row:row + 1, :]
    scale = mod_ref[row + 1:row + 2, :]
    return _rms(x) * (1.0 + scale) + shift


def _dot(a, b):
    return jnp.dot(a, b, preferred_element_type=F32)


def _ada_kernel(cond_ref, w_ref, b_ref, o_ref):
    cnd = cond_ref[...]
    s = (cnd * (1.0 / (1.0 + jnp.exp(-cnd)))).astype(BF16)
    o_ref[...] = _dot(s, w_ref[...].astype(BF16)) + b_ref[...]


def _ada_params(cond, mod_w, mod_b):
    depth, d, n = mod_w.shape
    tn = n // 8
    return pl.pallas_call(
        _ada_kernel,
        grid=(depth, n // tn),
        in_specs=[
            pl.BlockSpec((8, d), lambda i, j: (0, 0)),
            pl.BlockSpec((None, d, tn), lambda i, j: (i, 0, j)),
            pl.BlockSpec((None, 1, tn), lambda i, j: (i, 0, j)),
        ],
        out_specs=pl.BlockSpec((None, 8, tn), lambda i, j: (i, 0, j)),
        out_shape=jax.ShapeDtypeStruct((depth, 8, n), F32),
        compiler_params=_params(2),
        name="ada_params",
    )(cond, mod_w, mod_b.reshape(depth, 1, n))


def _ffn_kernel(has_pro, mod_row, *refs):
    if has_pro:
        x_ref, y_ref, wp_ref, gp_ref, mod_ref, wi_ref, wo_ref, o_ref, a_ref = refs
    else:
        x_ref, mod_ref, wi_ref, wo_ref, o_ref, a_ref = refs
    x = x_ref[...]
    if has_pro:
        x = x + gp_ref[...] * _dot(y_ref[...], wp_ref[...])
    h = _modulate(x, mod_ref, mod_row).astype(BF16)
    for j in range(D_FF // FF_CHUNK):
        g = _dot(h, wi_ref[:, j * FF_CHUNK:(j + 1) * FF_CHUNK])
        u = _dot(h, wi_ref[:, D_FF + j * FF_CHUNK:D_FF + (j + 1) * FF_CHUNK])
        a_ref[:, j * FF_CHUNK:(j + 1) * FF_CHUNK] = (g * (1.0 / (1.0 + jnp.exp(-g))) * u).astype(BF16)
    gate = mod_ref[mod_row + 2:mod_row + 3, :]
    o_ref[...] = x + (0.5 * gate) * _dot(a_ref[...], wo_ref[...])


def _slab_spec(shape, lead):
    rest = len(shape) - len(lead)
    return pl.BlockSpec((None,) * len(lead) + tuple(shape[len(lead):]), lambda *_: tuple(lead) + (0,) * rest,
                        pipeline_mode=pl.Buffered(1))


def _ffn(x, mod, mod_row, w_in, w_out, which, pro=None):
    t, d = x.shape
    tm = min(ROW_TILE, t)
    row = lambda i: (i, 0)
    in_specs = [pl.BlockSpec((tm, d), row)]
    args = [x]
    if pro is not None:
        y, w_p, g_p = pro
        in_specs += [pl.BlockSpec((tm, d), row), _const_spec(w_p.shape), _const_spec(g_p.shape)]
        args += [y, w_p, g_p]
    in_specs += [_const_spec(mod.shape), _slab_spec(w_in.shape, which), _slab_spec(w_out.shape, which)]
    args += [mod, w_in, w_out]
    return pl.pallas_call(
        functools.partial(_ffn_kernel, pro is not None, mod_row),
        grid=(t // tm,),
        in_specs=in_specs,
        out_specs=pl.BlockSpec((tm, d), row),
        out_shape=jax.ShapeDtypeStruct((t, d), F32),
        scratch_shapes=[pltpu.VMEM((tm, D_FF), BF16)],
        compiler_params=_params(1),
        name="ffn_pro" if pro is not None else "ffn",
    )(*args)


def _rope_pair(r2, gain, cs, first_half):
    ms = jnp.sum(jnp.where(first_half, r2 * r2, 0.0), axis=-1, keepdims=True) * (1.0 / MLA_ROPE)
    t = r2 * lax.rsqrt(ms + EPS) * gain * cs
    return t + pltpu.roll(t, MLA_ROPE, axis=1)


def _mla_proj_kernel(x_ref, mod_ref, cs_ref, wdq_ref, gdq_ref, wuq_ref, gq_ref, qpad_ref, wdkv_ref, gdkv_ref, wuk_ref,
                     gk_ref, wuvt_ref, q_ref, k_ref, vt_ref):
    h = _modulate(x_ref[...], mod_ref, 3).astype(BF16)
    cs = cs_ref[...]
    lane = lax.broadcasted_iota(jnp.int32, (1, LANES), 1)
    first_half = lane < MLA_ROPE
    cq = (_rms(_dot(h, wdq_ref[...])) * gdq_ref[...]).astype(BF16)
    q = _dot(cq, wuq_ref[...])
    for hd in range(MLA_HEADS):
        c0 = hd * 2 * LANES
        qn = q[:, c0:c0 + LANES]
        q_ref[hd, :, 0:LANES] = (_rms(qn) * gq_ref[:, c0:c0 + LANES]).astype(BF16)
        rot = _rope_pair(q[:, c0 + LANES:c0 + 2 * LANES], gq_ref[:, c0 + LANES:c0 + 2 * LANES], cs, first_half)
        q_ref[hd, :, LANES:2 * LANES] = jnp.where(first_half, rot, qpad_ref[...]).astype(BF16)
    kv = _dot(h, wdkv_ref[...])
    ckv = (_rms(kv[:, :MLA_KV_LORA]) * gdkv_ref[...]).astype(BF16)
    rot = _rope_pair(kv[:, MLA_KV_LORA:], gk_ref[:, LANES:2 * LANES], cs, first_half)
    kr = jnp.where(first_half, rot, jnp.where(lane == MLA_ROPE, 1.0, 0.0)).astype(BF16)
    kn = _dot(ckv, wuk_ref[...])
    vt = lax.dot_general(wuvt_ref[...], ckv, _NT, preferred_element_type=F32)
    for hd in range(MLA_HEADS):
        blk = kn[:, hd * LANES:(hd + 1) * LANES]
        k_ref[hd, :, 0:LANES] = (_rms(blk) * gk_ref[:, 0:LANES]).astype(BF16)
        k_ref[hd, :, LANES:2 * LANES] = kr
        vt_ref[hd] = vt[hd * MLA_V:(hd + 1) * MLA_V, :].astype(BF16)


def _mla_proj(x, mod, cs, w):
    t, d = x.shape
    tm = min(ROW_TILE, t)
    consts = [w["w_dq"], w["g_dq"], w["w_uq"], w["g_q"], w["q_pad"], w["w_dkv"], w["g_dkv"], w["w_uk"], w["g_k"], w["w_uvt"]]
    return pl.pallas_call(
        _mla_proj_kernel,
        grid=(t // tm,),
        in_specs=[pl.BlockSpec((tm, d), lambda i: (i, 0)), _const_spec(mod.shape),
                  pl.BlockSpec((tm, LANES), lambda i: (i, 0))] + [_const_spec(a.shape) for a in consts],
        out_specs=[
            pl.BlockSpec((MLA_HEADS, tm, 2 * LANES), lambda i: (0, i, 0)),
            pl.BlockSpec((MLA_HEADS, tm, 2 * LANES), lambda i: (0, i, 0)),
            pl.BlockSpec((MLA_HEADS, None, MLA_V, tm), lambda i: (0, i, 0, 0)),
        ],
        out_shape=[
            jax.ShapeDtypeStruct((MLA_HEADS, t, 2 * LANES), BF16),
            jax.ShapeDtypeStruct((MLA_HEADS, t, 2 * LANES), BF16),
            jax.ShapeDtypeStruct((MLA_HEADS, t // tm, MLA_V, tm), BF16),
        ],
        compiler_params=_params(1),
        name="mla_proj",
    )(x, mod, cs, *consts)


def _mla_attn_kernel(n_main, sub, tv, has_extra, *refs):
    if has_extra:
        q_ref, k_ref, vt_ref, ke_ref, vte_ref, o_ref, acc_ref, s_ref = refs
    else:
        q_ref, k_ref, vt_ref, o_ref, acc_ref, s_ref = refs
    q = q_ref[...]
    tq = q.shape[0]
    tk = sub * tv
    acc_ref[...] = jnp.zeros_like(acc_ref)

    def scores(j):
        k = k_ref[pl.ds(pl.multiple_of(j * tk, tk), tk), :]
        return lax.dot_general(k, q, _NT, preferred_element_type=F32)

    def main_vts(j):
        return [vt_ref[j * sub + c] for c in range(sub)]

    def produce(slot, j):
        sc = scores(j)
        s_ref[slot] = sc
        return jnp.max(sc, axis=0, keepdims=True)

    def update(carry, s, s_max, vts):
        m, l = carry
        m_new = jnp.maximum(m, s_max)
        alpha = jnp.exp2(m - m_new)
        p = jnp.exp2(s - m_new)
        l = alpha * l + jnp.sum(p, axis=0, keepdims=True)
        pb = p.astype(BF16)
        pv = _dot(vts[0], pb[0:vts[0].shape[1], :])
        for c in range(1, len(vts)):
            pv = pv + _dot(vts[c], pb[c * tv:(c + 1) * tv, :])
        acc_ref[...] = alpha * acc_ref[...] + pv
        return m_new, l

    carry = (jnp.full((1, tq), NEG_BIG, F32), jnp.zeros((1, tq), F32))
    max0 = produce(0, 0)
    if n_main > 1:
        assert n_main % 2 == 0

        def body(jj, state):
            carry, max0 = state
            j = 2 * jj
            max1 = produce(1, j + 1)
            carry = update(carry, s_ref[0], max0, main_vts(j))
            max0 = produce(0, j + 2)
            return update(carry, s_ref[1], max1, main_vts(j + 1)), max0

        carry, max0 = lax.fori_loop(0, n_main // 2 - 1, body, (carry, max0))
        max1 = produce(1, n_main - 1)
        carry = update(carry, s_ref[0], max0, main_vts(n_main - 2))
        carry = update(carry, s_ref[1], max1, main_vts(n_main - 1))
    else:
        carry = update(carry, s_ref[0], max0, main_vts(0))
    if has_extra:
        s_e = lax.dot_general(ke_ref[...], q, _NT, preferred_element_type=F32)
        carry = update(carry, s_e, jnp.max(s_e, axis=0, keepdims=True), [vte_ref[...]])
    o_ref[...] = (acc_ref[...] * (1.0 / carry[1])).T.astype(BF16)


def _mla_attn_bounded_kernel(n_main, sub, tv, has_extra, *refs):
    if has_extra:
        q_ref, k_ref, vt_ref, ke_ref, vte_ref, o_ref, acc_ref, l_ref = refs
    else:
        q_ref, k_ref, vt_ref, o_ref, acc_ref, l_ref = refs
    q = q_ref[...]
    tk = sub * tv
    acc_ref[...] = jnp.zeros_like(acc_ref)
    l_ref[...] = jnp.zeros_like(l_ref)

    def accumulate(k, vts):
        p = jnp.exp2(lax.dot_general(k, q, _NT, preferred_element_type=F32))
        l_ref[...] += jnp.sum(p, axis=0, keepdims=True)
        pb = p.astype(BF16)
        pv = _dot(vts[0], pb[0:vts[0].shape[1], :])
        for c in range(1, len(vts)):
            pv = pv + _dot(vts[c], pb[c * tv:(c + 1) * tv, :])
        acc_ref[...] += pv

    def body(j, _):
        k = k_ref[pl.ds(pl.multiple_of(j * tk, tk), tk), :]
        accumulate(k, [vt_ref[j * sub + c] for c in range(sub)])
        return 0

    lax.fori_loop(0, n_main, body, 0, unroll=2)
    if has_extra:
        accumulate(ke_ref[...], [vte_ref[...]])
    o_ref[...] = (acc_ref[...] * (1.0 / l_ref[...])).T.astype(BF16)


def _mla_attn(q, k, vt, extra=None, bounded=False):
    nh, tq_all, dk = q.shape
    tk_all = k.shape[1]
    n_v, tv = vt.shape[1], vt.shape[3]
    sub = 2 if n_v % 2 == 0 else 1
    tq = min(MLA_Q_TILE, tq_all)
    kv_mode = None if bounded else pl.Buffered(1)
    in_specs = [
        pl.BlockSpec((None, tq, dk), lambda h, i: (h, i, 0)),
        pl.BlockSpec((None, tk_all, dk), lambda h, i: (h, 0, 0), pipeline_mode=kv_mode),
        pl.BlockSpec((None, n_v, MLA_V, tv), lambda h, i: (h, 0, 0, 0), pipeline_mode=kv_mode),
    ]
    if bounded:
        body = functools.partial(_mla_attn_bounded_kernel, n_v // sub, sub, tv, extra is not None)
        scratch = [pltpu.VMEM((MLA_V, tq), F32), pltpu.VMEM((1, tq), F32)]
    else:
        body = functools.partial(_mla_attn_kernel, n_v // sub, sub, tv, extra is not None)
        scratch = [pltpu.VMEM((MLA_V, tq), F32), pltpu.VMEM((2, sub * tv, tq), F32)]
    args = [q, k, vt]
    if extra is not None:
        k_e, vt_e = extra
        in_specs += [pl.BlockSpec((None,) + k_e.shape[1:], lambda h, i: (h, 0, 0)),
                     pl.BlockSpec((None, None) + vt_e.shape[2:], lambda h, i: (h, 0, 0, 0))]
        args += [k_e, vt_e]
    return pl.pallas_call(
        body,
        grid=(nh, tq_all // tq),
        in_specs=in_specs,
        out_specs=pl.BlockSpec((tq, MLA_V), lambda h, i: (i, h)),
        out_shape=jax.ShapeDtypeStruct((tq_all, nh * MLA_V), BF16),
        scratch_shapes=scratch,
        compiler_params=_params(2),
        name="mla_attn_bounded" if bounded else "mla_attn",
    )(*args)


def _pool_kernel(t_total, x_ref, xp_ref, xn_ref, mod_ref, o_ref, ext_ref):
    tm = x_ref.shape[0]
    base = pl.program_id(0) * tm
    hc = _modulate(x_ref[...], mod_ref, 3)
    halo_rows = lax.broadcasted_iota(jnp.int32, (POOL_HALO, 1), 0)
    hp = jnp.where(base - POOL_HALO + halo_rows >= 0, _modulate(xp_ref[...], mod_ref, 3), 0.0)
    hn = jnp.where(base + tm + halo_rows < t_total, _modulate(xn_ref[...], mod_ref, 3), 0.0)
    ext_ref[0:POOL_HALO, :] = hp
    ext_ref[POOL_HALO:POOL_HALO + tm, :] = hc
    ext_ref[POOL_HALO + tm:, :] = hn
    tok = base + lax.broadcasted_iota(jnp.int32, (tm, 1), 0)
    for g, win in enumerate(POOL_WINDOWS):
        half = win // 2
        cols = slice(g * POOL_GROUP, (g + 1) * POOL_GROUP)
        acc = ext_ref[POOL_HALO - half:POOL_HALO - half + tm, cols]
        for j in range(-half + 1, half):
            acc = acc + ext_ref[POOL_HALO + j:POOL_HALO + j + tm, cols]
        cnt = (jnp.minimum(tok + half, t_total) - jnp.maximum(tok - half, 0)).astype(F32)
        o_ref[:, cols] = (acc / cnt - hc[:, cols]).astype(BF16)


def _pool(x, mod):
    t, d = x.shape
    tm = min(ROW_TILE, t)
    per = tm // POOL_HALO
    last = t // POOL_HALO - 1
    return pl.pallas_call(
        functools.partial(_pool_kernel, t),
        grid=(t // tm,),
        in_specs=[
            pl.BlockSpec((tm, d), lambda i: (i, 0)),
            pl.BlockSpec((POOL_HALO, d), lambda i: (jnp.maximum(i * per - 1, 0), 0)),
            pl.BlockSpec((POOL_HALO, d), lambda i: (jnp.minimum((i + 1) * per, last), 0)),
            _const_spec(mod.shape),
        ],
        out_specs=pl.BlockSpec((tm, d), lambda i: (i, 0)),
        out_shape=jax.ShapeDtypeStruct((t, d), BF16),
        scratch_shapes=[pltpu.VMEM((tm + 2 * POOL_HALO, d), F32)],
        compiler_params=_params(1),
        name="pool",
    )(x, x, x, mod)


def _conv_kernel(t_total, x_ref, xp_ref, xn_ref, mod_ref, wb_ref, wcu_ref, wconv_ref, o_ref, ext_ref, cu_ref):
    tm, d = x_ref.shape
    base = pl.program_id(0) * tm
    ext_ref[0:CONV_HALO, :] = _modulate(xp_ref[...], mod_ref, 3).astype(BF16)
    ext_ref[CONV_HALO:CONV_HALO + tm, :] = _modulate(x_ref[...], mod_ref, 3).astype(BF16)
    ext_ref[CONV_HALO + tm:, :] = _modulate(xn_ref[...], mod_ref, 3).astype(BF16)
    ext = ext_ref[...]
    cu = _dot(ext, wcu_ref[:, 0:d]) * _dot(ext, wcu_ref[:, d:2 * d])
    tok = base - CONV_HALO + lax.broadcasted_iota(jnp.int32, (tm + 2 * CONV_HALO, 1), 0)
    cu_ref[...] = jnp.where((tok >= 0) & (tok < t_total), cu, 0.0)
    z = (wconv_ref[0:1, :] * cu_ref[CONV_HALO - 1:CONV_HALO - 1 + tm, :]
         + wconv_ref[1:2, :] * cu_ref[CONV_HALO:CONV_HALO + tm, :]
         + wconv_ref[2:3, :] * cu_ref[CONV_HALO + 1:CONV_HALO + 1 + tm, :])
    b = _dot(ext_ref[CONV_HALO:CONV_HALO + tm, :], wb_ref[...])
    o_ref[...] = (b * z).astype(BF16)


def _conv(x, mod, w_b, w_cu, w_conv):
    t, d = x.shape
    tm = min(ROW_TILE, t)
    per = tm // CONV_HALO
    last = t // CONV_HALO - 1
    return pl.pallas_call(
        functools.partial(_conv_kernel, t),
        grid=(t // tm,),
        in_specs=[
            pl.BlockSpec((tm, d), lambda i: (i, 0)),
            pl.BlockSpec((CONV_HALO, d), lambda i: (jnp.maximum(i * per - 1, 0), 0)),
            pl.BlockSpec((CONV_HALO, d), lambda i: (jnp.minimum((i + 1) * per, last), 0)),
            _const_spec(mod.shape), _const_spec(w_b.shape), _const_spec(w_cu.shape), _const_spec(w_conv.shape),
        ],
        out_specs=pl.BlockSpec((tm, d), lambda i: (i, 0)),
        out_shape=jax.ShapeDtypeStruct((t, d), BF16),
        scratch_shapes=[pltpu.VMEM((tm + 2 * CONV_HALO, d), BF16), pltpu.VMEM((tm + 2 * CONV_HALO, d), F32)],
        compiler_params=_params(1),
        name="conv",
    )(x, x, x, mod, w_b, w_cu, w_conv)


def _head_rms(v, first_half):
    outs = []
    for c in range(v.shape[1] // LANES):
        blk = v[:, c * LANES:(c + 1) * LANES]
        sq = blk * blk
        tot = jnp.sum(sq, axis=-1, keepdims=True)
        lo = jnp.sum(jnp.where(first_half, sq, 0.0), axis=-1, keepdims=True)
        ms = jnp.where(first_half, lo, tot - lo) * (1.0 / NA_HEAD_DIM)
        outs.append(blk * lax.rsqrt(ms + EPS))
    return outs


def _na_proj_kernel(x_ref, mod_ref, w_ref, gq_ref, gk_ref, q_ref, k_ref, v_ref):
    d = x_ref.shape[1]
    h = _modulate(x_ref[...], mod_ref, 3).astype(BF16)
    first_half = lax.broadcasted_iota(jnp.int32, (1, LANES), 1) < NA_HEAD_DIM
    for c, blk in enumerate(_head_rms(_dot(h, w_ref[:, 0:d]), first_half)):
        q_ref[:, c * LANES:(c + 1) * LANES] = (blk * gq_ref[...]).astype(BF16)
    for c, blk in enumerate(_head_rms(_dot(h, w_ref[:, d:2 * d]), first_half)):
        k_ref[:, c * LANES:(c + 1) * LANES] = (blk * gk_ref[...]).astype(BF16)
    v_ref[...] = _dot(h, w_ref[:, 2 * d:3 * d]).astype(BF16)


def _na_proj(x, mod, w_qkv, g_q2, g_k2):
    t, d = x.shape
    tm = min(ROW_TILE, t)
    spec = pl.BlockSpec((tm, d), lambda i: (i, 0))
    return pl.pallas_call(
        _na_proj_kernel,
        grid=(t // tm,),
        in_specs=[spec, _const_spec(mod.shape), _const_spec(w_qkv.shape), _const_spec(g_q2.shape),
                  _const_spec(g_k2.shape)],
        out_specs=[spec, spec, spec],
        out_shape=[jax.ShapeDtypeStruct((t, d), BF16)] * 3,
        compiler_params=_params(1),
        name="na_proj",
    )(x, mod, w_qkv, g_q2, g_k2)


def _na_attn_kernel(bounded, q_ref, k0_ref, k1_ref, k2_ref, v0_ref, v1_ref, v2_ref, kx_ref, vx_ref, bias_ref,
                    xshift_ref, o_ref):
    nq = q_ref.shape[0]
    first_half = lax.broadcasted_iota(jnp.int32, (1, LANES), 1) < NA_HEAD_DIM
    n_pairs = NA_HEADS // 2

    def scores(p):
        cols = slice(p * LANES, (p + 1) * LANES)
        qp = q_ref[:, cols]
        zero = jnp.zeros_like(qp)
        qs = jnp.concatenate([jnp.where(first_half, qp, zero), jnp.where(first_half, zero, qp)], axis=0)
        parts = []
        for t, k_ref in enumerate((k0_ref, k1_ref, k2_ref)):
            lanes = slice(t * nq, (t + 1) * nq)
            bias = jnp.concatenate([bias_ref[2 * p, :, lanes], bias_ref[2 * p + 1, :, lanes]], axis=0)
            parts.append(lax.dot_general(qs, k_ref[:, cols], _NT, preferred_element_type=F32) + bias)
        parts.append(lax.dot_general(qs, kx_ref[:, cols], _NT, preferred_element_type=F32) + xshift_ref[...])
        return parts

    def finish(p, parts):
        cols = slice(p * LANES, (p + 1) * LANES)
        if not bounded:
            m = jnp.max(jnp.maximum(jnp.maximum(parts[0], parts[1]), jnp.maximum(parts[2], parts[3])),
                        axis=-1, keepdims=True)
        e_sum = None
        o = None
        for s, v_ref in zip(parts, (v0_ref, v1_ref, v2_ref, vx_ref)):
            e = jnp.exp2(s) if bounded else jnp.exp2(s - m)
            part_o = _dot(e.astype(BF16), v_ref[:, cols])
            e_sum = e if e_sum is None else e_sum + e
            o = part_o if o is None else o + part_o
        o = o * (1.0 / jnp.sum(e_sum, axis=-1, keepdims=True))
        o_ref[:, cols] = jnp.where(first_half, o[0:nq], o[nq:2 * nq]).astype(BF16)

    parts_next = scores(0)
    for p in range(n_pairs):
        parts = parts_next
        if p + 1 < n_pairs:
            parts_next = scores(p + 1)
        finish(p, parts)


def _na_attn(q, k, v, k_ctx, v_ctx, bias, xshift, bounded=False):
    s, d = q.shape
    chunk = NA_Q_ROWS * GRID_W
    n_steps = s // chunk

    def band(t):
        return pl.BlockSpec((chunk, d), lambda i: (jnp.clip(i - 1, 0, n_steps - 3) + t, 0))

    cur = pl.BlockSpec((chunk, d), lambda i: (i, 0))
    bias_spec = pl.BlockSpec((None,) + bias.shape[1:],
                             lambda i: (jnp.where(i == 0, 0, jnp.where(i == n_steps - 1, 2, 1)), 0, 0, 0),
                             pipeline_mode=pl.Buffered(1))
    return pl.pallas_call(
        functools.partial(_na_attn_kernel, bounded),
        grid=(n_steps,),
        in_specs=[cur, band(0), band(1), band(2), band(0), band(1), band(2), _const_spec(k_ctx.shape),
                  _const_spec(v_ctx.shape), bias_spec, _const_spec(xshift.shape)],
        out_specs=cur,
        out_shape=jax.ShapeDtypeStruct((s, d), BF16),
        compiler_params=_params(1),
        name="na_attn_bounded" if bounded else "na_attn",
    )(q, k, k, k, v, v, v, k_ctx, v_ctx, bias, xshift)


def _na_band_pattern(step, n_rows):
    n_steps = n_rows // NA_Q_ROWS
    band0 = NA_Q_ROWS * int(np.clip(step - 1, 0, n_steps - 3))
    r = step * NA_Q_ROWS + np.arange(NA_Q_ROWS)[:, None]
    key_row = band0 + np.arange(3 * NA_Q_ROWS)[None, :]
    win0 = np.clip(r - NA_ROWS // 2, 0, n_rows - NA_ROWS)
    valid = (key_row >= win0) & (key_row < win0 + NA_ROWS)
    return np.where(valid, key_row - r + (NA_ROWS - 1), 0), valid


def _na_bias_tables(rpb, n_rows, shift):
    n_steps = n_rows // NA_Q_ROWS
    assert n_steps >= 4
    patterns = [_na_band_pattern(t, n_rows) for t in range(n_steps)]
    for dr, valid in patterns[2:-1]:
        assert np.array_equal(dr, patterns[1][0]) and np.array_equal(valid, patterns[1][1])
    cols = np.arange(GRID_W)
    col_start = np.clip(cols - NA_COLS // 2, 0, GRID_W - NA_COLS)
    kc = np.arange(GRID_W)
    col_ok = (kc[None, :] >= col_start[:, None]) & (kc[None, :] < col_start[:, None] + NA_COLS)
    pad = GRID_W - NA_COLS
    padded = jnp.pad(rpb * LOG2_E - shift, ((0, 0), (0, 0), (pad, pad)))
    toeplitz = jnp.stack([padded[:, :, GRID_W - 1 - c:2 * GRID_W - 1 - c] for c in range(GRID_W)], axis=2)
    toeplitz = jnp.where(col_ok[None, None], toeplitz, NEG_BIG)
    n_band = 3 * NA_Q_ROWS
    lead = NA_Q_ROWS
    n_dr = toeplitz.shape[1]
    strip = jnp.transpose(toeplitz, (0, 2, 1, 3)).reshape(rpb.shape[0], GRID_W, n_dr * GRID_W)
    strip = jnp.pad(strip, ((0, 0), (0, 0), (lead * GRID_W, lead * GRID_W)), constant_values=NEG_BIG)
    tables = []
    for dr, valid in (patterns[0], patterns[1], patterns[-1]):
        blocks = []
        for jq in range(NA_Q_ROWS):
            i0 = int(np.argmax(valid[jq]))
            start = int(dr[jq, i0]) - i0 + lead
            assert 0 <= start and start + n_band <= n_dr + 2 * lead
            assert all(dr[jq, i] == start - lead + i for i in range(n_band) if valid[jq, i])
            window = strip[:, :, start * GRID_W:(start + n_band) * GRID_W]
            row_ok = np.repeat(valid[jq], GRID_W)
            blocks.append(jnp.where(row_ok[None, None, :], window, NEG_BIG))
        tables.append(jnp.concatenate(blocks, axis=-2))
    return jnp.stack(tables, axis=0)


def _rope_tables(t):
    pos = jnp.arange(t)
    row = (pos // GRID_W).astype(F32)
    col = (pos % GRID_W).astype(F32)
    n = MLA_ROPE // 4
    freqs = ROPE_BASE ** (-jnp.arange(n, dtype=F32) / n)
    ang = jnp.concatenate([row[:, None] * freqs, col[:, None] * freqs], axis=-1)
    cos, sin = jnp.cos(ang), jnp.sin(ang)
    return jnp.concatenate([cos, cos, sin, sin], axis=-1)


_HALF_SPLIT = np.concatenate([np.arange(0, MLA_ROPE, 2), np.arange(1, MLA_ROPE, 2)])


def _rope_cols(w):
    hs = w[..., _HALF_SPLIT]
    return jnp.concatenate([hs, -hs[..., MLA_ROPE // 2:], hs[..., :MLA_ROPE // 2]], axis=-1)


def _rope_gain(g):
    hs = g[_HALF_SPLIT]
    return jnp.concatenate([hs, hs[MLA_ROPE // 2:], hs[:MLA_ROPE // 2]])


def _mla_weights(w_dq, g_dq, w_uq, w_dkv, g_dkv, w_uk, w_uv, g_qn, g_qr, g_kn, g_kr):
    w_uq_ext = jnp.concatenate([w_uq[..., :MLA_NOPE], _rope_cols(w_uq[..., MLA_NOPE:])], axis=-1)
    g_q = jnp.tile(jnp.concatenate([g_qn, _rope_gain(g_qr)]) * (MLA_SCALE * LOG2_E), MLA_HEADS)
    q_norm = jnp.sqrt(MLA_NOPE * jnp.max(g_qn * g_qn) + MLA_ROPE * jnp.max(g_qr * g_qr)) * (MLA_SCALE * LOG2_E)
    k_norm = jnp.sqrt(MLA_NOPE * jnp.max(g_kn * g_kn) + MLA_ROPE * jnp.max(g_kr * g_kr))
    bound = BOUND_SLACK * q_norm * k_norm
    q_pad = jnp.zeros((1, LANES), F32).at[0, MLA_ROPE].set(-bound)
    return {
        "score_bound": bound,
        "q_pad": q_pad,
        "w_dq": w_dq.astype(BF16),
        "g_dq": g_dq[None, :],
        "w_uq": w_uq_ext.reshape(MLA_Q_LORA, -1).astype(BF16),
        "g_q": g_q[None, :],
        "w_dkv": jnp.concatenate([w_dkv[:, :MLA_KV_LORA], _rope_cols(w_dkv[:, MLA_KV_LORA:])], axis=-1).astype(BF16),
        "g_dkv": g_dkv[None, :],
        "w_uk": w_uk.reshape(MLA_KV_LORA, -1).astype(BF16),
        "g_k": jnp.concatenate([g_kn, _rope_gain(g_kr)])[None, :],
        "w_uvt": w_uv.reshape(MLA_KV_LORA, -1).T.astype(BF16),
    }


def _block_diag(w):
    g, c, _ = w.shape
    out = jnp.zeros((g * c, g * c), w.dtype)
    for i in range(g):
        out = out.at[i * c:(i + 1) * c, i * c:(i + 1) * c].set(w[i])
    return out


def kernel(x, c, ctx, c_ctx, mod_w, mod_b, ffn_w_in, ffn_w_out, mla_w_dq, mla_g_dq, mla_w_uq, mla_w_dkv, mla_g_dkv, mla_w_uk, mla_w_uv, mla_g_qn, mla_g_qr, mla_g_kn, mla_g_kr, mla_w_o, pool_w, pool_scale, na_w_qkv, na_g_q, na_g_k, na_rpb, na_w_o, conv_w_in, conv_w, conv_w_out):
    assert x.shape[0] == 1 and x.shape[2] == D_MODEL and x.shape[1] % ROW_TILE == 0
    s = x.shape[1]
    d = D_MODEL
    xs = x[0]
    hc = ctx[0]
    n_ctx = hc.shape[0]

    cond = jnp.zeros((8, d), F32).at[0].set(c[0]).at[1].set(c_ctx)
    mods = _ada_params(cond, mod_w, mod_b)
    w_in = ffn_w_in.astype(BF16)
    w_out = ffn_w_out.astype(BF16)

    mx = mods[0, 0].reshape(N_MOD, d)
    mc = mods[0, 1].reshape(N_MOD, d)
    xs = _ffn(xs, mx, 0, w_in, w_out, (0, 0))
    hc = _ffn(hc, mc, 0, w_in, w_out, (0, 0))
    mw = _mla_weights(mla_w_dq[0], mla_g_dq[0], mla_w_uq[0], mla_w_dkv[0], mla_g_dkv[0], mla_w_uk[0], mla_w_uv[0],
                      mla_g_qn[0], mla_g_qr[0], mla_g_kn[0], mla_g_kr[0])
    no_rope = jnp.concatenate([jnp.ones((n_ctx, LANES // 2), F32), jnp.zeros((n_ctx, LANES // 2), F32)], axis=-1)
    q_x, k_x, vt_x = _mla_proj(xs, mx, _rope_tables(s), mw)
    q_c, k_c, vt_c = _mla_proj(hc, mc, no_rope, mw)
    o_x = lax.cond(2.0 * mw["score_bound"] <= SCORE_RANGE_LOG2,
                   lambda q, k, vt, ke, vte: _mla_attn(q, k, vt, extra=(ke, vte), bounded=True),
                   lambda q, k, vt, ke, vte: _mla_attn(q, k, vt, extra=(ke, vte)),
                   q_x, k_x, vt_x, k_c, vt_c)
    o_c = _mla_attn(q_c, k_c, vt_c)
    w_o = mla_w_o[0].astype(BF16)
    xs = _ffn(xs, mx, 6, w_in, w_out, (0, 1), pro=(o_x, w_o, mx[5:6]))
    hc = _ffn(hc, mc, 6, w_in, w_out, (0, 1), pro=(o_c, w_o, mc[5:6]))

    mx = mods[1, 0].reshape(N_MOD, d)
    mc = mods[1, 1].reshape(N_MOD, d)
    xs = _ffn(xs, mx, 0, w_in, w_out, (1, 0))
    hc = _ffn(hc, mc, 0, w_in, w_out, (1, 0))
    w_p = _block_diag(pool_w[0]).astype(BF16)
    xs = _ffn(xs, mx, 6, w_in, w_out, (1, 1), pro=(_pool(xs, mx), w_p, mx[5:6] * pool_scale[0][None, :]))
    hc = _ffn(hc, mc, 6, w_in, w_out, (1, 1), pro=(_pool(hc, mc), w_p, mc[5:6] * pool_scale[0][None, :]))

    mx = mods[2, 0].reshape(N_MOD, d)
    mc = mods[2, 1].reshape(N_MOD, d)
    xs = _ffn(xs, mx, 0, w_in, w_out, (2, 0))
    hc = _ffn(hc, mc, 0, w_in, w_out, (2, 0))
    w_qkv = na_w_qkv[0].astype(BF16)
    g_q2 = jnp.tile(na_g_q[0] * (NA_SCALE * LOG2_E), 2)[None, :]
    g_k2 = jnp.tile(na_g_k[0], 2)[None, :]
    q_n, k_n, v_n = _na_proj(xs, mx, w_qkv, g_q2, g_k2)
    _, k_nc, v_nc = _na_proj(hc, mc, w_qkv, g_q2, g_k2)
    qk_bound = BOUND_SLACK * (NA_SCALE * LOG2_E) * NA_HEAD_DIM * jnp.sqrt(
        jnp.max(na_g_q[0] * na_g_q[0]) * jnp.max(na_g_k[0] * na_g_k[0]))
    shift = qk_bound + jnp.maximum(jnp.max(na_rpb[0]) * LOG2_E, 0.0)
    bias = _na_bias_tables(na_rpb[0], s // GRID_W, shift)
    xshift = jnp.full((1, n_ctx), -shift, F32)
    o_n = lax.cond(qk_bound + shift <= SCORE_RANGE_LOG2,
                   lambda *a: _na_attn(*a, bounded=True), lambda *a: _na_attn(*a),
                   q_n, k_n, v_n, k_nc, v_nc, bias, xshift)
    xs = _ffn(xs, mx, 6, w_in, w_out, (2, 1), pro=(o_n, na_w_o[0].astype(BF16), mx[5:6]))

    mx = mods[3, 0].reshape(N_MOD, d)
    xs = _ffn(xs, mx, 0, w_in, w_out, (3, 0))
    w_ci = conv_w_in[0].astype(BF16)
    y_c = _conv(xs, mx, w_ci[:, :d], w_ci[:, d:], conv_w[0])
    xs = _ffn(xs, mx, 6, w_in, w_out, (3, 1), pro=(y_c, conv_w_out[0].astype(BF16), mx[5:6]))
    return xs[None]
```

```python
import functools

import jax
import jax.numpy as jnp
import numpy as np
from jax import lax
from jax.experimental import pallas as pl
from jax.experimental.pallas import tpu as pltpu

F32 = jnp.float32
BF16 = jnp.bfloat16

D_MODEL = 1024
DEPTH = 4
GRID_W = 64
N_MOD = 9
D_FF = 2816
EPS = 1e-6
MLA_HEADS = 8
MLA_Q_LORA = 384
MLA_KV_LORA = 256
MLA_NOPE = 128
MLA_ROPE = 64
MLA_V = 128
MLA_SCALE = (MLA_NOPE + MLA_ROPE) ** -0.5
ROPE_BASE = 10000.0
POOL_WINDOWS = (2, 4, 8, 16)
POOL_GROUP = D_MODEL // len(POOL_WINDOWS)
NA_HEADS = 16
NA_HEAD_DIM = D_MODEL // NA_HEADS
NA_ROWS = 8
NA_COLS = 16
NA_SCALE = NA_HEAD_DIM ** -0.5

LANES = 128
MXU_COLS = 256
VMEM_LIMIT = 56 * 1024 * 1024
ROW_TILE = 512
FFN_ROW_TILE = 1024
MLA_Q_TILE = 2048
FF_CHUNK = MXU_COLS
POOL_HALO = 8
CONV_HALO = 16
NA_Q_ROWS = 4
NEG_BIG = -1e30
LOG2_E = 1.4426950408889634
SCORE_RANGE_LOG2 = 60.0
BOUND_SLACK = 1.01

_NT = (((1,), (1,)), ((), ()))


def _params(n_axes, flags=None):
    return pltpu.CompilerParams(dimension_semantics=("arbitrary",) * n_axes, vmem_limit_bytes=VMEM_LIMIT, flags=flags)


def _const_spec(shape):
    nd = len(shape)
    return pl.BlockSpec(shape, lambda *_: (0,) * nd, pipeline_mode=pl.Buffered(1))


def _rms(x):
    return x * lax.rsqrt(jnp.mean(x * x, axis=-1, keepdims=True) + EPS)


def _modulate(x, mod_ref, row):
    shift = mod_ref[row:row + 1, :]
    scale = mod_ref[row + 1:row + 2, :]
    return _rms(x) * (1.0 + scale) + shift


def _dot(a, b):
    return jnp.dot(a, b, preferred_element_type=F32)


def _ada_kernel(cond_ref, w_ref, b_ref, o_ref):
    cnd = cond_ref[...]
    s = (cnd * (1.0 / (1.0 + jnp.exp(-cnd)))).astype(BF16)
    o_ref[...] = _dot(s, w_ref[...].astype(BF16)) + b_ref[...]


def _ada_params(cond, mod_w, mod_b):
    depth, d, n = mod_w.shape
    tn = n // 8
    return pl.pallas_call(
        _ada_kernel,
        grid=(depth, n // tn),
        in_specs=[
            pl.BlockSpec((8, d), lambda i, j: (0, 0)),
            pl.BlockSpec((None, d, tn), lambda i, j: (i, 0, j)),
            pl.BlockSpec((None, 1, tn), lambda i, j: (i, 0, j)),
        ],
        out_specs=pl.BlockSpec((None, 8, tn), lambda i, j: (i, 0, j)),
        out_shape=jax.ShapeDtypeStruct((depth, 8, n), F32),
        compiler_params=_params(2),
        name="ada_params",
    )(cond, mod_w, mod_b.reshape(depth, 1, n))


def _ffn_kernel(has_pro, n_cast, mod_row, *refs):
    refs = list(refs)
    x_ref = refs.pop(0)
    if has_pro:
        y_ref, wp_ref, gp_ref = refs[:3]
        refs = refs[3:]
    mod_ref, wi_ref, wo_ref = refs[:3]
    cast_src = refs[3:3 + n_cast]
    o_ref = refs[3 + n_cast]
    cast_dst = refs[4 + n_cast:4 + 2 * n_cast]
    a_ref = refs[4 + 2 * n_cast]
    x = x_ref[...]
    if has_pro:
        x = x + gp_ref[...] * _dot(y_ref[...], wp_ref[...])
    h = _modulate(x, mod_ref, mod_row).astype(BF16)
    for j in range(D_FF // FF_CHUNK):
        g = _dot(h, wi_ref[:, j * FF_CHUNK:(j + 1) * FF_CHUNK])
        u = _dot(h, wi_ref[:, D_FF + j * FF_CHUNK:D_FF + (j + 1) * FF_CHUNK])
        a_ref[:, j * FF_CHUNK:(j + 1) * FF_CHUNK] = (g * (1.0 / (1.0 + jnp.exp(-g))) * u).astype(BF16)
    gate = mod_ref[mod_row + 2:mod_row + 3, :]
    o_ref[...] = x + (0.5 * gate) * _dot(a_ref[...], wo_ref[...])
    for src, dst in zip(cast_src, cast_dst):
        dst[...] = src[...].astype(BF16)


def _ffn(x, mod, mod_row, w_in, w_out, pro=None, cast=()):
    t, d = x.shape
    tm = min(FFN_ROW_TILE, t)
    n_steps = t // tm
    row = lambda i: (i, 0)
    in_specs = [pl.BlockSpec((tm, d), row)]
    args = [x]
    if pro is not None:
        y, w_p, g_p = pro
        in_specs += [pl.BlockSpec((tm, d), row), _const_spec(w_p.shape), _const_spec(g_p.shape)]
        args += [y, w_p, g_p]
    in_specs += [_const_spec(mod.shape), _const_spec(w_in.shape), _const_spec(w_out.shape)]
    args += [mod, w_in, w_out]
    out_specs = [pl.BlockSpec((tm, d), row)]
    out_shape = [jax.ShapeDtypeStruct((t, d), F32)]
    for stack, lead in cast:
        rows, cols = stack.shape[-2:]
        assert rows % (16 * n_steps) == 0
        blk = rows // n_steps
        in_specs.append(pl.BlockSpec((None,) * len(lead) + (blk, cols), lambda i, lead=lead: tuple(lead) + (i, 0)))
        args.append(stack)
        out_specs.append(pl.BlockSpec((blk, cols), row))
        out_shape.append(jax.ShapeDtypeStruct((rows, cols), BF16))
    outs = pl.pallas_call(
        functools.partial(_ffn_kernel, pro is not None, len(cast), mod_row),
        grid=(n_steps,),
        in_specs=in_specs,
        out_specs=out_specs,
        out_shape=out_shape,
        scratch_shapes=[pltpu.VMEM((tm, D_FF), BF16)],
        compiler_params=_params(1),
        name="ffn_pro" if pro is not None else "ffn",
    )(*args)
    return outs[0], list(outs[1:])


def _rope_pair(r2, gain, cs, first_half):
    ms = jnp.sum(jnp.where(first_half, r2 * r2, 0.0), axis=-1, keepdims=True) * (1.0 / MLA_ROPE)
    t = r2 * lax.rsqrt(ms + EPS) * gain * cs
    return t + pltpu.roll(t, MLA_ROPE, axis=1)


def _mla_proj_kernel(x_ref, mod_ref, cs_ref, wdq_ref, gdq_ref, wuq_ref, gq_ref, qpad_ref, wdkv_ref, gdkv_ref, wuk_ref,
                     gk_ref, wuvt_ref, q_ref, k_ref, vt_ref):
    h = _modulate(x_ref[...], mod_ref, 3).astype(BF16)
    cs = cs_ref[...]
    lane = lax.broadcasted_iota(jnp.int32, (1, LANES), 1)
    first_half = lane < MLA_ROPE
    cq = (_rms(_dot(h, wdq_ref[...])) * gdq_ref[...]).astype(BF16)
    q = _dot(cq, wuq_ref[...])
    for hd in range(MLA_HEADS):
        c0 = hd * 2 * LANES
        qn = q[:, c0:c0 + LANES]
        q_ref[hd, :, 0:LANES] = (_rms(qn) * gq_ref[:, c0:c0 + LANES]).astype(BF16)
        rot = _rope_pair(q[:, c0 + LANES:c0 + 2 * LANES], gq_ref[:, c0 + LANES:c0 + 2 * LANES], cs, first_half)
        q_ref[hd, :, LANES:2 * LANES] = jnp.where(first_half, rot, qpad_ref[...]).astype(BF16)
    kv = _dot(h, wdkv_ref[...])
    ckv = (_rms(kv[:, :MLA_KV_LORA]) * gdkv_ref[...]).astype(BF16)
    rot = _rope_pair(kv[:, MLA_KV_LORA:], gk_ref[:, LANES:2 * LANES], cs, first_half)
    kr = jnp.where(first_half, rot, jnp.where(lane == MLA_ROPE, 1.0, 0.0)).astype(BF16)
    kn = _dot(ckv, wuk_ref[...])
    vt = lax.dot_general(wuvt_ref[...], ckv, _NT, preferred_element_type=F32)
    for hd in range(MLA_HEADS):
        blk = kn[:, hd * LANES:(hd + 1) * LANES]
        k_ref[hd, :, 0:LANES] = (_rms(blk) * gk_ref[:, 0:LANES]).astype(BF16)
        k_ref[hd, :, LANES:2 * LANES] = kr
        vt_ref[hd] = vt[hd * MLA_V:(hd + 1) * MLA_V, :].astype(BF16)


def _mla_proj(x, mod, cs, w):
    t, d = x.shape
    tm = min(ROW_TILE, t)
    consts = [w["w_dq"], w["g_dq"], w["w_uq"], w["g_q"], w["q_pad"], w["w_dkv"], w["g_dkv"], w["w_uk"], w["g_k"], w["w_uvt"]]
    return pl.pallas_call(
        _mla_proj_kernel,
        grid=(t // tm,),
        in_specs=[pl.BlockSpec((tm, d), lambda i: (i, 0)), _const_spec(mod.shape),
                  pl.BlockSpec((tm, LANES), lambda i: (i, 0))] + [_const_spec(a.shape) for a in consts],
        out_specs=[
            pl.BlockSpec((MLA_HEADS, tm, 2 * LANES), lambda i: (0, i, 0)),
            pl.BlockSpec((MLA_HEADS, tm, 2 * LANES), lambda i: (0, i, 0)),
            pl.BlockSpec((MLA_HEADS, None, MLA_V, tm), lambda i: (0, i, 0, 0)),
        ],
        out_shape=[
            jax.ShapeDtypeStruct((MLA_HEADS, t, 2 * LANES), BF16),
            jax.ShapeDtypeStruct((MLA_HEADS, t, 2 * LANES), BF16),
            jax.ShapeDtypeStruct((MLA_HEADS, t // tm, MLA_V, tm), BF16),
        ],
        compiler_params=_params(1),
        name="mla_proj",
    )(x, mod, cs, *consts)


def _mla_attn_kernel(n_main, sub, tv, has_extra, *refs):
    if has_extra:
        q_ref, k_ref, vt_ref, ke_ref, vte_ref, o_ref, acc_ref, s_ref = refs
    else:
        q_ref, k_ref, vt_ref, o_ref, acc_ref, s_ref = refs
    q = q_ref[...]
    tq = q.shape[0]
    tk = sub * tv
    acc_ref[...] = jnp.zeros_like(acc_ref)

    def scores(j):
        k = k_ref[pl.ds(pl.multiple_of(j * tk, tk), tk), :]
        return lax.dot_general(k, q, _NT, preferred_element_type=F32)

    def main_vts(j):
        return [vt_ref[j * sub + c] for c in range(sub)]

    def produce(slot, j):
        sc = scores(j)
        s_ref[slot] = sc
        return jnp.max(sc, axis=0, keepdims=True)

    def update(carry, s, s_max, vts):
        m, l = carry
        m_new = jnp.maximum(m, s_max)
        alpha = jnp.exp2(m - m_new)
        p = jnp.exp2(s - m_new)
        l = alpha * l + jnp.sum(p, axis=0, keepdims=True)
        pb = p.astype(BF16)
        pv = _dot(vts[0], pb[0:vts[0].shape[1], :])
        for c in range(1, len(vts)):
            pv = pv + _dot(vts[c], pb[c * tv:(c + 1) * tv, :])
        acc_ref[...] = alpha * acc_ref[...] + pv
        return m_new, l

    carry = (jnp.full((1, tq), NEG_BIG, F32), jnp.zeros((1, tq), F32))
    max0 = produce(0, 0)
    if n_main > 1:
        assert n_main % 2 == 0

        def body(jj, state):
            carry, max0 = state
            j = 2 * jj
            max1 = produce(1, j + 1)
            carry = update(carry, s_ref[0], max0, main_vts(j))
            max0 = produce(0, j + 2)
            return update(carry, s_ref[1], max1, main_vts(j + 1)), max0

        carry, max0 = lax.fori_loop(0, n_main // 2 - 1, body, (carry, max0))
        max1 = produce(1, n_main - 1)
        carry = update(carry, s_ref[0], max0, main_vts(n_main - 2))
        carry = update(carry, s_ref[1], max1, main_vts(n_main - 1))
    else:
        carry = update(carry, s_ref[0], max0, main_vts(0))
    if has_extra:
        s_e = lax.dot_general(ke_ref[...], q, _NT, preferred_element_type=F32)
        carry = update(carry, s_e, jnp.max(s_e, axis=0, keepdims=True), [vte_ref[...]])
    o_ref[...] = (acc_ref[...] * (1.0 / carry[1])).T.astype(BF16)


def _mla_attn_bounded_kernel(n_main, sub, tv, has_extra, *refs):
    if has_extra:
        q_ref, k_ref, vt_ref, ke_ref, vte_ref, o_ref, acc_ref, l_ref = refs
    else:
        q_ref, k_ref, vt_ref, o_ref, acc_ref, l_ref = refs
    q = q_ref[...]
    tk = sub * tv
    acc_ref[...] = jnp.zeros_like(acc_ref)
    l_ref[...] = jnp.zeros_like(l_ref)

    def accumulate(k, vts):
        p = jnp.exp2(lax.dot_general(k, q, _NT, preferred_element_type=F32))
        l_ref[...] += jnp.sum(p, axis=0, keepdims=True)
        pb = p.astype(BF16)
        pv = _dot(vts[0], pb[0:vts[0].shape[1], :])
        for c in range(1, len(vts)):
            pv = pv + _dot(vts[c], pb[c * tv:(c + 1) * tv, :])
        acc_ref[...] += pv

    def body(j, _):
        k = k_ref[pl.ds(pl.multiple_of(j * tk, tk), tk), :]
        accumulate(k, [vt_ref[j * sub + c] for c in range(sub)])
        return 0

    lax.fori_loop(0, n_main, body, 0, unroll=2)
    if has_extra:
        accumulate(ke_ref[...], [vte_ref[...]])
    o_ref[...] = (acc_ref[...] * (1.0 / l_ref[...])).T.astype(BF16)


def _mla_attn(q, k, vt, extra=None, bounded=False):
    nh, tq_all, dk = q.shape
    tk_all = k.shape[1]
    n_v, tv = vt.shape[1], vt.shape[3]
    sub = 2 if n_v % 2 == 0 else 1
    tq = min(MLA_Q_TILE, tq_all)
    kv_mode = None if bounded else pl.Buffered(1)
    in_specs = [
        pl.BlockSpec((None, tq, dk), lambda h, i: (h, i, 0)),
        pl.BlockSpec((None, tk_all, dk), lambda h, i: (h, 0, 0), pipeline_mode=kv_mode),
        pl.BlockSpec((None, n_v, MLA_V, tv), lambda h, i: (h, 0, 0, 0), pipeline_mode=kv_mode),
    ]
    if bounded:
        body = functools.partial(_mla_attn_bounded_kernel, n_v // sub, sub, tv, extra is not None)
        scratch = [pltpu.VMEM((MLA_V, tq), F32), pltpu.VMEM((1, tq), F32)]
    else:
        body = functools.partial(_mla_attn_kernel, n_v // sub, sub, tv, extra is not None)
        scratch = [pltpu.VMEM((MLA_V, tq), F32), pltpu.VMEM((2, sub * tv, tq), F32)]
    args = [q, k, vt]
    if extra is not None:
        k_e, vt_e = extra
        in_specs += [pl.BlockSpec((None,) + k_e.shape[1:], lambda h, i: (h, 0, 0)),
                     pl.BlockSpec((None, None) + vt_e.shape[2:], lambda h, i: (h, 0, 0, 0))]
        args += [k_e, vt_e]
    return pl.pallas_call(
        body,
        grid=(nh, tq_all // tq),
        in_specs=in_specs,
        out_specs=pl.BlockSpec((tq, MLA_V), lambda h, i: (i, h)),
        out_shape=jax.ShapeDtypeStruct((tq_all, nh * MLA_V), BF16),
        scratch_shapes=scratch,
        compiler_params=_params(2),
        name="mla_attn_bounded" if bounded else "mla_attn",
    )(*args)


def _pool_kernel(t_total, x_ref, xp_ref, xn_ref, mod_ref, o_ref, ext_ref):
    tm = x_ref.shape[0]
    base = pl.program_id(0) * tm
    hc = _modulate(x_ref[...], mod_ref, 3)
    halo_rows = lax.broadcasted_iota(jnp.int32, (POOL_HALO, 1), 0)
    hp = jnp.where(base - POOL_HALO + halo_rows >= 0, _modulate(xp_ref[...], mod_ref, 3), 0.0)
    hn = jnp.where(base + tm + halo_rows < t_total, _modulate(xn_ref[...], mod_ref, 3), 0.0)
    ext_ref[0:POOL_HALO, :] = hp
    ext_ref[POOL_HALO:POOL_HALO + tm, :] = hc
    ext_ref[POOL_HALO + tm:, :] = hn
    tok = base + lax.broadcasted_iota(jnp.int32, (tm, 1), 0)
    for g, win in enumerate(POOL_WINDOWS):
        half = win // 2
        cols = slice(g * POOL_GROUP, (g + 1) * POOL_GROUP)
        acc = ext_ref[POOL_HALO - half:POOL_HALO - half + tm, cols]
        for j in range(-half + 1, half):
            acc = acc + ext_ref[POOL_HALO + j:POOL_HALO + j + tm, cols]
        cnt = (jnp.minimum(tok + half, t_total) - jnp.maximum(tok - half, 0)).astype(F32)
        o_ref[:, cols] = (acc / cnt - hc[:, cols]).astype(BF16)


def _pool(x, mod):
    t, d = x.shape
    tm = min(ROW_TILE, t)
    per = tm // POOL_HALO
    last = t // POOL_HALO - 1
    return pl.pallas_call(
        functools.partial(_pool_kernel, t),
        grid=(t // tm,),
        in_specs=[
            pl.BlockSpec((tm, d), lambda i: (i, 0)),
            pl.BlockSpec((POOL_HALO, d), lambda i: (jnp.maximum(i * per - 1, 0), 0)),
            pl.BlockSpec((POOL_HALO, d), lambda i: (jnp.minimum((i + 1) * per, last), 0)),
            _const_spec(mod.shape),
        ],
        out_specs=pl.BlockSpec((tm, d), lambda i: (i, 0)),
        out_shape=jax.ShapeDtypeStruct((t, d), BF16),
        scratch_shapes=[pltpu.VMEM((tm + 2 * POOL_HALO, d), F32)],
        compiler_params=_params(1),
        name="pool",
    )(x, x, x, mod)


def _conv_kernel(t_total, x_ref, xp_ref, xn_ref, mod_ref, wb_ref, wcu_ref, wconv_ref, o_ref, ext_ref, cu_ref):
    tm, d = x_ref.shape
    base = pl.program_id(0) * tm
    ext_ref[0:CONV_HALO, :] = _modulate(xp_ref[...], mod_ref, 3).astype(BF16)
    ext_ref[CONV_HALO:CONV_HALO + tm, :] = _modulate(x_ref[...], mod_ref, 3).astype(BF16)
    ext_ref[CONV_HALO + tm:, :] = _modulate(xn_ref[...], mod_ref, 3).astype(BF16)
    ext = ext_ref[...]
    cu = _dot(ext, wcu_ref[:, 0:d]) * _dot(ext, wcu_ref[:, d:2 * d])
    tok = base - CONV_HALO + lax.broadcasted_iota(jnp.int32, (tm + 2 * CONV_HALO, 1), 0)
    cu_ref[...] = jnp.where((tok >= 0) & (tok < t_total), cu, 0.0)
    z = (wconv_ref[0:1, :] * cu_ref[CONV_HALO - 1:CONV_HALO - 1 + tm, :]
         + wconv_ref[1:2, :] * cu_ref[CONV_HALO:CONV_HALO + tm, :]
         + wconv_ref[2:3, :] * cu_ref[CONV_HALO + 1:CONV_HALO + 1 + tm, :])
    b = _dot(ext_ref[CONV_HALO:CONV_HALO + tm, :], wb_ref[...])
    o_ref[...] = (b * z).astype(BF16)


def _conv(x, mod, w_b, w_cu, w_conv):
    t, d = x.shape
    tm = min(ROW_TILE, t)
    per = tm // CONV_HALO
    last = t // CONV_HALO - 1
    return pl.pallas_call(
        functools.partial(_conv_kernel, t),
        grid=(t // tm,),
        in_specs=[
            pl.BlockSpec((tm, d), lambda i: (i, 0)),
            pl.BlockSpec((CONV_HALO, d), lambda i: (jnp.maximum(i * per - 1, 0), 0)),
            pl.BlockSpec((CONV_HALO, d), lambda i: (jnp.minimum((i + 1) * per, last), 0)),
            _const_spec(mod.shape), _const_spec(w_b.shape), _const_spec(w_cu.shape), _const_spec(w_conv.shape),
        ],
        out_specs=pl.BlockSpec((tm, d), lambda i: (i, 0)),
        out_shape=jax.ShapeDtypeStruct((t, d), BF16),
        scratch_shapes=[pltpu.VMEM((tm + 2 * CONV_HALO, d), BF16), pltpu.VMEM((tm + 2 * CONV_HALO, d), F32)],
        compiler_params=_params(1),
        name="conv",
    )(x, x, x, mod, w_b, w_cu, w_conv)


def _head_rms(v, first_half):
    outs = []
    for c in range(v.shape[1] // LANES):
        blk = v[:, c * LANES:(c + 1) * LANES]
        sq = blk * blk
        tot = jnp.sum(sq, axis=-1, keepdims=True)
        lo = jnp.sum(jnp.where(first_half, sq, 0.0), axis=-1, keepdims=True)
        ms = jnp.where(first_half, lo, tot - lo) * (1.0 / NA_HEAD_DIM)
        outs.append(blk * lax.rsqrt(ms + EPS))
    return outs


def _na_proj_kernel(x_ref, mod_ref, w_ref, gq_ref, gk_ref, q_ref, k_ref, v_ref):
    d = x_ref.shape[1]
    h = _modulate(x_ref[...], mod_ref, 3).astype(BF16)
    first_half = lax.broadcasted_iota(jnp.int32, (1, LANES), 1) < NA_HEAD_DIM
    for c, blk in enumerate(_head_rms(_dot(h, w_ref[:, 0:d]), first_half)):
        q_ref[:, c * LANES:(c + 1) * LANES] = (blk * gq_ref[...]).astype(BF16)
    for c, blk in enumerate(_head_rms(_dot(h, w_ref[:, d:2 * d]), first_half)):
        k_ref[:, c * LANES:(c + 1) * LANES] = (blk * gk_ref[...]).astype(BF16)
    v_ref[...] = _dot(h, w_ref[:, 2 * d:3 * d]).astype(BF16)


def _na_proj(x, mod, w_qkv, g_q2, g_k2):
    t, d = x.shape
    tm = min(ROW_TILE, t)
    spec = pl.BlockSpec((tm, d), lambda i: (i, 0))
    return pl.pallas_call(
        _na_proj_kernel,
        grid=(t // tm,),
        in_specs=[spec, _const_spec(mod.shape), _const_spec(w_qkv.shape), _const_spec(g_q2.shape),
                  _const_spec(g_k2.shape)],
        out_specs=[spec, spec, spec],
        out_shape=[jax.ShapeDtypeStruct((t, d), BF16)] * 3,
        compiler_params=_params(1),
        name="na_proj",
    )(x, mod, w_qkv, g_q2, g_k2)


def _na_attn_kernel(bounded, q_ref, k0_ref, k1_ref, k2_ref, v0_ref, v1_ref, v2_ref, kx_ref, vx_ref, bias_ref,
                    xshift_ref, o_ref):
    nq = q_ref.shape[0]
    first_half = lax.broadcasted_iota(jnp.int32, (1, LANES), 1) < NA_HEAD_DIM
    n_pairs = NA_HEADS // 2

    def scores(p):
        cols = slice(p * LANES, (p + 1) * LANES)
        qp = q_ref[:, cols]
        zero = jnp.zeros_like(qp)
        qs = jnp.concatenate([jnp.where(first_half, qp, zero), jnp.where(first_half, zero, qp)], axis=0)
        parts = []
        for t, k_ref in enumerate((k0_ref, k1_ref, k2_ref)):
            lanes = slice(t * nq, (t + 1) * nq)
            bias = jnp.concatenate([bias_ref[2 * p, :, lanes], bias_ref[2 * p + 1, :, lanes]], axis=0)
            parts.append(lax.dot_general(qs, k_ref[:, cols], _NT, preferred_element_type=F32) + bias)
        parts.append(lax.dot_general(qs, kx_ref[:, cols], _NT, preferred_element_type=F32) + xshift_ref[...])
        return parts

    def finish(p, parts):
        cols = slice(p * LANES, (p + 1) * LANES)
        if not bounded:
            m = jnp.max(jnp.maximum(jnp.maximum(parts[0], parts[1]), jnp.maximum(parts[2], parts[3])),
                        axis=-1, keepdims=True)
        e_sum = None
        o = None
        for s, v_ref in zip(parts, (v0_ref, v1_ref, v2_ref, vx_ref)):
            e = jnp.exp2(s) if bounded else jnp.exp2(s - m)
            part_o = _dot(e.astype(BF16), v_ref[:, cols])
            e_sum = e if e_sum is None else e_sum + e
            o = part_o if o is None else o + part_o
        o = o * (1.0 / jnp.sum(e_sum, axis=-1, keepdims=True))
        o_ref[:, cols] = jnp.where(first_half, o[0:nq], o[nq:2 * nq]).astype(BF16)

    parts_next = scores(0)
    for p in range(n_pairs):
        parts = parts_next
        if p + 1 < n_pairs:
            parts_next = scores(p + 1)
        finish(p, parts)


def _na_attn(q, k, v, k_ctx, v_ctx, bias, xshift, bounded=False):
    s, d = q.shape
    chunk = NA_Q_ROWS * GRID_W
    n_steps = s // chunk

    def band(t):
        return pl.BlockSpec((chunk, d), lambda i: (jnp.clip(i - 1, 0, n_steps - 3) + t, 0))

    cur = pl.BlockSpec((chunk, d), lambda i: (i, 0))
    bias_spec = pl.BlockSpec((None,) + bias.shape[1:],
                             lambda i: (jnp.where(i == 0, 0, jnp.where(i == n_steps - 1, 2, 1)), 0, 0, 0),
                             pipeline_mode=pl.Buffered(1))
    return pl.pallas_call(
        functools.partial(_na_attn_kernel, bounded),
        grid=(n_steps,),
        in_specs=[cur, band(0), band(1), band(2), band(0), band(1), band(2), _const_spec(k_ctx.shape),
                  _const_spec(v_ctx.shape), bias_spec, _const_spec(xshift.shape)],
        out_specs=cur,
        out_shape=jax.ShapeDtypeStruct((s, d), BF16),
        compiler_params=_params(1),
        name="na_attn_bounded" if bounded else "na_attn",
    )(q, k, k, k, v, v, v, k_ctx, v_ctx, bias, xshift)


def _na_band_pattern(step, n_rows):
    n_steps = n_rows // NA_Q_ROWS
    band0 = NA_Q_ROWS * int(np.clip(step - 1, 0, n_steps - 3))
    r = step * NA_Q_ROWS + np.arange(NA_Q_ROWS)[:, None]
    key_row = band0 + np.arange(3 * NA_Q_ROWS)[None, :]
    win0 = np.clip(r - NA_ROWS // 2, 0, n_rows - NA_ROWS)
    valid = (key_row >= win0) & (key_row < win0 + NA_ROWS)
    return np.where(valid, key_row - r + (NA_ROWS - 1), 0), valid


def _na_bias_tables(rpb, n_rows, shift):
    n_steps = n_rows // NA_Q_ROWS
    assert n_steps >= 4
    patterns = [_na_band_pattern(t, n_rows) for t in range(n_steps)]
    for dr, valid in patterns[2:-1]:
        assert np.array_equal(dr, patterns[1][0]) and np.array_equal(valid, patterns[1][1])
    cols = np.arange(GRID_W)
    col_start = np.clip(cols - NA_COLS // 2, 0, GRID_W - NA_COLS)
    kc = np.arange(GRID_W)
    col_ok = (kc[None, :] >= col_start[:, None]) & (kc[None, :] < col_start[:, None] + NA_COLS)
    pad = GRID_W - NA_COLS
    padded = jnp.pad(rpb * LOG2_E - shift, ((0, 0), (0, 0), (pad, pad)))
    toeplitz = jnp.stack([padded[:, :, GRID_W - 1 - c:2 * GRID_W - 1 - c] for c in range(GRID_W)], axis=2)
    toeplitz = jnp.where(col_ok[None, None], toeplitz, NEG_BIG)
    n_band = 3 * NA_Q_ROWS
    lead = NA_Q_ROWS
    n_dr = toeplitz.shape[1]
    strip = jnp.transpose(toeplitz, (0, 2, 1, 3)).reshape(rpb.shape[0], GRID_W, n_dr * GRID_W)
    strip = jnp.pad(strip, ((0, 0), (0, 0), (lead * GRID_W, lead * GRID_W)), constant_values=NEG_BIG)
    tables = []
    for dr, valid in (patterns[0], patterns[1], patterns[-1]):
        blocks = []
        for jq in range(NA_Q_ROWS):
            i0 = int(np.argmax(valid[jq]))
            start = int(dr[jq, i0]) - i0 + lead
            assert 0 <= start and start + n_band <= n_dr + 2 * lead
            assert all(dr[jq, i] == start - lead + i for i in range(n_band) if valid[jq, i])
            window = strip[:, :, start * GRID_W:(start + n_band) * GRID_W]
            row_ok = np.repeat(valid[jq], GRID_W)
            blocks.append(jnp.where(row_ok[None, None, :], window, NEG_BIG))
        tables.append(jnp.concatenate(blocks, axis=-2))
    return jnp.stack(tables, axis=0)


def _rope_tables(t):
    pos = jnp.arange(t)
    row = (pos // GRID_W).astype(F32)
    col = (pos % GRID_W).astype(F32)
    n = MLA_ROPE // 4
    freqs = ROPE_BASE ** (-jnp.arange(n, dtype=F32) / n)
    ang = jnp.concatenate([row[:, None] * freqs, col[:, None] * freqs], axis=-1)
    cos, sin = jnp.cos(ang), jnp.sin(ang)
    return jnp.concatenate([cos, cos, sin, sin], axis=-1)


_HALF_SPLIT = np.concatenate([np.arange(0, MLA_ROPE, 2), np.arange(1, MLA_ROPE, 2)])


def _rope_cols(w):
    hs = w[..., _HALF_SPLIT]
    return jnp.concatenate([hs, -hs[..., MLA_ROPE // 2:], hs[..., :MLA_ROPE // 2]], axis=-1)


def _rope_gain(g):
    hs = g[_HALF_SPLIT]
    return jnp.concatenate([hs, hs[MLA_ROPE // 2:], hs[:MLA_ROPE // 2]])


def _mla_weights(w_dq, g_dq, w_uq, w_dkv, g_dkv, w_uk, w_uv, g_qn, g_qr, g_kn, g_kr):
    w_uq_ext = jnp.concatenate([w_uq[..., :MLA_NOPE], _rope_cols(w_uq[..., MLA_NOPE:])], axis=-1)
    g_q = jnp.tile(jnp.concatenate([g_qn, _rope_gain(g_qr)]) * (MLA_SCALE * LOG2_E), MLA_HEADS)
    q_norm = jnp.sqrt(MLA_NOPE * jnp.max(g_qn * g_qn) + MLA_ROPE * jnp.max(g_qr * g_qr)) * (MLA_SCALE * LOG2_E)
    k_norm = jnp.sqrt(MLA_NOPE * jnp.max(g_kn * g_kn) + MLA_ROPE * jnp.max(g_kr * g_kr))
    bound = BOUND_SLACK * q_norm * k_norm
    q_pad = jnp.zeros((1, LANES), F32).at[0, MLA_ROPE].set(-bound)
    return {
        "score_bound": bound,
        "q_pad": q_pad,
        "w_dq": w_dq.astype(BF16),
        "g_dq": g_dq[None, :],
        "w_uq": w_uq_ext.reshape(MLA_Q_LORA, -1).astype(BF16),
        "g_q": g_q[None, :],
        "w_dkv": jnp.concatenate([w_dkv[:, :MLA_KV_LORA], _rope_cols(w_dkv[:, MLA_KV_LORA:])], axis=-1).astype(BF16),
        "g_dkv": g_dkv[None, :],
        "w_uk": w_uk.reshape(MLA_KV_LORA, -1).astype(BF16),
        "g_k": jnp.concatenate([g_kn, _rope_gain(g_kr)])[None, :],
        "w_uvt": w_uv.reshape(MLA_KV_LORA, -1).T.astype(BF16),
    }


def _block_diag(w):
    g, c, _ = w.shape
    out = jnp.zeros((g * c, g * c), w.dtype)
    for i in range(g):
        out = out.at[i * c:(i + 1) * c, i * c:(i + 1) * c].set(w[i])
    return out


def kernel(x, c, ctx, c_ctx, mod_w, mod_b, ffn_w_in, ffn_w_out, mla_w_dq, mla_g_dq, mla_w_uq, mla_w_dkv, mla_g_dkv, mla_w_uk, mla_w_uv, mla_g_qn, mla_g_qr, mla_g_kn, mla_g_kr, mla_w_o, pool_w, pool_scale, na_w_qkv, na_g_q, na_g_k, na_rpb, na_w_o, conv_w_in, conv_w, conv_w_out):
    assert x.shape[0] == 1 and x.shape[2] == D_MODEL and x.shape[1] % ROW_TILE == 0
    s = x.shape[1]
    d = D_MODEL
    xs = x[0]
    hc = ctx[0]
    n_ctx = hc.shape[0]

    cond = jnp.zeros((8, d), F32).at[0].set(c[0]).at[1].set(c_ctx)
    mods = _ada_params(cond, mod_w, mod_b)
    wts = {(0, 0): (ffn_w_in[0, 0].astype(BF16), ffn_w_out[0, 0].astype(BF16))}

    def ffn_x(xs, mx, layer, cast_next):
        cast = [(stack, key) for key in cast_next for stack in (ffn_w_in, ffn_w_out)]
        xs, slabs = _ffn(xs, mx, 0, *wts[(layer, 0)], cast=cast)
        for n, key in enumerate(cast_next):
            wts[key] = (slabs[2 * n], slabs[2 * n + 1])
        return xs

    mx = mods[0, 0].reshape(N_MOD, d)
    mc = mods[0, 1].reshape(N_MOD, d)
    xs = ffn_x(xs, mx, 0, [(0, 1), (1, 0)])
    hc = _ffn(hc, mc, 0, *wts[(0, 0)])[0]
    mw = _mla_weights(mla_w_dq[0], mla_g_dq[0], mla_w_uq[0], mla_w_dkv[0], mla_g_dkv[0], mla_w_uk[0], mla_w_uv[0],
                      mla_g_qn[0], mla_g_qr[0], mla_g_kn[0], mla_g_kr[0])
    no_rope = jnp.concatenate([jnp.ones((n_ctx, LANES // 2), F32), jnp.zeros((n_ctx, LANES // 2), F32)], axis=-1)
    q_x, k_x, vt_x = _mla_proj(xs, mx, _rope_tables(s), mw)
    q_c, k_c, vt_c = _mla_proj(hc, mc, no_rope, mw)
    o_x = lax.cond(2.0 * mw["score_bound"] <= SCORE_RANGE_LOG2,
                   lambda q, k, vt, ke, vte: _mla_attn(q, k, vt, extra=(ke, vte), bounded=True),
                   lambda q, k, vt, ke, vte: _mla_attn(q, k, vt, extra=(ke, vte)),
                   q_x, k_x, vt_x, k_c, vt_c)
    o_c = _mla_attn(q_c, k_c, vt_c)
    w_o = mla_w_o[0].astype(BF16)
    xs = _ffn(xs, mx, 6, *wts[(0, 1)], pro=(o_x, w_o, mx[5:6]))[0]
    hc = _ffn(hc, mc, 6, *wts[(0, 1)], pro=(o_c, w_o, mc[5:6]))[0]

    mx = mods[1, 0].reshape(N_MOD, d)
    mc = mods[1, 1].reshape(N_MOD, d)
    xs = ffn_x(xs, mx, 1, [(1, 1), (2, 0)])
    hc = _ffn(hc, mc, 0, *wts[(1, 0)])[0]
    w_p = _block_diag(pool_w[0]).astype(BF16)
    xs = _ffn(xs, mx, 6, *wts[(1, 1)], pro=(_pool(xs, mx), w_p, mx[5:6] * pool_scale[0][None, :]))[0]
    hc = _ffn(hc, mc, 6, *wts[(1, 1)], pro=(_pool(hc, mc), w_p, mc[5:6] * pool_scale[0][None, :]))[0]

    mx = mods[2, 0].reshape(N_MOD, d)
    mc = mods[2, 1].reshape(N_MOD, d)
    xs = ffn_x(xs, mx, 2, [(2, 1), (3, 0)])
    hc = _ffn(hc, mc, 0, *wts[(2, 0)])[0]
    w_qkv = na_w_qkv[0].astype(BF16)
    g_q2 = jnp.tile(na_g_q[0] * (NA_SCALE * LOG2_E), 2)[None, :]
    g_k2 = jnp.tile(na_g_k[0], 2)[None, :]
    q_n, k_n, v_n = _na_proj(xs, mx, w_qkv, g_q2, g_k2)
    _, k_nc, v_nc = _na_proj(hc, mc, w_qkv, g_q2, g_k2)
    qk_bound = BOUND_SLACK * (NA_SCALE * LOG2_E) * NA_HEAD_DIM * jnp.sqrt(
        jnp.max(na_g_q[0] * na_g_q[0]) * jnp.max(na_g_k[0] * na_g_k[0]))
    shift = qk_bound + jnp.maximum(jnp.max(na_rpb[0]) * LOG2_E, 0.0)
    bias = _na_bias_tables(na_rpb[0], s // GRID_W, shift)
    xshift = jnp.full((1, n_ctx), -shift, F32)
    o_n = lax.cond(qk_bound + shift <= SCORE_RANGE_LOG2,
                   lambda *a: _na_attn(*a, bounded=True), lambda *a: _na_attn(*a),
                   q_n, k_n, v_n, k_nc, v_nc, bias, xshift)
    xs = _ffn(xs, mx, 6, *wts[(2, 1)], pro=(o_n, na_w_o[0].astype(BF16), mx[5:6]))[0]

    mx = mods[3, 0].reshape(N_MOD, d)
    xs = ffn_x(xs, mx, 3, [(3, 1)])
    w_ci = conv_w_in[0].astype(BF16)
    y_c = _conv(xs, mx, w_ci[:, :d], w_ci[:, d:], conv_w[0])
    xs = _ffn(xs, mx, 6, *wts[(3, 1)], pro=(y_c, conv_w_out[0].astype(BF16), mx[5:6]))[0]
    return xs[None]
```

```python
import functools

import jax
import jax.numpy as jnp
import numpy as np
from jax import lax
from jax.experimental import pallas as pl
from jax.experimental.pallas import tpu as pltpu

F32 = jnp.float32
BF16 = jnp.bfloat16

D_MODEL = 1024
DEPTH = 4
GRID_W = 64
N_MOD = 9
D_FF = 2816
EPS = 1e-6
MLA_HEADS = 8
MLA_Q_LORA = 384
MLA_KV_LORA = 256
MLA_NOPE = 128
MLA_ROPE = 64
MLA_V = 128
MLA_SCALE = (MLA_NOPE + MLA_ROPE) ** -0.5
ROPE_BASE = 10000.0
POOL_WINDOWS = (2, 4, 8, 16)
POOL_GROUP = D_MODEL // len(POOL_WINDOWS)
NA_HEADS = 16
NA_HEAD_DIM = D_MODEL // NA_HEADS
NA_ROWS = 8
NA_COLS = 16
NA_SCALE = NA_HEAD_DIM ** -0.5

LANES = 128
MXU_COLS = 256
VMEM_LIMIT = 56 * 1024 * 1024
ROW_TILE = 512
FFN_ROW_TILE = 1024
MLA_Q_TILE = 2048
FF_CHUNK = MXU_COLS
POOL_HALO = 8
CONV_HALO = 16
NA_Q_ROWS = 4
NEG_BIG = -1e30
LOG2_E = 1.4426950408889634
SCORE_RANGE_LOG2 = 60.0
BOUND_SLACK = 1.01

_NT = (((1,), (1,)), ((), ()))


def _params(n_axes, flags=None):
    return pltpu.CompilerParams(dimension_semantics=("arbitrary",) * n_axes, vmem_limit_bytes=VMEM_LIMIT, flags=flags)


def _const_spec(shape):
    nd = len(shape)
    return pl.BlockSpec(shape, lambda *_: (0,) * nd, pipeline_mode=pl.Buffered(1))


def _rms(x):
    return x * lax.rsqrt(jnp.mean(x * x, axis=-1, keepdims=True) + EPS)


def _modulate(x, mod_ref, row):
    shift = mod_ref[row:row + 1, :]
    scale = mod_ref[row + 1:row + 2, :]
    return _rms(x) * (1.0 + scale) + shift


def _dot(a, b):
    return jnp.dot(a, b, preferred_element_type=F32)


def _ada_kernel(cond_ref, w_ref, b_ref, o_ref):
    cnd = cond_ref[...]
    s = (cnd * (1.0 / (1.0 + jnp.exp(-cnd)))).astype(BF16)
    o_ref[...] = _dot(s, w_ref[...].astype(BF16)) + b_ref[...]


def _ada_params(cond, mod_w, mod_b):
    depth, d, n = mod_w.shape
    tn = n // 8
    return pl.pallas_call(
        _ada_kernel,
        grid=(depth, n // tn),
        in_specs=[
            pl.BlockSpec((8, d), lambda i, j: (0, 0)),
            pl.BlockSpec((None, d, tn), lambda i, j: (i, 0, j)),
            pl.BlockSpec((None, 1, tn), lambda i, j: (i, 0, j)),
        ],
        out_specs=pl.BlockSpec((None, 8, tn), lambda i, j: (i, 0, j)),
        out_shape=jax.ShapeDtypeStruct((depth, 8, n), F32),
        compiler_params=_params(2),
        name="ada_params",
    )(cond, mod_w, mod_b.reshape(depth, 1, n))


def _ffn_kernel(has_pro, n_cast, mod_row, *refs):
    refs = list(refs)
    x_ref = refs.pop(0)
    if has_pro:
        y_ref, wp_ref, gp_ref = refs[:3]
        refs = refs[3:]
    mod_ref, wi_ref, wo_ref = refs[:3]
    cast_src = refs[3:3 + n_cast]
    o_ref = refs[3 + n_cast]
    cast_dst = refs[4 + n_cast:4 + 2 * n_cast]
    a_ref = refs[4 + 2 * n_cast]
    x = x_ref[...]
    if has_pro:
        x = x + gp_ref[...] * _dot(y_ref[...], wp_ref[...])
    h = _modulate(x, mod_ref, mod_row).astype(BF16)
    for j in range(D_FF // FF_CHUNK):
        g = _dot(h, wi_ref[:, j * FF_CHUNK:(j + 1) * FF_CHUNK])
        u = _dot(h, wi_ref[:, D_FF + j * FF_CHUNK:D_FF + (j + 1) * FF_CHUNK])
        a_ref[:, j * FF_CHUNK:(j + 1) * FF_CHUNK] = (g * (1.0 / (1.0 + jnp.exp(-g))) * u).astype(BF16)
    gate = mod_ref[mod_row + 2:mod_row + 3, :]
    o_ref[...] = x + (0.5 * gate) * _dot(a_ref[...], wo_ref[...])
    for src, dst in zip(cast_src, cast_dst):
        dst[...] = src[...].astype(BF16)


def _ffn(x, mod, mod_row, w_in, w_out, pro=None, cast=()):
    t, d = x.shape
    tm = min(FFN_ROW_TILE, t)
    n_steps = t // tm
    row = lambda i: (i, 0)
    in_specs = [pl.BlockSpec((tm, d), row)]
    args = [x]
    if pro is not None:
        y, w_p, g_p = pro
        in_specs += [pl.BlockSpec((tm, d), row), _const_spec(w_p.shape), _const_spec(g_p.shape)]
        args += [y, w_p, g_p]
    in_specs += [_const_spec(mod.shape), _const_spec(w_in.shape), _const_spec(w_out.shape)]
    args += [mod, w_in, w_out]
    out_specs = [pl.BlockSpec((tm, d), row)]
    out_shape = [jax.ShapeDtypeStruct((t, d), F32)]
    for stack, lead in cast:
        rows, cols = stack.shape[-2:]
        assert rows % (16 * n_steps) == 0
        blk = rows // n_steps
        in_specs.append(pl.BlockSpec((None,) * len(lead) + (blk, cols), lambda i, lead=lead: tuple(lead) + (i, 0)))
        args.append(stack)
        out_specs.append(pl.BlockSpec((blk, cols), row))
        out_shape.append(jax.ShapeDtypeStruct((rows, cols), BF16))
    outs = pl.pallas_call(
        functools.partial(_ffn_kernel, pro is not None, len(cast), mod_row),
        grid=(n_steps,),
        in_specs=in_specs,
        out_specs=out_specs,
        out_shape=out_shape,
        scratch_shapes=[pltpu.VMEM((tm, D_FF), BF16)],
        compiler_params=_params(1),
        name="ffn_pro" if pro is not None else "ffn",
    )(*args)
    return outs[0], list(outs[1:])


def _rope_pair(r2, gain, cs, first_half):
    ms = jnp.sum(jnp.where(first_half, r2 * r2, 0.0), axis=-1, keepdims=True) * (1.0 / MLA_ROPE)
    t = r2 * lax.rsqrt(ms + EPS) * gain * cs
    return t + pltpu.roll(t, MLA_ROPE, axis=1)


def _mla_proj_kernel(x_ref, mod_ref, cs_ref, wdq_ref, gdq_ref, wuq_ref, gq_ref, qpad_ref, wdkv_ref, gdkv_ref, wuk_ref,
                     gk_ref, wuvt_ref, q_ref, k_ref, vt_ref):
    h = _modulate(x_ref[...], mod_ref, 3).astype(BF16)
    cs = cs_ref[...]
    lane = lax.broadcasted_iota(jnp.int32, (1, LANES), 1)
    first_half = lane < MLA_ROPE
    cq = (_rms(_dot(h, wdq_ref[...])) * gdq_ref[...]).astype(BF16)
    q = _dot(cq, wuq_ref[...])
    for hd in range(MLA_HEADS):
        c0 = hd * 2 * LANES
        qn = q[:, c0:c0 + LANES]
        q_ref[hd, :, 0:LANES] = (_rms(qn) * gq_ref[:, c0:c0 + LANES]).astype(BF16)
        rot = _rope_pair(q[:, c0 + LANES:c0 + 2 * LANES], gq_ref[:, c0 + LANES:c0 + 2 * LANES], cs, first_half)
        q_ref[hd, :, LANES:2 * LANES] = jnp.where(first_half, rot, qpad_ref[...]).astype(BF16)
    kv = _dot(h, wdkv_ref[...])
    ckv = (_rms(kv[:, :MLA_KV_LORA]) * gdkv_ref[...]).astype(BF16)
    rot = _rope_pair(kv[:, MLA_KV_LORA:], gk_ref[:, LANES:2 * LANES], cs, first_half)
    kr = jnp.where(first_half, rot, jnp.where(lane == MLA_ROPE, 1.0, 0.0)).astype(BF16)
    kn = _dot(ckv, wuk_ref[...])
    vt = lax.dot_general(wuvt_ref[...], ckv, _NT, preferred_element_type=F32)
    for hd in range(MLA_HEADS):
        blk = kn[:, hd * LANES:(hd + 1) * LANES]
        k_ref[hd, :, 0:LANES] = (_rms(blk) * gk_ref[:, 0:LANES]).astype(BF16)
        k_ref[hd, :, LANES:2 * LANES] = kr
        vt_ref[hd] = vt[hd * MLA_V:(hd + 1) * MLA_V, :].astype(BF16)


def _mla_proj(x, mod, cs, w):
    t, d = x.shape
    tm = min(ROW_TILE, t)
    consts = [w["w_dq"], w["g_dq"], w["w_uq"], w["g_q"], w["q_pad"], w["w_dkv"], w["g_dkv"], w["w_uk"], w["g_k"], w["w_uvt"]]
    return pl.pallas_call(
        _mla_proj_kernel,
        grid=(t // tm,),
        in_specs=[pl.BlockSpec((tm, d), lambda i: (i, 0)), _const_spec(mod.shape),
                  pl.BlockSpec((tm, LANES), lambda i: (i, 0))] + [_const_spec(a.shape) for a in consts],
        out_specs=[
            pl.BlockSpec((MLA_HEADS, tm, 2 * LANES), lambda i: (0, i, 0)),
            pl.BlockSpec((MLA_HEADS, tm, 2 * LANES), lambda i: (0, i, 0)),
            pl.BlockSpec((MLA_HEADS, None, MLA_V, tm), lambda i: (0, i, 0, 0)),
        ],
        out_shape=[
            jax.ShapeDtypeStruct((MLA_HEADS, t, 2 * LANES), BF16),
            jax.ShapeDtypeStruct((MLA_HEADS, t, 2 * LANES), BF16),
            jax.ShapeDtypeStruct((MLA_HEADS, t // tm, MLA_V, tm), BF16),
        ],
        compiler_params=_params(1),
        name="mla_proj",
    )(x, mod, cs, *consts)


def _mla_attn_kernel(n_main, sub, tv, has_extra, *refs):
    if has_extra:
        q_ref, k_ref, vt_ref, ke_ref, vte_ref, o_ref, acc_ref, s_ref = refs
    else:
        q_ref, k_ref, vt_ref, o_ref, acc_ref, s_ref = refs
    q = q_ref[...]
    tq = q.shape[0]
    tk = sub * tv
    acc_ref[...] = jnp.zeros_like(acc_ref)

    def scores(j):
        k = k_ref[pl.ds(pl.multiple_of(j * tk, tk), tk), :]
        return lax.dot_general(k, q, _NT, preferred_element_type=F32)

    def main_vts(j):
        return [vt_ref[j * sub + c] for c in range(sub)]

    def produce(slot, j):
        sc = scores(j)
        s_ref[slot] = sc
        return jnp.max(sc, axis=0, keepdims=True)

    def update(carry, s, s_max, vts):
        m, l = carry
        m_new = jnp.maximum(m, s_max)
        alpha = jnp.exp2(m - m_new)
        p = jnp.exp2(s - m_new)
        l = alpha * l + jnp.sum(p, axis=0, keepdims=True)
        pb = p.astype(BF16)
        pv = _dot(vts[0], pb[0:vts[0].shape[1], :])
        for c in range(1, len(vts)):
            pv = pv + _dot(vts[c], pb[c * tv:(c + 1) * tv, :])
        acc_ref[...] = alpha * acc_ref[...] + pv
        return m_new, l

    carry = (jnp.full((1, tq), NEG_BIG, F32), jnp.zeros((1, tq), F32))
    max0 = produce(0, 0)
    if n_main > 1:
        assert n_main % 2 == 0

        def body(jj, state):
            carry, max0 = state
            j = 2 * jj
            max1 = produce(1, j + 1)
            carry = update(carry, s_ref[0], max0, main_vts(j))
            max0 = produce(0, j + 2)
            return update(carry, s_ref[1], max1, main_vts(j + 1)), max0

        carry, max0 = lax.fori_loop(0, n_main // 2 - 1, body, (carry, max0))
        max1 = produce(1, n_main - 1)
        carry = update(carry, s_ref[0], max0, main_vts(n_main - 2))
        carry = update(carry, s_ref[1], max1, main_vts(n_main - 1))
    else:
        carry = update(carry, s_ref[0], max0, main_vts(0))
    if has_extra:
        s_e = lax.dot_general(ke_ref[...], q, _NT, preferred_element_type=F32)
        carry = update(carry, s_e, jnp.max(s_e, axis=0, keepdims=True), [vte_ref[...]])
    o_ref[...] = (acc_ref[...] * (1.0 / carry[1])).T.astype(BF16)


def _mla_attn_bounded_kernel(n_main, sub, tv, has_extra, *refs):
    if has_extra:
        q_ref, k_ref, vt_ref, ke_ref, vte_ref, o_ref, acc_ref, l_ref = refs
    else:
        q_ref, k_ref, vt_ref, o_ref, acc_ref, l_ref = refs
    q = q_ref[...]
    tk = sub * tv
    acc_ref[...] = jnp.zeros_like(acc_ref)
    l_ref[...] = jnp.zeros_like(l_ref)

    def accumulate(k, vts):
        p = jnp.exp2(lax.dot_general(k, q, _NT, preferred_element_type=F32))
        l_ref[...] += jnp.sum(p, axis=0, keepdims=True)
        pb = p.astype(BF16)
        pv = _dot(vts[0], pb[0:vts[0].shape[1], :])
        for c in range(1, len(vts)):
            pv = pv + _dot(vts[c], pb[c * tv:(c + 1) * tv, :])
        acc_ref[...] += pv

    def body(j, _):
        k = k_ref[pl.ds(pl.multiple_of(j * tk, tk), tk), :]
        accumulate(k, [vt_ref[j * sub + c] for c in range(sub)])
        return 0

    lax.fori_loop(0, n_main, body, 0, unroll=2)
    if has_extra:
        accumulate(ke_ref[...], [vte_ref[...]])
    o_ref[...] = (acc_ref[...] * (1.0 / l_ref[...])).T.astype(BF16)


def _mla_attn(q, k, vt, extra=None, bounded=False):
    nh, tq_all, dk = q.shape
    tk_all = k.shape[1]
    n_v, tv = vt.shape[1], vt.shape[3]
    sub = 2 if n_v % 2 == 0 else 1
    tq = min(MLA_Q_TILE, tq_all)
    kv_mode = None if bounded else pl.Buffered(1)
    in_specs = [
        pl.BlockSpec((None, tq, dk), lambda h, i: (h, i, 0)),
        pl.BlockSpec((None, tk_all, dk), lambda h, i: (h, 0, 0), pipeline_mode=kv_mode),
        pl.BlockSpec((None, n_v, MLA_V, tv), lambda h, i: (h, 0, 0, 0), pipeline_mode=kv_mode),
    ]
    if bounded:
        body = functools.partial(_mla_attn_bounded_kernel, n_v // sub, sub, tv, extra is not None)
        scratch = [pltpu.VMEM((MLA_V, tq), F32), pltpu.VMEM((1, tq), F32)]
    else:
        body = functools.partial(_mla_attn_kernel, n_v // sub, sub, tv, extra is not None)
        scratch = [pltpu.VMEM((MLA_V, tq), F32), pltpu.VMEM((2, sub * tv, tq), F32)]
    args = [q, k, vt]
    if extra is not None:
        k_e, vt_e = extra
        in_specs += [pl.BlockSpec((None,) + k_e.shape[1:], lambda h, i: (h, 0, 0)),
                     pl.BlockSpec((None, None) + vt_e.shape[2:], lambda h, i: (h, 0, 0, 0))]
        args += [k_e, vt_e]
    return pl.pallas_call(
        body,
        grid=(nh, tq_all // tq),
        in_specs=in_specs,
        out_specs=pl.BlockSpec((tq, MLA_V), lambda h, i: (i, h)),
        out_shape=jax.ShapeDtypeStruct((tq_all, nh * MLA_V), BF16),
        scratch_shapes=scratch,
        compiler_params=_params(2),
        name="mla_attn_bounded" if bounded else "mla_attn",
    )(*args)


def _pool_kernel(t_total, x_ref, xp_ref, xn_ref, mod_ref, o_ref, ext_ref, lvl_a_ref, lvl_b_ref):
    tm = x_ref.shape[0]
    base = pl.program_id(0) * tm
    hc = _modulate(x_ref[...], mod_ref, 3)
    halo_rows = lax.broadcasted_iota(jnp.int32, (POOL_HALO, 1), 0)
    hp = jnp.where(base - POOL_HALO + halo_rows >= 0, _modulate(xp_ref[...], mod_ref, 3), 0.0)
    hn = jnp.where(base + tm + halo_rows < t_total, _modulate(xn_ref[...], mod_ref, 3), 0.0)
    ext_ref[0:POOL_HALO, :] = hp
    ext_ref[POOL_HALO:POOL_HALO + tm, :] = hc
    ext_ref[POOL_HALO + tm:, :] = hn

    tok = base + lax.broadcasted_iota(jnp.int32, (tm, 1), 0)
    for g, win in enumerate(POOL_WINDOWS):
        half = win // 2
        cols = slice(g * POOL_GROUP, (g + 1) * POOL_GROUP)
        cur, length, step, nxt = ext_ref, tm + 2 * POOL_HALO, 1, 0
        while step < half:
            length -= step
            dst = (lvl_a_ref, lvl_b_ref)[nxt]
            dst[0:length, cols] = cur[0:length, cols] + cur[step:step + length, cols]
            cur, step, nxt = dst, 2 * step, 1 - nxt
        win_sum = cur[POOL_HALO - half:POOL_HALO - half + tm, cols] + cur[POOL_HALO:POOL_HALO + tm, cols]
        cnt = (jnp.minimum(tok + half, t_total) - jnp.maximum(tok - half, 0)).astype(F32)
        o_ref[:, cols] = (win_sum / cnt - ext_ref[POOL_HALO:POOL_HALO + tm, cols]).astype(BF16)


def _pool(x, mod):
    t, d = x.shape
    tm = min(ROW_TILE, t)
    per = tm // POOL_HALO
    last = t // POOL_HALO - 1
    return pl.pallas_call(
        functools.partial(_pool_kernel, t),
        grid=(t // tm,),
        in_specs=[
            pl.BlockSpec((tm, d), lambda i: (i, 0)),
            pl.BlockSpec((POOL_HALO, d), lambda i: (jnp.maximum(i * per - 1, 0), 0)),
            pl.BlockSpec((POOL_HALO, d), lambda i: (jnp.minimum((i + 1) * per, last), 0)),
            _const_spec(mod.shape),
        ],
        out_specs=pl.BlockSpec((tm, d), lambda i: (i, 0)),
        out_shape=jax.ShapeDtypeStruct((t, d), BF16),
        scratch_shapes=[pltpu.VMEM((tm + 2 * POOL_HALO, d), F32)] * 3,
        compiler_params=_params(1),
        name="pool",
    )(x, x, x, mod)


def _conv_kernel(t_total, x_ref, xp_ref, xn_ref, mod_ref, wb_ref, wcu_ref, wconv_ref, o_ref, ext_ref, cu_ref):
    tm, d = x_ref.shape
    base = pl.program_id(0) * tm
    ext_ref[0:CONV_HALO, :] = _modulate(xp_ref[...], mod_ref, 3).astype(BF16)
    ext_ref[CONV_HALO:CONV_HALO + tm, :] = _modulate(x_ref[...], mod_ref, 3).astype(BF16)
    ext_ref[CONV_HALO + tm:, :] = _modulate(xn_ref[...], mod_ref, 3).astype(BF16)
    ext = ext_ref[...]
    cu = _dot(ext, wcu_ref[:, 0:d]) * _dot(ext, wcu_ref[:, d:2 * d])
    tok = base - CONV_HALO + lax.broadcasted_iota(jnp.int32, (tm + 2 * CONV_HALO, 1), 0)
    cu_ref[...] = jnp.where((tok >= 0) & (tok < t_total), cu, 0.0)
    z = (wconv_ref[0:1, :] * cu_ref[CONV_HALO - 1:CONV_HALO - 1 + tm, :]
         + wconv_ref[1:2, :] * cu_ref[CONV_HALO:CONV_HALO + tm, :]
         + wconv_ref[2:3, :] * cu_ref[CONV_HALO + 1:CONV_HALO + 1 + tm, :])
    b = _dot(ext_ref[CONV_HALO:CONV_HALO + tm, :], wb_ref[...])
    o_ref[...] = (b * z).astype(BF16)


def _conv(x, mod, w_b, w_cu, w_conv):
    t, d = x.shape
    tm = min(ROW_TILE, t)
    per = tm // CONV_HALO
    last = t // CONV_HALO - 1
    return pl.pallas_call(
        functools.partial(_conv_kernel, t),
        grid=(t // tm,),
        in_specs=[
            pl.BlockSpec((tm, d), lambda i: (i, 0)),
            pl.BlockSpec((CONV_HALO, d), lambda i: (jnp.maximum(i * per - 1, 0), 0)),
            pl.BlockSpec((CONV_HALO, d), lambda i: (jnp.minimum((i + 1) * per, last), 0)),
            _const_spec(mod.shape), _const_spec(w_b.shape), _const_spec(w_cu.shape), _const_spec(w_conv.shape),
        ],
        out_specs=pl.BlockSpec((tm, d), lambda i: (i, 0)),
        out_shape=jax.ShapeDtypeStruct((t, d), BF16),
        scratch_shapes=[pltpu.VMEM((tm + 2 * CONV_HALO, d), BF16), pltpu.VMEM((tm + 2 * CONV_HALO, d), F32)],
        compiler_params=_params(1),
        name="conv",
    )(x, x, x, mod, w_b, w_cu, w_conv)


def _head_rms(v, first_half):
    outs = []
    for c in range(v.shape[1] // LANES):
        blk = v[:, c * LANES:(c + 1) * LANES]
        sq = blk * blk
        tot = jnp.sum(sq, axis=-1, keepdims=True)
        lo = jnp.sum(jnp.where(first_half, sq, 0.0), axis=-1, keepdims=True)
        ms = jnp.where(first_half, lo, tot - lo) * (1.0 / NA_HEAD_DIM)
        outs.append(blk * lax.rsqrt(ms + EPS))
    return outs


def _na_proj_kernel(x_ref, mod_ref, w_ref, gq_ref, gk_ref, q_ref, k_ref, v_ref):
    d = x_ref.shape[1]
    h = _modulate(x_ref[...], mod_ref, 3).astype(BF16)
    first_half = lax.broadcasted_iota(jnp.int32, (1, LANES), 1) < NA_HEAD_DIM
    for c, blk in enumerate(_head_rms(_dot(h, w_ref[:, 0:d]), first_half)):
        q_ref[:, c * LANES:(c + 1) * LANES] = (blk * gq_ref[...]).astype(BF16)
    for c, blk in enumerate(_head_rms(_dot(h, w_ref[:, d:2 * d]), first_half)):
        k_ref[:, c * LANES:(c + 1) * LANES] = (blk * gk_ref[...]).astype(BF16)
    v_ref[...] = _dot(h, w_ref[:, 2 * d:3 * d]).astype(BF16)


def _na_proj(x, mod, w_qkv, g_q2, g_k2):
    t, d = x.shape
    tm = min(ROW_TILE, t)
    spec = pl.BlockSpec((tm, d), lambda i: (i, 0))
    return pl.pallas_call(
        _na_proj_kernel,
        grid=(t // tm,),
        in_specs=[spec, _const_spec(mod.shape), _const_spec(w_qkv.shape), _const_spec(g_q2.shape),
                  _const_spec(g_k2.shape)],
        out_specs=[spec, spec, spec],
        out_shape=[jax.ShapeDtypeStruct((t, d), BF16)] * 3,
        compiler_params=_params(1),
        name="na_proj",
    )(x, mod, w_qkv, g_q2, g_k2)


def _na_attn_kernel(bounded, q_ref, k0_ref, k1_ref, k2_ref, v0_ref, v1_ref, v2_ref, kx_ref, vx_ref, bias_ref,
                    xshift_ref, o_ref):
    nq = q_ref.shape[0]
    first_half = lax.broadcasted_iota(jnp.int32, (1, LANES), 1) < NA_HEAD_DIM
    n_pairs = NA_HEADS // 2

    def scores(p):
        cols = slice(p * LANES, (p + 1) * LANES)
        qp = q_ref[:, cols]
        zero = jnp.zeros_like(qp)
        qs = jnp.concatenate([jnp.where(first_half, qp, zero), jnp.where(first_half, zero, qp)], axis=0)
        parts = []
        for t, k_ref in enumerate((k0_ref, k1_ref, k2_ref)):
            lanes = slice(t * nq, (t + 1) * nq)
            bias = jnp.concatenate([bias_ref[2 * p, :, lanes], bias_ref[2 * p + 1, :, lanes]], axis=0)
            parts.append(lax.dot_general(qs, k_ref[:, cols], _NT, preferred_element_type=F32) + bias)
        parts.append(lax.dot_general(qs, kx_ref[:, cols], _NT, preferred_element_type=F32) + xshift_ref[...])
        return parts

    def finish(p, parts):
        cols = slice(p * LANES, (p + 1) * LANES)
        if not bounded:
            m = jnp.max(jnp.maximum(jnp.maximum(parts[0], parts[1]), jnp.maximum(parts[2], parts[3])),
                        axis=-1, keepdims=True)
        e_sum = None
        o = None
        for s, v_ref in zip(parts, (v0_ref, v1_ref, v2_ref, vx_ref)):
            e = jnp.exp2(s) if bounded else jnp.exp2(s - m)
            part_o = _dot(e.astype(BF16), v_ref[:, cols])
            e_sum = e if e_sum is None else e_sum + e
            o = part_o if o is None else o + part_o
        o = o * (1.0 / jnp.sum(e_sum, axis=-1, keepdims=True))
        o_ref[:, cols] = jnp.where(first_half, o[0:nq], o[nq:2 * nq]).astype(BF16)

    parts_next = scores(0)
    for p in range(n_pairs):
        parts = parts_next
        if p + 1 < n_pairs:
            parts_next = scores(p + 1)
        finish(p, parts)


def _na_attn(q, k, v, k_ctx, v_ctx, bias, xshift, bounded=False):
    s, d = q.shape
    chunk = NA_Q_ROWS * GRID_W
    n_steps = s // chunk

    def band(t):
        return pl.BlockSpec((chunk, d), lambda i: (jnp.clip(i - 1, 0, n_steps - 3) + t, 0))

    cur = pl.BlockSpec((chunk, d), lambda i: (i, 0))
    bias_spec = pl.BlockSpec((None,) + bias.shape[1:],
                             lambda i: (jnp.where(i == 0, 0, jnp.where(i == n_steps - 1, 2, 1)), 0, 0, 0),
                             pipeline_mode=pl.Buffered(1))
    return pl.pallas_call(
        functools.partial(_na_attn_kernel, bounded),
        grid=(n_steps,),
        in_specs=[cur, band(0), band(1), band(2), band(0), band(1), band(2), _const_spec(k_ctx.shape),
                  _const_spec(v_ctx.shape), bias_spec, _const_spec(xshift.shape)],
        out_specs=cur,
        out_shape=jax.ShapeDtypeStruct((s, d), BF16),
        compiler_params=_params(1),
        name="na_attn_bounded" if bounded else "na_attn",
    )(q, k, k, k, v, v, v, k_ctx, v_ctx, bias, xshift)


def _na_band_pattern(step, n_rows):
    n_steps = n_rows // NA_Q_ROWS
    band0 = NA_Q_ROWS * int(np.clip(step - 1, 0, n_steps - 3))
    r = step * NA_Q_ROWS + np.arange(NA_Q_ROWS)[:, None]
    key_row = band0 + np.arange(3 * NA_Q_ROWS)[None, :]
    win0 = np.clip(r - NA_ROWS // 2, 0, n_rows - NA_ROWS)
    valid = (key_row >= win0) & (key_row < win0 + NA_ROWS)
    return np.where(valid, key_row - r + (NA_ROWS - 1), 0), valid


def _na_bias_tables(rpb, n_rows, shift):
    n_steps = n_rows // NA_Q_ROWS
    assert n_steps >= 4
    patterns = [_na_band_pattern(t, n_rows) for t in range(n_steps)]
    for dr, valid in patterns[2:-1]:
        assert np.array_equal(dr, patterns[1][0]) and np.array_equal(valid, patterns[1][1])
    cols = np.arange(GRID_W)
    col_start = np.clip(cols - NA_COLS // 2, 0, GRID_W - NA_COLS)
    kc = np.arange(GRID_W)
    col_ok = (kc[None, :] >= col_start[:, None]) & (kc[None, :] < col_start[:, None] + NA_COLS)
    pad = GRID_W - NA_COLS
    padded = jnp.pad(rpb * LOG2_E - shift, ((0, 0), (0, 0), (pad, pad)))
    toeplitz = jnp.stack([padded[:, :, GRID_W - 1 - c:2 * GRID_W - 1 - c] for c in range(GRID_W)], axis=2)
    toeplitz = jnp.where(col_ok[None, None], toeplitz, NEG_BIG)
    n_band = 3 * NA_Q_ROWS
    lead = NA_Q_ROWS
    n_dr = toeplitz.shape[1]
    strip = jnp.transpose(toeplitz, (0, 2, 1, 3)).reshape(rpb.shape[0], GRID_W, n_dr * GRID_W)
    strip = jnp.pad(strip, ((0, 0), (0, 0), (lead * GRID_W, lead * GRID_W)), constant_values=NEG_BIG)
    plan = []
    for dr, valid in (patterns[0], patterns[1], patterns[-1]):
        for jq in range(NA_Q_ROWS):
            rows_ok = np.nonzero(valid[jq])[0]
            lo, hi = int(rows_ok[0]), int(rows_ok[-1]) + 1
            assert hi - lo == len(rows_ok)
            start = int(dr[jq, lo]) - lo + lead
            assert 0 <= start and start + n_band <= n_dr + 2 * lead
            assert all(dr[jq, i] == start - lead + i for i in range(lo, hi))
            plan.append((start, lo, hi))

    def assemble(strip_ref, out_ref):
        lanes = lax.broadcasted_iota(jnp.int32, (1, n_band * GRID_W), 1)
        for n, (start, lo, hi) in enumerate(plan):
            kind, jq = divmod(n, NA_Q_ROWS)
            window = strip_ref[:, start * GRID_W:(start + n_band) * GRID_W]
            row_ok = (lanes >= lo * GRID_W) & (lanes < hi * GRID_W)
            out_ref[kind, jq * GRID_W:(jq + 1) * GRID_W, :] = jnp.where(row_ok, window, NEG_BIG)

    n_heads = rpb.shape[0]
    return pl.pallas_call(
        assemble,
        grid=(n_heads,),
        in_specs=[pl.BlockSpec((None,) + strip.shape[1:], lambda h: (h, 0, 0))],
        out_specs=pl.BlockSpec((3, None, NA_Q_ROWS * GRID_W, n_band * GRID_W), lambda h: (0, h, 0, 0)),
        out_shape=jax.ShapeDtypeStruct((3, n_heads, NA_Q_ROWS * GRID_W, n_band * GRID_W), F32),
        compiler_params=_params(1),
        name="na_bias_tables",
    )(strip)


def _rope_tables(t):
    pos = jnp.arange(t)
    row = (pos // GRID_W).astype(F32)
    col = (pos % GRID_W).astype(F32)
    n = MLA_ROPE // 4
    freqs = ROPE_BASE ** (-jnp.arange(n, dtype=F32) / n)
    ang = jnp.concatenate([row[:, None] * freqs, col[:, None] * freqs], axis=-1)
    cos, sin = jnp.cos(ang), jnp.sin(ang)
    return jnp.concatenate([cos, cos, sin, sin], axis=-1)


_HALF_SPLIT = np.concatenate([np.arange(0, MLA_ROPE, 2), np.arange(1, MLA_ROPE, 2)])


def _rope_cols(w):
    hs = w[..., _HALF_SPLIT]
    return jnp.concatenate([hs, -hs[..., MLA_ROPE // 2:], hs[..., :MLA_ROPE // 2]], axis=-1)


def _rope_gain(g):
    hs = g[_HALF_SPLIT]
    return jnp.concatenate([hs, hs[MLA_ROPE // 2:], hs[:MLA_ROPE // 2]])


def _mla_weights(w_dq, g_dq, w_uq, w_dkv, g_dkv, w_uk, w_uv, g_qn, g_qr, g_kn, g_kr):
    w_uq_ext = jnp.concatenate([w_uq[..., :MLA_NOPE], _rope_cols(w_uq[..., MLA_NOPE:])], axis=-1)
    g_q = jnp.tile(jnp.concatenate([g_qn, _rope_gain(g_qr)]) * (MLA_SCALE * LOG2_E), MLA_HEADS)
    q_norm = jnp.sqrt(MLA_NOPE * jnp.max(g_qn * g_qn) + MLA_ROPE * jnp.max(g_qr * g_qr)) * (MLA_SCALE * LOG2_E)
    k_norm = jnp.sqrt(MLA_NOPE * jnp.max(g_kn * g_kn) + MLA_ROPE * jnp.max(g_kr * g_kr))
    bound = BOUND_SLACK * q_norm * k_norm
    q_pad = jnp.zeros((1, LANES), F32).at[0, MLA_ROPE].set(-bound)
    return {
        "score_bound": bound,
        "q_pad": q_pad,
        "w_dq": w_dq.astype(BF16),
        "g_dq": g_dq[None, :],
        "w_uq": w_uq_ext.reshape(MLA_Q_LORA, -1).astype(BF16),
        "g_q": g_q[None, :],
        "w_dkv": jnp.concatenate([w_dkv[:, :MLA_KV_LORA], _rope_cols(w_dkv[:, MLA_KV_LORA:])], axis=-1).astype(BF16),
        "g_dkv": g_dkv[None, :],
        "w_uk": w_uk.reshape(MLA_KV_LORA, -1).astype(BF16),
        "g_k": jnp.concatenate([g_kn, _rope_gain(g_kr)])[None, :],
        "w_uvt": w_uv.reshape(MLA_KV_LORA, -1).T.astype(BF16),
    }


def _block_diag(w):
    g, c, _ = w.shape
    out = jnp.zeros((g * c, g * c), w.dtype)
    for i in range(g):
        out = out.at[i * c:(i + 1) * c, i * c:(i + 1) * c].set(w[i])
    return out


def kernel(x, c, ctx, c_ctx, mod_w, mod_b, ffn_w_in, ffn_w_out, mla_w_dq, mla_g_dq, mla_w_uq, mla_w_dkv, mla_g_dkv, mla_w_uk, mla_w_uv, mla_g_qn, mla_g_qr, mla_g_kn, mla_g_kr, mla_w_o, pool_w, pool_scale, na_w_qkv, na_g_q, na_g_k, na_rpb, na_w_o, conv_w_in, conv_w, conv_w_out):
    assert x.shape[0] == 1 and x.shape[2] == D_MODEL and x.shape[1] % ROW_TILE == 0
    s = x.shape[1]
    d = D_MODEL
    xs = x[0]
    hc = ctx[0]
    n_ctx = hc.shape[0]

    cond = jnp.zeros((8, d), F32).at[0].set(c[0]).at[1].set(c_ctx)
    mods = _ada_params(cond, mod_w, mod_b)
    wts = {(0, 0): (ffn_w_in[0, 0].astype(BF16), ffn_w_out[0, 0].astype(BF16))}

    def ffn_x(xs, mx, layer, cast_next):
        cast = [(stack, key) for key in cast_next for stack in (ffn_w_in, ffn_w_out)]
        xs, slabs = _ffn(xs, mx, 0, *wts[(layer, 0)], cast=cast)
        for n, key in enumerate(cast_next):
            wts[key] = (slabs[2 * n], slabs[2 * n + 1])
        return xs

    mx = mods[0, 0].reshape(N_MOD, d)
    mc = mods[0, 1].reshape(N_MOD, d)
    xs = ffn_x(xs, mx, 0, [(0, 1), (1, 0)])
    hc = _ffn(hc, mc, 0, *wts[(0, 0)])[0]
    mw = _mla_weights(mla_w_dq[0], mla_g_dq[0], mla_w_uq[0], mla_w_dkv[0], mla_g_dkv[0], mla_w_uk[0], mla_w_uv[0],
                      mla_g_qn[0], mla_g_qr[0], mla_g_kn[0], mla_g_kr[0])
    no_rope = jnp.concatenate([jnp.ones((n_ctx, LANES // 2), F32), jnp.zeros((n_ctx, LANES // 2), F32)], axis=-1)
    q_x, k_x, vt_x = _mla_proj(xs, mx, _rope_tables(s), mw)
    q_c, k_c, vt_c = _mla_proj(hc, mc, no_rope, mw)
    o_x = lax.cond(2.0 * mw["score_bound"] <= SCORE_RANGE_LOG2,
                   lambda q, k, vt, ke, vte: _mla_attn(q, k, vt, extra=(ke, vte), bounded=True),
                   lambda q, k, vt, ke, vte: _mla_attn(q, k, vt, extra=(ke, vte)),
                   q_x, k_x, vt_x, k_c, vt_c)
    o_c = _mla_attn(q_c, k_c, vt_c)
    w_o = mla_w_o[0].astype(BF16)
    xs = _ffn(xs, mx, 6, *wts[(0, 1)], pro=(o_x, w_o, mx[5:6]))[0]
    hc = _ffn(hc, mc, 6, *wts[(0, 1)], pro=(o_c, w_o, mc[5:6]))[0]

    mx = mods[1, 0].reshape(N_MOD, d)
    mc = mods[1, 1].reshape(N_MOD, d)
    xs = ffn_x(xs, mx, 1, [(1, 1), (2, 0)])
    hc = _ffn(hc, mc, 0, *wts[(1, 0)])[0]
    w_p = _block_diag(pool_w[0]).astype(BF16)
    xs = _ffn(xs, mx, 6, *wts[(1, 1)], pro=(_pool(xs, mx), w_p, mx[5:6] * pool_scale[0][None, :]))[0]
    hc = _ffn(hc, mc, 6, *wts[(1, 1)], pro=(_pool(hc, mc), w_p, mc[5:6] * pool_scale[0][None, :]))[0]

    mx = mods[2, 0].reshape(N_MOD, d)
    mc = mods[2, 1].reshape(N_MOD, d)
    xs = ffn_x(xs, mx, 2, [(2, 1), (3, 0)])
    hc = _ffn(hc, mc, 0, *wts[(2, 0)])[0]
    w_qkv = na_w_qkv[0].astype(BF16)
    g_q2 = jnp.tile(na_g_q[0] * (NA_SCALE * LOG2_E), 2)[None, :]
    g_k2 = jnp.tile(na_g_k[0], 2)[None, :]
    q_n, k_n, v_n = _na_proj(xs, mx, w_qkv, g_q2, g_k2)
    _, k_nc, v_nc = _na_proj(hc, mc, w_qkv, g_q2, g_k2)
    qk_bound = BOUND_SLACK * (NA_SCALE * LOG2_E) * NA_HEAD_DIM * jnp.sqrt(
        jnp.max(na_g_q[0] * na_g_q[0]) * jnp.max(na_g_k[0] * na_g_k[0]))
    shift = qk_bound + jnp.maximum(jnp.max(na_rpb[0]) * LOG2_E, 0.0)
    bias = _na_bias_tables(na_rpb[0], s // GRID_W, shift)
    xshift = jnp.full((1, n_ctx), -shift, F32)
    o_n = lax.cond(qk_bound + shift <= SCORE_RANGE_LOG2,
                   lambda *a: _na_attn(*a, bounded=True), lambda *a: _na_attn(*a),
                   q_n, k_n, v_n, k_nc, v_nc, bias, xshift)
    xs = _ffn(xs, mx, 6, *wts[(2, 1)], pro=(o_n, na_w_o[0].astype(BF16), mx[5:6]))[0]

    mx = mods[3, 0].reshape(N_MOD, d)
    xs = ffn_x(xs, mx, 3, [(3, 1)])
    w_ci = conv_w_in[0].astype(BF16)
    y_c = _conv(xs, mx, w_ci[:, :d], w_ci[:, d:], conv_w[0])
    xs = _ffn(xs, mx, 6, *wts[(3, 1)], pro=(y_c, conv_w_out[0].astype(BF16), mx[5:6]))[0]
    return xs[None]
```

```python
import functools

import jax
import jax.numpy as jnp
import numpy as np
from jax import lax
from jax.experimental import pallas as pl
from jax.experimental.pallas import tpu as pltpu

F32 = jnp.float32
BF16 = jnp.bfloat16

D_MODEL = 1024
DEPTH = 4
GRID_W = 64
N_MOD = 9
D_FF = 2816
EPS = 1e-6
MLA_HEADS = 8
MLA_Q_LORA = 384
MLA_KV_LORA = 256
MLA_NOPE = 128
MLA_ROPE = 64
MLA_V = 128
MLA_SCALE = (MLA_NOPE + MLA_ROPE) ** -0.5
ROPE_BASE = 10000.0
POOL_WINDOWS = (2, 4, 8, 16)
POOL_GROUP = D_MODEL // len(POOL_WINDOWS)
NA_HEADS = 16
NA_HEAD_DIM = D_MODEL // NA_HEADS
NA_ROWS = 8
NA_COLS = 16
NA_SCALE = NA_HEAD_DIM ** -0.5

LANES = 128
MXU_COLS = 256
VMEM_LIMIT = 56 * 1024 * 1024
ROW_TILE = 512
FFN_ROW_TILE = 1024
MLA_Q_TILE = 2048
FF_CHUNK = MXU_COLS
POOL_HALO = 8
CONV_HALO = 16
NA_Q_ROWS = 4
NEG_BIG = -1e30
LOG2_E = 1.4426950408889634
SCORE_RANGE_LOG2 = 60.0
BOUND_SLACK = 1.01

_NT = (((1,), (1,)), ((), ()))


def _params(n_axes, flags=None):
    return pltpu.CompilerParams(dimension_semantics=("arbitrary",) * n_axes, vmem_limit_bytes=VMEM_LIMIT, flags=flags)


def _const_spec(shape):
    nd = len(shape)
    return pl.BlockSpec(shape, lambda *_: (0,) * nd, pipeline_mode=pl.Buffered(1))


def _rms(x):
    return x * lax.rsqrt(jnp.mean(x * x, axis=-1, keepdims=True) + EPS)


def _modulate(x, mod_ref, row):
    shift = mod_ref[row:row + 1, :]
    scale = mod_ref[row + 1:row + 2, :]
    return _rms(x) * (1.0 + scale) + shift


def _dot(a, b):
    return jnp.dot(a, b, preferred_element_type=F32)


def _ada_kernel(cond_ref, w_ref, b_ref, o_ref):
    cnd = cond_ref[...]
    s = (cnd * (1.0 / (1.0 + jnp.exp(-cnd)))).astype(BF16)
    o_ref[...] = _dot(s, w_ref[...].astype(BF16)) + b_ref[...]


def _ada_params(cond, mod_w, mod_b):
    depth, d, n = mod_w.shape
    tn = n // 8
    return pl.pallas_call(
        _ada_kernel,
        grid=(depth, n // tn),
        in_specs=[
            pl.BlockSpec((8, d), lambda i, j: (0, 0)),
            pl.BlockSpec((None, d, tn), lambda i, j: (i, 0, j)),
            pl.BlockSpec((None, 1, tn), lambda i, j: (i, 0, j)),
        ],
        out_specs=pl.BlockSpec((None, 8, tn), lambda i, j: (i, 0, j)),
        out_shape=jax.ShapeDtypeStruct((depth, 8, n), F32),
        compiler_params=_params(2),
        name="ada_params",
    )(cond, mod_w, mod_b.reshape(depth, 1, n))


def _ffn_kernel(has_pro, n_cast, mod_row, *refs):
    refs = list(refs)
    x_ref = refs.pop(0)
    if has_pro:
        y_ref, wp_ref, gp_ref = refs[:3]
        refs = refs[3:]
    mod_ref, wi_ref, wo_ref = refs[:3]
    cast_src = refs[3:3 + n_cast]
    o_ref = refs[3 + n_cast]
    cast_dst = refs[4 + n_cast:4 + 2 * n_cast]
    a_ref = refs[4 + 2 * n_cast]
    x = x_ref[...]
    if has_pro:
        x = x + gp_ref[...] * _dot(y_ref[...], wp_ref[...])
    h = _modulate(x, mod_ref, mod_row).astype(BF16)
    for j in range(D_FF // FF_CHUNK):
        g = _dot(h, wi_ref[:, j * FF_CHUNK:(j + 1) * FF_CHUNK])
        u = _dot(h, wi_ref[:, D_FF + j * FF_CHUNK:D_FF + (j + 1) * FF_CHUNK])
        a_ref[:, j * FF_CHUNK:(j + 1) * FF_CHUNK] = (g * (1.0 / (1.0 + jnp.exp(-g))) * u).astype(BF16)
    gate = mod_ref[mod_row + 2:mod_row + 3, :]
    o_ref[...] = x + (0.5 * gate) * _dot(a_ref[...], wo_ref[...])
    for src, dst in zip(cast_src, cast_dst):
        dst[...] = src[...].astype(BF16)


def _ffn(x, mod, mod_row, w_in, w_out, pro=None, cast=()):
    t, d = x.shape
    tm = min(FFN_ROW_TILE, t)
    n_steps = t // tm
    row = lambda i: (i, 0)
    in_specs = [pl.BlockSpec((tm, d), row)]
    args = [x]
    if pro is not None:
        y, w_p, g_p = pro
        in_specs += [pl.BlockSpec((tm, d), row), _const_spec(w_p.shape), _const_spec(g_p.shape)]
        args += [y, w_p, g_p]
    in_specs += [_const_spec(mod.shape), _const_spec(w_in.shape), _const_spec(w_out.shape)]
    args += [mod, w_in, w_out]
    out_specs = [pl.BlockSpec((tm, d), row)]
    out_shape = [jax.ShapeDtypeStruct((t, d), F32)]
    for stack, lead in cast:
        rows, cols = stack.shape[-2:]
        assert rows % (16 * n_steps) == 0
        blk = rows // n_steps
        in_specs.append(pl.BlockSpec((None,) * len(lead) + (blk, cols), lambda i, lead=lead: tuple(lead) + (i, 0)))
        args.append(stack)
        out_specs.append(pl.BlockSpec((blk, cols), row))
        out_shape.append(jax.ShapeDtypeStruct((rows, cols), BF16))
    outs = pl.pallas_call(
        functools.partial(_ffn_kernel, pro is not None, len(cast), mod_row),
        grid=(n_steps,),
        in_specs=in_specs,
        out_specs=out_specs,
        out_shape=out_shape,
        scratch_shapes=[pltpu.VMEM((tm, D_FF), BF16)],
        compiler_params=_params(1),
        name="ffn_pro" if pro is not None else "ffn",
    )(*args)
    return outs[0], list(outs[1:])


def _rope_pair(r2, gain, cs, first_half):
    ms = jnp.sum(jnp.where(first_half, r2 * r2, 0.0), axis=-1, keepdims=True) * (1.0 / MLA_ROPE)
    t = r2 * lax.rsqrt(ms + EPS) * gain * cs
    return t + pltpu.roll(t, MLA_ROPE, axis=1)


def _mla_proj_kernel(x_ref, mod_ref, cs_ref, wdq_ref, gdq_ref, wuq_ref, gq_ref, qpad_ref, wdkv_ref, gdkv_ref, wuk_ref,
                     gk_ref, wuvt_ref, q_ref, k_ref, vt_ref):
    h = _modulate(x_ref[...], mod_ref, 3).astype(BF16)
    cs = cs_ref[...]
    lane = lax.broadcasted_iota(jnp.int32, (1, LANES), 1)
    first_half = lane < MLA_ROPE
    cq = (_rms(_dot(h, wdq_ref[...])) * gdq_ref[...]).astype(BF16)
    q = _dot(cq, wuq_ref[...])
    for hd in range(MLA_HEADS):
        c0 = hd * 2 * LANES
        qn = q[:, c0:c0 + LANES]
        q_ref[hd, :, 0:LANES] = (_rms(qn) * gq_ref[:, c0:c0 + LANES]).astype(BF16)
        rot = _rope_pair(q[:, c0 + LANES:c0 + 2 * LANES], gq_ref[:, c0 + LANES:c0 + 2 * LANES], cs, first_half)
        q_ref[hd, :, LANES:2 * LANES] = jnp.where(first_half, rot, qpad_ref[...]).astype(BF16)
    kv = _dot(h, wdkv_ref[...])
    ckv = (_rms(kv[:, :MLA_KV_LORA]) * gdkv_ref[...]).astype(BF16)
    rot = _rope_pair(kv[:, MLA_KV_LORA:], gk_ref[:, LANES:2 * LANES], cs, first_half)
    kr = jnp.where(first_half, rot, jnp.where(lane == MLA_ROPE, 1.0, 0.0)).astype(BF16)
    kn = _dot(ckv, wuk_ref[...])
    vt = lax.dot_general(wuvt_ref[...], ckv, _NT, preferred_element_type=F32)
    for hd in range(MLA_HEADS):
        blk = kn[:, hd * LANES:(hd + 1) * LANES]
        k_ref[hd, :, 0:LANES] = (_rms(blk) * gk_ref[:, 0:LANES]).astype(BF16)
        k_ref[hd, :, LANES:2 * LANES] = kr
        vt_ref[hd] = vt[hd * MLA_V:(hd + 1) * MLA_V, :].astype(BF16)


def _mla_proj(x, mod, cs, w):
    t, d = x.shape
    tm = min(ROW_TILE, t)
    consts = [w["w_dq"], w["g_dq"], w["w_uq"], w["g_q"], w["q_pad"], w["w_dkv"], w["g_dkv"], w["w_uk"], w["g_k"], w["w_uvt"]]
    return pl.pallas_call(
        _mla_proj_kernel,
        grid=(t // tm,),
        in_specs=[pl.BlockSpec((tm, d), lambda i: (i, 0)), _const_spec(mod.shape),
                  pl.BlockSpec((tm, LANES), lambda i: (i, 0))] + [_const_spec(a.shape) for a in consts],
        out_specs=[
            pl.BlockSpec((MLA_HEADS, tm, 2 * LANES), lambda i: (0, i, 0)),
            pl.BlockSpec((MLA_HEADS, tm, 2 * LANES), lambda i: (0, i, 0)),
            pl.BlockSpec((MLA_HEADS, None, MLA_V, tm), lambda i: (0, i, 0, 0)),
        ],
        out_shape=[
            jax.ShapeDtypeStruct((MLA_HEADS, t, 2 * LANES), BF16),
            jax.ShapeDtypeStruct((MLA_HEADS, t, 2 * LANES), BF16),
            jax.ShapeDtypeStruct((MLA_HEADS, t // tm, MLA_V, tm), BF16),
        ],
        compiler_params=_params(1),
        name="mla_proj",
    )(x, mod, cs, *consts)


def _mla_attn_kernel(n_main, sub, tv, has_extra, *refs):
    if has_extra:
        q_ref, k_ref, vt_ref, ke_ref, vte_ref, o_ref, acc_ref, s_ref = refs
    else:
        q_ref, k_ref, vt_ref, o_ref, acc_ref, s_ref = refs
    q = q_ref[...]
    tq = q.shape[0]
    tk = sub * tv
    acc_ref[...] = jnp.zeros_like(acc_ref)

    def scores(j):
        k = k_ref[pl.ds(pl.multiple_of(j * tk, tk), tk), :]
        return lax.dot_general(k, q, _NT, preferred_element_type=F32)

    def main_vts(j):
        return [vt_ref[j * sub + c] for c in range(sub)]

    def produce(slot, j):
        sc = scores(j)
        s_ref[slot] = sc
        return jnp.max(sc, axis=0, keepdims=True)

    def update(carry, s, s_max, vts):
        m, l = carry
        m_new = jnp.maximum(m, s_max)
        alpha = jnp.exp2(m - m_new)
        p = jnp.exp2(s - m_new)
        l = alpha * l + jnp.sum(p, axis=0, keepdims=True)
        pb = p.astype(BF16)
        pv = _dot(vts[0], pb[0:vts[0].shape[1], :])
        for c in range(1, len(vts)):
            pv = pv + _dot(vts[c], pb[c * tv:(c + 1) * tv, :])
        acc_ref[...] = alpha * acc_ref[...] + pv
        return m_new, l

    carry = (jnp.full((1, tq), NEG_BIG, F32), jnp.zeros((1, tq), F32))
    max0 = produce(0, 0)
    if n_main > 1:
        assert n_main % 2 == 0

        def body(jj, state):
            carry, max0 = state
            j = 2 * jj
            max1 = produce(1, j + 1)
            carry = update(carry, s_ref[0], max0, main_vts(j))
            max0 = produce(0, j + 2)
            return update(carry, s_ref[1], max1, main_vts(j + 1)), max0

        carry, max0 = lax.fori_loop(0, n_main // 2 - 1, body, (carry, max0))
        max1 = produce(1, n_main - 1)
        carry = update(carry, s_ref[0], max0, main_vts(n_main - 2))
        carry = update(carry, s_ref[1], max1, main_vts(n_main - 1))
    else:
        carry = update(carry, s_ref[0], max0, main_vts(0))
    if has_extra:
        s_e = lax.dot_general(ke_ref[...], q, _NT, preferred_element_type=F32)
        carry = update(carry, s_e, jnp.max(s_e, axis=0, keepdims=True), [vte_ref[...]])
    o_ref[...] = (acc_ref[...] * (1.0 / carry[1])).T.astype(BF16)


def _mla_attn_bounded_kernel(n_main, sub, tv, has_extra, *refs):
    if has_extra:
        q_ref, k_ref, vt_ref, ke_ref, vte_ref, o_ref, acc_ref, l_ref = refs
    else:
        q_ref, k_ref, vt_ref, o_ref, acc_ref, l_ref = refs
    q = q_ref[...]
    tk = sub * tv
    acc_ref[...] = jnp.zeros_like(acc_ref)
    l_ref[...] = jnp.zeros_like(l_ref)

    def accumulate(k, vts):
        p = jnp.exp2(lax.dot_general(k, q, _NT, preferred_element_type=F32))
        l_ref[...] += jnp.sum(p, axis=0, keepdims=True)
        pb = p.astype(BF16)
        pv = _dot(vts[0], pb[0:vts[0].shape[1], :])
        for c in range(1, len(vts)):
            pv = pv + _dot(vts[c], pb[c * tv:(c + 1) * tv, :])
        acc_ref[...] += pv

    def body(j, _):
        k = k_ref[pl.ds(pl.multiple_of(j * tk, tk), tk), :]
        accumulate(k, [vt_ref[j * sub + c] for c in range(sub)])
        return 0

    lax.fori_loop(0, n_main, body, 0, unroll=8)
    if has_extra:
        accumulate(ke_ref[...], [vte_ref[...]])
    o_ref[...] = (acc_ref[...] * (1.0 / l_ref[...])).T.astype(BF16)


def _mla_attn(q, k, vt, extra=None, bounded=False):
    nh, tq_all, dk = q.shape
    tk_all = k.shape[1]
    n_v, tv = vt.shape[1], vt.shape[3]
    sub = 2 if n_v % 2 == 0 else 1
    tq = min(MLA_Q_TILE, tq_all)
    kv_mode = None if bounded else pl.Buffered(1)
    in_specs = [
        pl.BlockSpec((None, tq, dk), lambda h, i: (h, i, 0)),
        pl.BlockSpec((None, tk_all, dk), lambda h, i: (h, 0, 0), pipeline_mode=kv_mode),
        pl.BlockSpec((None, n_v, MLA_V, tv), lambda h, i: (h, 0, 0, 0), pipeline_mode=kv_mode),
    ]
    if bounded:
        body = functools.partial(_mla_attn_bounded_kernel, n_v // sub, sub, tv, extra is not None)
        scratch = [pltpu.VMEM((MLA_V, tq), F32), pltpu.VMEM((1, tq), F32)]
    else:
        body = functools.partial(_mla_attn_kernel, n_v // sub, sub, tv, extra is not None)
        scratch = [pltpu.VMEM((MLA_V, tq), F32), pltpu.VMEM((2, sub * tv, tq), F32)]
    args = [q, k, vt]
    if extra is not None:
        k_e, vt_e = extra
        in_specs += [pl.BlockSpec((None,) + k_e.shape[1:], lambda h, i: (h, 0, 0)),
                     pl.BlockSpec((None, None) + vt_e.shape[2:], lambda h, i: (h, 0, 0, 0))]
        args += [k_e, vt_e]
    return pl.pallas_call(
        body,
        grid=(nh, tq_all // tq),
        in_specs=in_specs,
        out_specs=pl.BlockSpec((tq, MLA_V), lambda h, i: (i, h)),
        out_shape=jax.ShapeDtypeStruct((tq_all, nh * MLA_V), BF16),
        scratch_shapes=scratch,
        compiler_params=_params(2),
        name="mla_attn_bounded" if bounded else "mla_attn",
    )(*args)


def _pool_kernel(t_total, x_ref, xp_ref, xn_ref, mod_ref, o_ref, ext_ref, lvl_a_ref, lvl_b_ref):
    tm = x_ref.shape[0]
    base = pl.program_id(0) * tm
    hc = _modulate(x_ref[...], mod_ref, 3)
    halo_rows = lax.broadcasted_iota(jnp.int32, (POOL_HALO, 1), 0)
    hp = jnp.where(base - POOL_HALO + halo_rows >= 0, _modulate(xp_ref[...], mod_ref, 3), 0.0)
    hn = jnp.where(base + tm + halo_rows < t_total, _modulate(xn_ref[...], mod_ref, 3), 0.0)
    ext_ref[0:POOL_HALO, :] = hp
    ext_ref[POOL_HALO:POOL_HALO + tm, :] = hc
    ext_ref[POOL_HALO + tm:, :] = hn

    tok = base + lax.broadcasted_iota(jnp.int32, (tm, 1), 0)
    for g, win in enumerate(POOL_WINDOWS):
        half = win // 2
        cols = slice(g * POOL_GROUP, (g + 1) * POOL_GROUP)
        cur, length, step, nxt = ext_ref, tm + 2 * POOL_HALO, 1, 0
        while step < half:
            length -= step
            dst = (lvl_a_ref, lvl_b_ref)[nxt]
            dst[0:length, cols] = cur[0:length, cols] + cur[step:step + length, cols]
            cur, step, nxt = dst, 2 * step, 1 - nxt
        win_sum = cur[POOL_HALO - half:POOL_HALO - half + tm, cols] + cur[POOL_HALO:POOL_HALO + tm, cols]
        cnt = (jnp.minimum(tok + half, t_total) - jnp.maximum(tok - half, 0)).astype(F32)
        o_ref[:, cols] = (win_sum / cnt - ext_ref[POOL_HALO:POOL_HALO + tm, cols]).astype(BF16)


def _pool(x, mod):
    t, d = x.shape
    tm = min(ROW_TILE, t)
    per = tm // POOL_HALO
    last = t // POOL_HALO - 1
    return pl.pallas_call(
        functools.partial(_pool_kernel, t),
        grid=(t // tm,),
        in_specs=[
            pl.BlockSpec((tm, d), lambda i: (i, 0)),
            pl.BlockSpec((POOL_HALO, d), lambda i: (jnp.maximum(i * per - 1, 0), 0)),
            pl.BlockSpec((POOL_HALO, d), lambda i: (jnp.minimum((i + 1) * per, last), 0)),
            _const_spec(mod.shape),
        ],
        out_specs=pl.BlockSpec((tm, d), lambda i: (i, 0)),
        out_shape=jax.ShapeDtypeStruct((t, d), BF16),
        scratch_shapes=[pltpu.VMEM((tm + 2 * POOL_HALO, d), F32)] * 3,
        compiler_params=_params(1),
        name="pool",
    )(x, x, x, mod)


def _conv_kernel(t_total, x_ref, xp_ref, xn_ref, mod_ref, wb_ref, wcu_ref, wconv_ref, o_ref, ext_ref, cu_ref):
    tm, d = x_ref.shape
    base = pl.program_id(0) * tm
    ext_ref[0:CONV_HALO, :] = _modulate(xp_ref[...], mod_ref, 3).astype(BF16)
    ext_ref[CONV_HALO:CONV_HALO + tm, :] = _modulate(x_ref[...], mod_ref, 3).astype(BF16)
    ext_ref[CONV_HALO + tm:, :] = _modulate(xn_ref[...], mod_ref, 3).astype(BF16)
    ext = ext_ref[...]
    cu = _dot(ext, wcu_ref[:, 0:d]) * _dot(ext, wcu_ref[:, d:2 * d])
    tok = base - CONV_HALO + lax.broadcasted_iota(jnp.int32, (tm + 2 * CONV_HALO, 1), 0)
    cu_ref[...] = jnp.where((tok >= 0) & (tok < t_total), cu, 0.0)
    z = (wconv_ref[0:1, :] * cu_ref[CONV_HALO - 1:CONV_HALO - 1 + tm, :]
         + wconv_ref[1:2, :] * cu_ref[CONV_HALO:CONV_HALO + tm, :]
         + wconv_ref[2:3, :] * cu_ref[CONV_HALO + 1:CONV_HALO + 1 + tm, :])
    b = _dot(ext_ref[CONV_HALO:CONV_HALO + tm, :], wb_ref[...])
    o_ref[...] = (b * z).astype(BF16)


def _conv(x, mod, w_b, w_cu, w_conv):
    t, d = x.shape
    tm = min(ROW_TILE, t)
    per = tm // CONV_HALO
    last = t // CONV_HALO - 1
    return pl.pallas_call(
        functools.partial(_conv_kernel, t),
        grid=(t // tm,),
        in_specs=[
            pl.BlockSpec((tm, d), lambda i: (i, 0)),
            pl.BlockSpec((CONV_HALO, d), lambda i: (jnp.maximum(i * per - 1, 0), 0)),
            pl.BlockSpec((CONV_HALO, d), lambda i: (jnp.minimum((i + 1) * per, last), 0)),
            _const_spec(mod.shape), _const_spec(w_b.shape), _const_spec(w_cu.shape), _const_spec(w_conv.shape),
        ],
        out_specs=pl.BlockSpec((tm, d), lambda i: (i, 0)),
        out_shape=jax.ShapeDtypeStruct((t, d), BF16),
        scratch_shapes=[pltpu.VMEM((tm + 2 * CONV_HALO, d), BF16), pltpu.VMEM((tm + 2 * CONV_HALO, d), F32)],
        compiler_params=_params(1),
        name="conv",
    )(x, x, x, mod, w_b, w_cu, w_conv)


def _head_rms(v, first_half):
    outs = []
    for c in range(v.shape[1] // LANES):
        blk = v[:, c * LANES:(c + 1) * LANES]
        sq = blk * blk
        tot = jnp.sum(sq, axis=-1, keepdims=True)
        lo = jnp.sum(jnp.where(first_half, sq, 0.0), axis=-1, keepdims=True)
        ms = jnp.where(first_half, lo, tot - lo) * (1.0 / NA_HEAD_DIM)
        outs.append(blk * lax.rsqrt(ms + EPS))
    return outs


def _na_proj_kernel(x_ref, mod_ref, w_ref, gq_ref, gk_ref, q_ref, k_ref, v_ref):
    d = x_ref.shape[1]
    h = _modulate(x_ref[...], mod_ref, 3).astype(BF16)
    first_half = lax.broadcasted_iota(jnp.int32, (1, LANES), 1) < NA_HEAD_DIM
    for c, blk in enumerate(_head_rms(_dot(h, w_ref[:, 0:d]), first_half)):
        q_ref[:, c * LANES:(c + 1) * LANES] = (blk * gq_ref[...]).astype(BF16)
    for c, blk in enumerate(_head_rms(_dot(h, w_ref[:, d:2 * d]), first_half)):
        k_ref[:, c * LANES:(c + 1) * LANES] = (blk * gk_ref[...]).astype(BF16)
    v_ref[...] = _dot(h, w_ref[:, 2 * d:3 * d]).astype(BF16)


def _na_proj(x, mod, w_qkv, g_q2, g_k2):
    t, d = x.shape
    tm = min(ROW_TILE, t)
    spec = pl.BlockSpec((tm, d), lambda i: (i, 0))
    return pl.pallas_call(
        _na_proj_kernel,
        grid=(t // tm,),
        in_specs=[spec, _const_spec(mod.shape), _const_spec(w_qkv.shape), _const_spec(g_q2.shape),
                  _const_spec(g_k2.shape)],
        out_specs=[spec, spec, spec],
        out_shape=[jax.ShapeDtypeStruct((t, d), BF16)] * 3,
        compiler_params=_params(1),
        name="na_proj",
    )(x, mod, w_qkv, g_q2, g_k2)


def _na_attn_kernel(bounded, q_ref, k0_ref, k1_ref, k2_ref, v0_ref, v1_ref, v2_ref, kx_ref, vx_ref, bias_ref,
                    xshift_ref, o_ref):
    nq = q_ref.shape[0]
    first_half = lax.broadcasted_iota(jnp.int32, (1, LANES), 1) < NA_HEAD_DIM
    n_pairs = NA_HEADS // 2

    def scores(p):
        cols = slice(p * LANES, (p + 1) * LANES)
        qp = q_ref[:, cols]
        zero = jnp.zeros_like(qp)
        qs = jnp.concatenate([jnp.where(first_half, qp, zero), jnp.where(first_half, zero, qp)], axis=0)
        parts = []
        for t, k_ref in enumerate((k0_ref, k1_ref, k2_ref)):
            lanes = slice(t * nq, (t + 1) * nq)
            bias = jnp.concatenate([bias_ref[2 * p, :, lanes], bias_ref[2 * p + 1, :, lanes]], axis=0)
            parts.append(lax.dot_general(qs, k_ref[:, cols], _NT, preferred_element_type=F32) + bias)
        parts.append(lax.dot_general(qs, kx_ref[:, cols], _NT, preferred_element_type=F32) + xshift_ref[...])
        return parts

    def finish(p, parts):
        cols = slice(p * LANES, (p + 1) * LANES)
        if not bounded:
            m = jnp.max(jnp.maximum(jnp.maximum(parts[0], parts[1]), jnp.maximum(parts[2], parts[3])),
                        axis=-1, keepdims=True)
        e_sum = None
        o = None
        for s, v_ref in zip(parts, (v0_ref, v1_ref, v2_ref, vx_ref)):
            e = jnp.exp2(s) if bounded else jnp.exp2(s - m)
            part_o = _dot(e.astype(BF16), v_ref[:, cols])
            e_sum = e if e_sum is None else e_sum + e
            o = part_o if o is None else o + part_o
        o = o * (1.0 / jnp.sum(e_sum, axis=-1, keepdims=True))
        o_ref[:, cols] = jnp.where(first_half, o[0:nq], o[nq:2 * nq]).astype(BF16)

    parts_next = scores(0)
    for p in range(n_pairs):
        parts = parts_next
        if p + 1 < n_pairs:
            parts_next = scores(p + 1)
        finish(p, parts)


def _na_attn(q, k, v, k_ctx, v_ctx, bias, xshift, bounded=False):
    s, d = q.shape
    chunk = NA_Q_ROWS * GRID_W
    n_steps = s // chunk

    def band(t):
        return pl.BlockSpec((chunk, d), lambda i: (jnp.clip(i - 1, 0, n_steps - 3) + t, 0))

    cur = pl.BlockSpec((chunk, d), lambda i: (i, 0))
    bias_spec = pl.BlockSpec((None,) + bias.shape[1:],
                             lambda i: (jnp.where(i == 0, 0, jnp.where(i == n_steps - 1, 2, 1)), 0, 0, 0),
                             pipeline_mode=pl.Buffered(1))
    return pl.pallas_call(
        functools.partial(_na_attn_kernel, bounded),
        grid=(n_steps,),
        in_specs=[cur, band(0), band(1), band(2), band(0), band(1), band(2), _const_spec(k_ctx.shape),
                  _const_spec(v_ctx.shape), bias_spec, _const_spec(xshift.shape)],
        out_specs=cur,
        out_shape=jax.ShapeDtypeStruct((s, d), BF16),
        compiler_params=_params(1),
        name="na_attn_bounded" if bounded else "na_attn",
    )(q, k, k, k, v, v, v, k_ctx, v_ctx, bias, xshift)


def _na_band_pattern(step, n_rows):
    n_steps = n_rows // NA_Q_ROWS
    band0 = NA_Q_ROWS * int(np.clip(step - 1, 0, n_steps - 3))
    r = step * NA_Q_ROWS + np.arange(NA_Q_ROWS)[:, None]
    key_row = band0 + np.arange(3 * NA_Q_ROWS)[None, :]
    win0 = np.clip(r - NA_ROWS // 2, 0, n_rows - NA_ROWS)
    valid = (key_row >= win0) & (key_row < win0 + NA_ROWS)
    return np.where(valid, key_row - r + (NA_ROWS - 1), 0), valid


def _na_bias_tables(rpb, n_rows, shift):
    n_steps = n_rows // NA_Q_ROWS
    assert n_steps >= 4
    patterns = [_na_band_pattern(t, n_rows) for t in range(n_steps)]
    for dr, valid in patterns[2:-1]:
        assert np.array_equal(dr, patterns[1][0]) and np.array_equal(valid, patterns[1][1])
    cols = np.arange(GRID_W)
    col_start = np.clip(cols - NA_COLS // 2, 0, GRID_W - NA_COLS)
    kc = np.arange(GRID_W)
    col_ok = (kc[None, :] >= col_start[:, None]) & (kc[None, :] < col_start[:, None] + NA_COLS)
    pad = GRID_W - NA_COLS
    padded = jnp.pad(rpb * LOG2_E - shift, ((0, 0), (0, 0), (pad, pad)))
    toeplitz = jnp.stack([padded[:, :, GRID_W - 1 - c:2 * GRID_W - 1 - c] for c in range(GRID_W)], axis=2)
    toeplitz = jnp.where(col_ok[None, None], toeplitz, NEG_BIG)
    n_band = 3 * NA_Q_ROWS
    lead = NA_Q_ROWS
    n_dr = toeplitz.shape[1]
    strip = jnp.transpose(toeplitz, (0, 2, 1, 3)).reshape(rpb.shape[0], GRID_W, n_dr * GRID_W)
    strip = jnp.pad(strip, ((0, 0), (0, 0), (lead * GRID_W, lead * GRID_W)), constant_values=NEG_BIG)
    plan = []
    for dr, valid in (patterns[0], patterns[1], patterns[-1]):
        for jq in range(NA_Q_ROWS):
            rows_ok = np.nonzero(valid[jq])[0]
            lo, hi = int(rows_ok[0]), int(rows_ok[-1]) + 1
            assert hi - lo == len(rows_ok)
            start = int(dr[jq, lo]) - lo + lead
            assert 0 <= start and start + n_band <= n_dr + 2 * lead
            assert all(dr[jq, i] == start - lead + i for i in range(lo, hi))
            plan.append((start, lo, hi))

    def assemble(strip_ref, out_ref):
        lanes = lax.broadcasted_iota(jnp.int32, (1, n_band * GRID_W), 1)
        for n, (start, lo, hi) in enumerate(plan):
            kind, jq = divmod(n, NA_Q_ROWS)
            window = strip_ref[:, start * GRID_W:(start + n_band) * GRID_W]
            row_ok = (lanes >= lo * GRID_W) & (lanes < hi * GRID_W)
            out_ref[kind, jq * GRID_W:(jq + 1) * GRID_W, :] = jnp.where(row_ok, window, NEG_BIG)

    n_heads = rpb.shape[0]
    return pl.pallas_call(
        assemble,
        grid=(n_heads,),
        in_specs=[pl.BlockSpec((None,) + strip.shape[1:], lambda h: (h, 0, 0))],
        out_specs=pl.BlockSpec((3, None, NA_Q_ROWS * GRID_W, n_band * GRID_W), lambda h: (0, h, 0, 0)),
        out_shape=jax.ShapeDtypeStruct((3, n_heads, NA_Q_ROWS * GRID_W, n_band * GRID_W), F32),
        compiler_params=_params(1),
        name="na_bias_tables",
    )(strip)


def _rope_tables(t):
    pos = jnp.arange(t)
    row = (pos // GRID_W).astype(F32)
    col = (pos % GRID_W).astype(F32)
    n = MLA_ROPE // 4
    freqs = ROPE_BASE ** (-jnp.arange(n, dtype=F32) / n)
    ang = jnp.concatenate([row[:, None] * freqs, col[:, None] * freqs], axis=-1)
    cos, sin = jnp.cos(ang), jnp.sin(ang)
    return jnp.concatenate([cos, cos, sin, sin], axis=-1)


_HALF_SPLIT = np.concatenate([np.arange(0, MLA_ROPE, 2), np.arange(1, MLA_ROPE, 2)])


def _rope_cols(w):
    hs = w[..., _HALF_SPLIT]
    return jnp.concatenate([hs, -hs[..., MLA_ROPE // 2:], hs[..., :MLA_ROPE // 2]], axis=-1)


def _rope_gain(g):
    hs = g[_HALF_SPLIT]
    return jnp.concatenate([hs, hs[MLA_ROPE // 2:], hs[:MLA_ROPE // 2]])


def _mla_weights(w_dq, g_dq, w_uq, w_dkv, g_dkv, w_uk, w_uv, g_qn, g_qr, g_kn, g_kr):
    w_uq_ext = jnp.concatenate([w_uq[..., :MLA_NOPE], _rope_cols(w_uq[..., MLA_NOPE:])], axis=-1)
    g_q = jnp.tile(jnp.concatenate([g_qn, _rope_gain(g_qr)]) * (MLA_SCALE * LOG2_E), MLA_HEADS)
    q_norm = jnp.sqrt(MLA_NOPE * jnp.max(g_qn * g_qn) + MLA_ROPE * jnp.max(g_qr * g_qr)) * (MLA_SCALE * LOG2_E)
    k_norm = jnp.sqrt(MLA_NOPE * jnp.max(g_kn * g_kn) + MLA_ROPE * jnp.max(g_kr * g_kr))
    bound = BOUND_SLACK * q_norm * k_norm
    q_pad = jnp.zeros((1, LANES), F32).at[0, MLA_ROPE].set(-bound)
    return {
        "score_bound": bound,
        "q_pad": q_pad,
        "w_dq": w_dq.astype(BF16),
        "g_dq": g_dq[None, :],
        "w_uq": w_uq_ext.reshape(MLA_Q_LORA, -1).astype(BF16),
        "g_q": g_q[None, :],
        "w_dkv": jnp.concatenate([w_dkv[:, :MLA_KV_LORA], _rope_cols(w_dkv[:, MLA_KV_LORA:])], axis=-1).astype(BF16),
        "g_dkv": g_dkv[None, :],
        "w_uk": w_uk.reshape(MLA_KV_LORA, -1).astype(BF16),
        "g_k": jnp.concatenate([g_kn, _rope_gain(g_kr)])[None, :],
        "w_uvt": w_uv.reshape(MLA_KV_LORA, -1).T.astype(BF16),
    }


def _block_diag(w):
    g, c, _ = w.shape
    out = jnp.zeros((g * c, g * c), w.dtype)
    for i in range(g):
        out = out.at[i * c:(i + 1) * c, i * c:(i + 1) * c].set(w[i])
    return out


def kernel(x, c, ctx, c_ctx, mod_w, mod_b, ffn_w_in, ffn_w_out, mla_w_dq, mla_g_dq, mla_w_uq, mla_w_dkv, mla_g_dkv, mla_w_uk, mla_w_uv, mla_g_qn, mla_g_qr, mla_g_kn, mla_g_kr, mla_w_o, pool_w, pool_scale, na_w_qkv, na_g_q, na_g_k, na_rpb, na_w_o, conv_w_in, conv_w, conv_w_out):
    assert x.shape[0] == 1 and x.shape[2] == D_MODEL and x.shape[1] % ROW_TILE == 0
    s = x.shape[1]
    d = D_MODEL
    xs = x[0]
    hc = ctx[0]
    n_ctx = hc.shape[0]

    cond = jnp.zeros((8, d), F32).at[0].set(c[0]).at[1].set(c_ctx)
    mods = _ada_params(cond, mod_w, mod_b)
    wts = {(0, 0): (ffn_w_in[0, 0].astype(BF16), ffn_w_out[0, 0].astype(BF16))}

    def ffn_x(xs, mx, layer, cast_next):
        cast = [(stack, key) for key in cast_next for stack in (ffn_w_in, ffn_w_out)]
        xs, slabs = _ffn(xs, mx, 0, *wts[(layer, 0)], cast=cast)
        for n, key in enumerate(cast_next):
            wts[key] = (slabs[2 * n], slabs[2 * n + 1])
        return xs

    mx = mods[0, 0].reshape(N_MOD, d)
    mc = mods[0, 1].reshape(N_MOD, d)
    xs = ffn_x(xs, mx, 0, [(0, 1), (1, 0)])
    hc = _ffn(hc, mc, 0, *wts[(0, 0)])[0]
    mw = _mla_weights(mla_w_dq[0], mla_g_dq[0], mla_w_uq[0], mla_w_dkv[0], mla_g_dkv[0], mla_w_uk[0], mla_w_uv[0],
                      mla_g_qn[0], mla_g_qr[0], mla_g_kn[0], mla_g_kr[0])
    no_rope = jnp.concatenate([jnp.ones((n_ctx, LANES // 2), F32), jnp.zeros((n_ctx, LANES // 2), F32)], axis=-1)
    q_x, k_x, vt_x = _mla_proj(xs, mx, _rope_tables(s), mw)
    q_c, k_c, vt_c = _mla_proj(hc, mc, no_rope, mw)
    o_x = lax.cond(2.0 * mw["score_bound"] <= SCORE_RANGE_LOG2,
                   lambda q, k, vt, ke, vte: _mla_attn(q, k, vt, extra=(ke, vte), bounded=True),
                   lambda q, k, vt, ke, vte: _mla_attn(q, k, vt, extra=(ke, vte)),
                   q_x, k_x, vt_x, k_c, vt_c)
    o_c = _mla_attn(q_c, k_c, vt_c)
    w_o = mla_w_o[0].astype(BF16)
    xs = _ffn(xs, mx, 6, *wts[(0, 1)], pro=(o_x, w_o, mx[5:6]))[0]
    hc = _ffn(hc, mc, 6, *wts[(0, 1)], pro=(o_c, w_o, mc[5:6]))[0]

    mx = mods[1, 0].reshape(N_MOD, d)
    mc = mods[1, 1].reshape(N_MOD, d)
    xs = ffn_x(xs, mx, 1, [(1, 1), (2, 0)])
    hc = _ffn(hc, mc, 0, *wts[(1, 0)])[0]
    w_p = _block_diag(pool_w[0]).astype(BF16)
    xs = _ffn(xs, mx, 6, *wts[(1, 1)], pro=(_pool(xs, mx), w_p, mx[5:6] * pool_scale[0][None, :]))[0]
    hc = _ffn(hc, mc, 6, *wts[(1, 1)], pro=(_pool(hc, mc), w_p, mc[5:6] * pool_scale[0][None, :]))[0]

    mx = mods[2, 0].reshape(N_MOD, d)
    mc = mods[2, 1].reshape(N_MOD, d)
    xs = ffn_x(xs, mx, 2, [(2, 1), (3, 0)])
    hc = _ffn(hc, mc, 0, *wts[(2, 0)])[0]
    w_qkv = na_w_qkv[0].astype(BF16)
    g_q2 = jnp.tile(na_g_q[0] * (NA_SCALE * LOG2_E), 2)[None, :]
    g_k2 = jnp.tile(na_g_k[0], 2)[None, :]
    q_n, k_n, v_n = _na_proj(xs, mx, w_qkv, g_q2, g_k2)
    _, k_nc, v_nc = _na_proj(hc, mc, w_qkv, g_q2, g_k2)
    qk_bound = BOUND_SLACK * (NA_SCALE * LOG2_E) * NA_HEAD_DIM * jnp.sqrt(
        jnp.max(na_g_q[0] * na_g_q[0]) * jnp.max(na_g_k[0] * na_g_k[0]))
    shift = qk_bound + jnp.maximum(jnp.max(na_rpb[0]) * LOG2_E, 0.0)
    bias = _na_bias_tables(na_rpb[0], s // GRID_W, shift)
    xshift = jnp.full((1, n_ctx), -shift, F32)
    o_n = lax.cond(qk_bound + shift <= SCORE_RANGE_LOG2,
                   lambda *a: _na_attn(*a, bounded=True), lambda *a: _na_attn(*a),
                   q_n, k_n, v_n, k_nc, v_nc, bias, xshift)
    xs = _ffn(xs, mx, 6, *wts[(2, 1)], pro=(o_n, na_w_o[0].astype(BF16), mx[5:6]))[0]

    mx = mods[3, 0].reshape(N_MOD, d)
    xs = ffn_x(xs, mx, 3, [(3, 1)])
    w_ci = conv_w_in[0].astype(BF16)
    y_c = _conv(xs, mx, w_ci[:, :d], w_ci[:, d:], conv_w[0])
    xs = _ffn(xs, mx, 6, *wts[(3, 1)], pro=(y_c, conv_w_out[0].astype(BF16), mx[5:6]))[0]
    return xs[None]
```

```python
import functools

import jax
import jax.numpy as jnp
import numpy as np
from jax import lax
from jax.experimental import pallas as pl
from jax.experimental.pallas import tpu as pltpu

F32 = jnp.float32
BF16 = jnp.bfloat16

D_MODEL = 1024
DEPTH = 4
GRID_W = 64
N_MOD = 9
D_FF = 2816
EPS = 1e-6
MLA_HEADS = 8
MLA_Q_LORA = 384
MLA_KV_LORA = 256
MLA_NOPE = 128
MLA_ROPE = 64
MLA_V = 128
MLA_SCALE = (MLA_NOPE + MLA_ROPE) ** -0.5
ROPE_BASE = 10000.0
POOL_WINDOWS = (2, 4, 8, 16)
POOL_GROUP = D_MODEL // len(POOL_WINDOWS)
NA_HEADS = 16
NA_HEAD_DIM = D_MODEL // NA_HEADS
NA_ROWS = 8
NA_COLS = 16
NA_SCALE = NA_HEAD_DIM ** -0.5

LANES = 128
MXU_COLS = 256
VMEM_LIMIT = 56 * 1024 * 1024
ROW_TILE = 512
FFN_ROW_TILE = 1024
MLA_Q_TILE = 2048
FF_CHUNK = MXU_COLS
POOL_HALO = 8
CONV_HALO = 16
NA_Q_ROWS = 4
NEG_BIG = -1e30
LOG2_E = 1.4426950408889634
SCORE_RANGE_LOG2 = 60.0
BOUND_SLACK = 1.01

_NT = (((1,), (1,)), ((), ()))


def _params(n_axes, flags=None):
    return pltpu.CompilerParams(dimension_semantics=("arbitrary",) * n_axes, vmem_limit_bytes=VMEM_LIMIT, flags=flags)


def _const_spec(shape):
    nd = len(shape)
    return pl.BlockSpec(shape, lambda *_: (0,) * nd, pipeline_mode=pl.Buffered(1))


def _rms(x):
    return x * lax.rsqrt(jnp.mean(x * x, axis=-1, keepdims=True) + EPS)


def _modulate(x, mod_ref, row):
    shift = mod_ref[row:row + 1, :]
    scale = mod_ref[row + 1:row + 2, :]
    return _rms(x) * (1.0 + scale) + shift


def _dot(a, b):
    return jnp.dot(a, b, preferred_element_type=F32)


def _ada_kernel(cond_ref, w_ref, b_ref, o_ref):
    cnd = cond_ref[...]
    s = (cnd * (1.0 / (1.0 + jnp.exp(-cnd)))).astype(BF16)
    o_ref[...] = _dot(s, w_ref[...].astype(BF16)) + b_ref[...]


def _ada_params(cond, mod_w, mod_b):
    depth, d, n = mod_w.shape
    tn = n // 8
    return pl.pallas_call(
        _ada_kernel,
        grid=(depth, n // tn),
        in_specs=[
            pl.BlockSpec((8, d), lambda i, j: (0, 0)),
            pl.BlockSpec((None, d, tn), lambda i, j: (i, 0, j)),
            pl.BlockSpec((None, 1, tn), lambda i, j: (i, 0, j)),
        ],
        out_specs=pl.BlockSpec((None, 8, tn), lambda i, j: (i, 0, j)),
        out_shape=jax.ShapeDtypeStruct((depth, 8, n), F32),
        compiler_params=_params(2),
        name="ada_params",
    )(cond, mod_w, mod_b.reshape(depth, 1, n))


def _ffn_kernel(has_pro, n_cast, mod_row, *refs):
    refs = list(refs)
    x_ref = refs.pop(0)
    if has_pro:
        y_ref, wp_ref, gp_ref = refs[:3]
        refs = refs[3:]
    mod_ref, wi_ref, wo_ref = refs[:3]
    cast_src = refs[3:3 + n_cast]
    o_ref = refs[3 + n_cast]
    cast_dst = refs[4 + n_cast:4 + 2 * n_cast]
    a_ref = refs[4 + 2 * n_cast]
    x = x_ref[...]
    if has_pro:
        x = x + gp_ref[...] * _dot(y_ref[...], wp_ref[...])
    h = _modulate(x, mod_ref, mod_row).astype(BF16)
    for j in range(D_FF // FF_CHUNK):
        g = _dot(h, wi_ref[:, j * FF_CHUNK:(j + 1) * FF_CHUNK])
        u = _dot(h, wi_ref[:, D_FF + j * FF_CHUNK:D_FF + (j + 1) * FF_CHUNK])
        a_ref[:, j * FF_CHUNK:(j + 1) * FF_CHUNK] = (g * (1.0 / (1.0 + jnp.exp(-g))) * u).astype(BF16)
    gate = mod_ref[mod_row + 2:mod_row + 3, :]
    o_ref[...] = x + (0.5 * gate) * _dot(a_ref[...], wo_ref[...])
    for src, dst in zip(cast_src, cast_dst):
        dst[...] = src[...].astype(BF16)


def _ffn(x, mod, mod_row, w_in, w_out, pro=None, cast=()):
    t, d = x.shape
    tm = min(FFN_ROW_TILE, t)
    n_steps = t // tm
    row = lambda i: (i, 0)
    in_specs = [pl.BlockSpec((tm, d), row)]
    args = [x]
    if pro is not None:
        y, w_p, g_p = pro
        in_specs += [pl.BlockSpec((tm, d), row), _const_spec(w_p.shape), _const_spec(g_p.shape)]
        args += [y, w_p, g_p]
    in_specs += [_const_spec(mod.shape), _const_spec(w_in.shape), _const_spec(w_out.shape)]
    args += [mod, w_in, w_out]
    out_specs = [pl.BlockSpec((tm, d), row)]
    out_shape = [jax.ShapeDtypeStruct((t, d), F32)]
    for stack, lead in cast:
        rows, cols = stack.shape[-2:]
        assert rows % (16 * n_steps) == 0
        blk = rows // n_steps
        in_specs.append(pl.BlockSpec((None,) * len(lead) + (blk, cols), lambda i, lead=lead: tuple(lead) + (i, 0)))
        args.append(stack)
        out_specs.append(pl.BlockSpec((blk, cols), row))
        out_shape.append(jax.ShapeDtypeStruct((rows, cols), BF16))
    outs = pl.pallas_call(
        functools.partial(_ffn_kernel, pro is not None, len(cast), mod_row),
        grid=(n_steps,),
        in_specs=in_specs,
        out_specs=out_specs,
        out_shape=out_shape,
        scratch_shapes=[pltpu.VMEM((tm, D_FF), BF16)],
        compiler_params=_params(1),
        name="ffn_pro" if pro is not None else "ffn",
    )(*args)
    return outs[0], list(outs[1:])


def _rope_pair(r2, gain, cs, first_half):
    ms = jnp.sum(jnp.where(first_half, r2 * r2, 0.0), axis=-1, keepdims=True) * (1.0 / MLA_ROPE)
    t = r2 * lax.rsqrt(ms + EPS) * gain * cs
    return t + pltpu.roll(t, MLA_ROPE, axis=1)


def _mla_proj_kernel(x_ref, mod_ref, cs_ref, cst_ref, wdq_ref, gdq_ref, wuqt_ref, gqt_ref, qpadt_ref, wdkv_ref,
                     gdkv_ref, wuk_ref, gk_ref, wuvt_ref, qt_ref, k_ref, vt_ref):
    h = _modulate(x_ref[...], mod_ref, 3).astype(BF16)
    cs = cs_ref[...]
    lane = lax.broadcasted_iota(jnp.int32, (1, LANES), 1)
    first_half = lane < MLA_ROPE
    tm = x_ref.shape[0]

    def lanes_of(tab_ref, rows):
        return jnp.concatenate([tab_ref[rows, :]] * (tm // LANES), axis=1)

    cq = (_rms(_dot(h, wdq_ref[...])) * gdq_ref[...]).astype(BF16)
    qt = lax.dot_general(wuqt_ref[...], cq, _NT, preferred_element_type=F32)
    cst = cst_ref[...]
    for hd in range(MLA_HEADS):
        r0 = hd * 2 * LANES
        nope = qt[r0:r0 + MLA_NOPE, :]
        inv = lax.rsqrt(jnp.mean(nope * nope, axis=0, keepdims=True) + EPS)
        qt_ref[hd, 0:MLA_NOPE, :] = (nope * inv * lanes_of(gqt_ref, slice(r0, r0 + MLA_NOPE))).astype(BF16)
        r1 = r0 + MLA_NOPE
        rope = qt[r1:r1 + MLA_ROPE, :]
        swap = qt[r1 + MLA_ROPE:r1 + 2 * MLA_ROPE, :]
        inv = lax.rsqrt(jnp.mean(rope * rope, axis=0, keepdims=True) + EPS)
        rot = (rope * lanes_of(gqt_ref, slice(r1, r1 + MLA_ROPE)) * cst[0:MLA_ROPE, :]
               + swap * lanes_of(gqt_ref, slice(r1 + MLA_ROPE, r1 + 2 * MLA_ROPE)) * cst[MLA_ROPE:, :]) * inv
        qt_ref[hd, MLA_NOPE:MLA_NOPE + MLA_ROPE, :] = rot.astype(BF16)
        qt_ref[hd, MLA_NOPE + MLA_ROPE:, :] = lanes_of(qpadt_ref, slice(None)).astype(BF16)
    kv = _dot(h, wdkv_ref[...])
    ckv = (_rms(kv[:, :MLA_KV_LORA]) * gdkv_ref[...]).astype(BF16)
    rot = _rope_pair(kv[:, MLA_KV_LORA:], gk_ref[:, LANES:2 * LANES], cs, first_half)
    kr = jnp.where(first_half, rot, jnp.where(lane == MLA_ROPE, 1.0, 0.0)).astype(BF16)
    kn = _dot(ckv, wuk_ref[...])
    vt = lax.dot_general(wuvt_ref[...], ckv, _NT, preferred_element_type=F32)
    for hd in range(MLA_HEADS):
        blk = kn[:, hd * LANES:(hd + 1) * LANES]
        k_ref[hd, :, 0:LANES] = (_rms(blk) * gk_ref[:, 0:LANES]).astype(BF16)
        k_ref[hd, :, LANES:2 * LANES] = kr
        vt_ref[hd] = vt[hd * MLA_V:(hd + 1) * MLA_V, :].astype(BF16)


def _mla_proj(x, mod, cs, w):
    t, d = x.shape
    tm = min(ROW_TILE, t)
    consts = [w["w_dq"], w["g_dq"], w["w_uqt"], w["g_qt"], w["q_padt"], w["w_dkv"], w["g_dkv"], w["w_uk"], w["g_k"], w["w_uvt"]]
    return pl.pallas_call(
        _mla_proj_kernel,
        grid=(t // tm,),
        in_specs=[pl.BlockSpec((tm, d), lambda i: (i, 0)), _const_spec(mod.shape),
                  pl.BlockSpec((tm, LANES), lambda i: (i, 0)), pl.BlockSpec((LANES, tm), lambda i: (0, i))]
        + [_const_spec(a.shape) for a in consts],
        out_specs=[
            pl.BlockSpec((MLA_HEADS, 2 * LANES, tm), lambda i: (0, 0, i)),
            pl.BlockSpec((MLA_HEADS, tm, 2 * LANES), lambda i: (0, i, 0)),
            pl.BlockSpec((MLA_HEADS, None, MLA_V, tm), lambda i: (0, i, 0, 0)),
        ],
        out_shape=[
            jax.ShapeDtypeStruct((MLA_HEADS, 2 * LANES, t), BF16),
            jax.ShapeDtypeStruct((MLA_HEADS, t, 2 * LANES), BF16),
            jax.ShapeDtypeStruct((MLA_HEADS, t // tm, MLA_V, tm), BF16),
        ],
        compiler_params=_params(1),
        name="mla_proj",
    )(x, mod, cs, cs.T, *consts)


def _mla_attn_kernel(n_main, sub, tv, has_extra, *refs):
    if has_extra:
        q_ref, k_ref, vt_ref, ke_ref, vte_ref, o_ref, acc_ref, s_ref = refs
    else:
        q_ref, k_ref, vt_ref, o_ref, acc_ref, s_ref = refs
    qt = q_ref[...]
    tq = qt.shape[1]
    tk = sub * tv
    acc_ref[...] = jnp.zeros_like(acc_ref)

    def scores(j):
        k = k_ref[pl.ds(pl.multiple_of(j * tk, tk), tk), :]
        return _dot(k, qt)

    def main_vts(j):
        return [vt_ref[j * sub + c] for c in range(sub)]

    def produce(slot, j):
        sc = scores(j)
        s_ref[slot] = sc
        return jnp.max(sc, axis=0, keepdims=True)

    def update(carry, s, s_max, vts):
        m, l = carry
        m_new = jnp.maximum(m, s_max)
        alpha = jnp.exp2(m - m_new)
        p = jnp.exp2(s - m_new)
        l = alpha * l + jnp.sum(p, axis=0, keepdims=True)
        pb = p.astype(BF16)
        pv = _dot(vts[0], pb[0:vts[0].shape[1], :])
        for c in range(1, len(vts)):
            pv = pv + _dot(vts[c], pb[c * tv:(c + 1) * tv, :])
        acc_ref[...] = alpha * acc_ref[...] + pv
        return m_new, l

    carry = (jnp.full((1, tq), NEG_BIG, F32), jnp.zeros((1, tq), F32))
    max0 = produce(0, 0)
    if n_main > 1:
        assert n_main % 2 == 0

        def body(jj, state):
            carry, max0 = state
            j = 2 * jj
            max1 = produce(1, j + 1)
            carry = update(carry, s_ref[0], max0, main_vts(j))
            max0 = produce(0, j + 2)
            return update(carry, s_ref[1], max1, main_vts(j + 1)), max0

        carry, max0 = lax.fori_loop(0, n_main // 2 - 1, body, (carry, max0))
        max1 = produce(1, n_main - 1)
        carry = update(carry, s_ref[0], max0, main_vts(n_main - 2))
        carry = update(carry, s_ref[1], max1, main_vts(n_main - 1))
    else:
        carry = update(carry, s_ref[0], max0, main_vts(0))
    if has_extra:
        s_e = _dot(ke_ref[...], qt)
        carry = update(carry, s_e, jnp.max(s_e, axis=0, keepdims=True), [vte_ref[...]])
    o_ref[...] = (acc_ref[...] * (1.0 / carry[1])).T.astype(BF16)


def _mla_attn_bounded_kernel(n_main, sub, tv, has_extra, *refs):
    if has_extra:
        q_ref, k_ref, vt_ref, ke_ref, vte_ref, o_ref, acc_ref, l_ref = refs
    else:
        q_ref, k_ref, vt_ref, o_ref, acc_ref, l_ref = refs
    qt = q_ref[...]
    tk = sub * tv
    acc_ref[...] = jnp.zeros_like(acc_ref)
    l_ref[...] = jnp.zeros_like(l_ref)

    def accumulate(k, vts):
        p = jnp.exp2(_dot(k, qt))
        l_ref[...] += jnp.sum(p, axis=0, keepdims=True)
        pb = p.astype(BF16)
        pv = _dot(vts[0], pb[0:vts[0].shape[1], :])
        for c in range(1, len(vts)):
            pv = pv + _dot(vts[c], pb[c * tv:(c + 1) * tv, :])
        acc_ref[...] += pv

    def body(j, _):
        k = k_ref[pl.ds(pl.multiple_of(j * tk, tk), tk), :]
        accumulate(k, [vt_ref[j * sub + c] for c in range(sub)])
        return 0

    lax.fori_loop(0, n_main, body, 0, unroll=8)
    if has_extra:
        accumulate(ke_ref[...], [vte_ref[...]])
    o_ref[...] = (acc_ref[...] * (1.0 / l_ref[...])).T.astype(BF16)


def _mla_attn(q, k, vt, extra=None, bounded=False):
    nh, dk, tq_all = q.shape
    tk_all = k.shape[1]
    n_v, tv = vt.shape[1], vt.shape[3]
    sub = 2 if n_v % 2 == 0 else 1
    tq = min(MLA_Q_TILE, tq_all)
    kv_mode = None if bounded else pl.Buffered(1)
    in_specs = [
        pl.BlockSpec((None, dk, tq), lambda h, i: (h, 0, i)),
        pl.BlockSpec((None, tk_all, dk), lambda h, i: (h, 0, 0), pipeline_mode=kv_mode),
        pl.BlockSpec((None, n_v, MLA_V, tv), lambda h, i: (h, 0, 0, 0), pipeline_mode=kv_mode),
    ]
    if bounded:
        body = functools.partial(_mla_attn_bounded_kernel, n_v // sub, sub, tv, extra is not None)
        scratch = [pltpu.VMEM((MLA_V, tq), F32), pltpu.VMEM((1, tq), F32)]
    else:
        body = functools.partial(_mla_attn_kernel, n_v // sub, sub, tv, extra is not None)
        scratch = [pltpu.VMEM((MLA_V, tq), F32), pltpu.VMEM((2, sub * tv, tq), F32)]
    args = [q, k, vt]
    if extra is not None:
        k_e, vt_e = extra
        in_specs += [pl.BlockSpec((None,) + k_e.shape[1:], lambda h, i: (h, 0, 0)),
                     pl.BlockSpec((None, None) + vt_e.shape[2:], lambda h, i: (h, 0, 0, 0))]
        args += [k_e, vt_e]
    return pl.pallas_call(
        body,
        grid=(nh, tq_all // tq),
        in_specs=in_specs,
        out_specs=pl.BlockSpec((tq, MLA_V), lambda h, i: (i, h)),
        out_shape=jax.ShapeDtypeStruct((tq_all, nh * MLA_V), BF16),
        scratch_shapes=scratch,
        compiler_params=_params(2),
        name="mla_attn_bounded" if bounded else "mla_attn",
    )(*args)


def _pool_kernel(t_total, x_ref, xp_ref, xn_ref, mod_ref, o_ref, ext_ref, lvl_a_ref, lvl_b_ref):
    tm = x_ref.shape[0]
    base = pl.program_id(0) * tm
    hc = _modulate(x_ref[...], mod_ref, 3)
    halo_rows = lax.broadcasted_iota(jnp.int32, (POOL_HALO, 1), 0)
    hp = jnp.where(base - POOL_HALO + halo_rows >= 0, _modulate(xp_ref[...], mod_ref, 3), 0.0)
    hn = jnp.where(base + tm + halo_rows < t_total, _modulate(xn_ref[...], mod_ref, 3), 0.0)
    ext_ref[0:POOL_HALO, :] = hp
    ext_ref[POOL_HALO:POOL_HALO + tm, :] = hc
    ext_ref[POOL_HALO + tm:, :] = hn

    tok = base + lax.broadcasted_iota(jnp.int32, (tm, 1), 0)
    for g, win in enumerate(POOL_WINDOWS):
        half = win // 2
        cols = slice(g * POOL_GROUP, (g + 1) * POOL_GROUP)
        cur, length, step, nxt = ext_ref, tm + 2 * POOL_HALO, 1, 0
        while step < half:
            length -= step
            dst = (lvl_a_ref, lvl_b_ref)[nxt]
            dst[0:length, cols] = cur[0:length, cols] + cur[step:step + length, cols]
            cur, step, nxt = dst, 2 * step, 1 - nxt
        win_sum = cur[POOL_HALO - half:POOL_HALO - half + tm, cols] + cur[POOL_HALO:POOL_HALO + tm, cols]
        cnt = (jnp.minimum(tok + half, t_total) - jnp.maximum(tok - half, 0)).astype(F32)
        o_ref[:, cols] = (win_sum / cnt - ext_ref[POOL_HALO:POOL_HALO + tm, cols]).astype(BF16)


def _pool(x, mod):
    t, d = x.shape
    tm = min(ROW_TILE, t)
    per = tm // POOL_HALO
    last = t // POOL_HALO - 1
    return pl.pallas_call(
        functools.partial(_pool_kernel, t),
        grid=(t // tm,),
        in_specs=[
            pl.BlockSpec((tm, d), lambda i: (i, 0)),
            pl.BlockSpec((POOL_HALO, d), lambda i: (jnp.maximum(i * per - 1, 0), 0)),
            pl.BlockSpec((POOL_HALO, d), lambda i: (jnp.minimum((i + 1) * per, last), 0)),
            _const_spec(mod.shape),
        ],
        out_specs=pl.BlockSpec((tm, d), lambda i: (i, 0)),
        out_shape=jax.ShapeDtypeStruct((t, d), BF16),
        scratch_shapes=[pltpu.VMEM((tm + 2 * POOL_HALO, d), F32)] * 3,
        compiler_params=_params(1),
        name="pool",
    )(x, x, x, mod)


def _conv_kernel(t_total, x_ref, xp_ref, xn_ref, mod_ref, wb_ref, wcu_ref, wconv_ref, o_ref, ext_ref, cu_ref):
    tm, d = x_ref.shape
    base = pl.program_id(0) * tm
    ext_ref[0:CONV_HALO, :] = _modulate(xp_ref[...], mod_ref, 3).astype(BF16)
    ext_ref[CONV_HALO:CONV_HALO + tm, :] = _modulate(x_ref[...], mod_ref, 3).astype(BF16)
    ext_ref[CONV_HALO + tm:, :] = _modulate(xn_ref[...], mod_ref, 3).astype(BF16)
    ext = ext_ref[...]
    cu = _dot(ext, wcu_ref[:, 0:d]) * _dot(ext, wcu_ref[:, d:2 * d])
    tok = base - CONV_HALO + lax.broadcasted_iota(jnp.int32, (tm + 2 * CONV_HALO, 1), 0)
    cu_ref[...] = jnp.where((tok >= 0) & (tok < t_total), cu, 0.0)
    z = (wconv_ref[0:1, :] * cu_ref[CONV_HALO - 1:CONV_HALO - 1 + tm, :]
         + wconv_ref[1:2, :] * cu_ref[CONV_HALO:CONV_HALO + tm, :]
         + wconv_ref[2:3, :] * cu_ref[CONV_HALO + 1:CONV_HALO + 1 + tm, :])
    b = _dot(ext_ref[CONV_HALO:CONV_HALO + tm, :], wb_ref[...])
    o_ref[...] = (b * z).astype(BF16)


def _conv(x, mod, w_b, w_cu, w_conv):
    t, d = x.shape
    tm = min(FFN_ROW_TILE, t)
    per = tm // CONV_HALO
    last = t // CONV_HALO - 1
    return pl.pallas_call(
        functools.partial(_conv_kernel, t),
        grid=(t // tm,),
        in_specs=[
            pl.BlockSpec((tm, d), lambda i: (i, 0)),
            pl.BlockSpec((CONV_HALO, d), lambda i: (jnp.maximum(i * per - 1, 0), 0)),
            pl.BlockSpec((CONV_HALO, d), lambda i: (jnp.minimum((i + 1) * per, last), 0)),
            _const_spec(mod.shape), _const_spec(w_b.shape), _const_spec(w_cu.shape), _const_spec(w_conv.shape),
        ],
        out_specs=pl.BlockSpec((tm, d), lambda i: (i, 0)),
        out_shape=jax.ShapeDtypeStruct((t, d), BF16),
        scratch_shapes=[pltpu.VMEM((tm + 2 * CONV_HALO, d), BF16), pltpu.VMEM((tm + 2 * CONV_HALO, d), F32)],
        compiler_params=_params(1),
        name="conv",
    )(x, x, x, mod, w_b, w_cu, w_conv)


def _head_rms(v, first_half):
    outs = []
    for c in range(v.shape[1] // LANES):
        blk = v[:, c * LANES:(c + 1) * LANES]
        sq = blk * blk
        tot = jnp.sum(sq, axis=-1, keepdims=True)
        lo = jnp.sum(jnp.where(first_half, sq, 0.0), axis=-1, keepdims=True)
        ms = jnp.where(first_half, lo, tot - lo) * (1.0 / NA_HEAD_DIM)
        outs.append(blk * lax.rsqrt(ms + EPS))
    return outs


def _na_proj_kernel(x_ref, mod_ref, w_ref, gq_ref, gk_ref, q_ref, k_ref, v_ref):
    d = x_ref.shape[1]
    h = _modulate(x_ref[...], mod_ref, 3).astype(BF16)
    first_half = lax.broadcasted_iota(jnp.int32, (1, LANES), 1) < NA_HEAD_DIM
    for c, blk in enumerate(_head_rms(_dot(h, w_ref[:, 0:d]), first_half)):
        q_ref[:, c * LANES:(c + 1) * LANES] = (blk * gq_ref[...]).astype(BF16)
    for c, blk in enumerate(_head_rms(_dot(h, w_ref[:, d:2 * d]), first_half)):
        k_ref[:, c * LANES:(c + 1) * LANES] = (blk * gk_ref[...]).astype(BF16)
    v_ref[...] = _dot(h, w_ref[:, 2 * d:3 * d]).astype(BF16)


def _na_proj(x, mod, w_qkv, g_q2, g_k2):
    t, d = x.shape
    tm = min(FFN_ROW_TILE, t)
    spec = pl.BlockSpec((tm, d), lambda i: (i, 0))
    return pl.pallas_call(
        _na_proj_kernel,
        grid=(t // tm,),
        in_specs=[spec, _const_spec(mod.shape), _const_spec(w_qkv.shape), _const_spec(g_q2.shape),
                  _const_spec(g_k2.shape)],
        out_specs=[spec, spec, spec],
        out_shape=[jax.ShapeDtypeStruct((t, d), BF16)] * 3,
        compiler_params=_params(1),
        name="na_proj",
    )(x, mod, w_qkv, g_q2, g_k2)


def _na_attn_kernel(bounded, q_ref, k0_ref, k1_ref, k2_ref, v0_ref, v1_ref, v2_ref, kx_ref, vx_ref, bias_ref,
                    xshift_ref, o_ref):
    nq = q_ref.shape[0]
    first_half = lax.broadcasted_iota(jnp.int32, (1, LANES), 1) < NA_HEAD_DIM
    n_pairs = NA_HEADS // 2

    def scores(p):
        cols = slice(p * LANES, (p + 1) * LANES)
        qp = q_ref[:, cols]
        zero = jnp.zeros_like(qp)
        qs = jnp.concatenate([jnp.where(first_half, qp, zero), jnp.where(first_half, zero, qp)], axis=0)
        parts = []
        for t, k_ref in enumerate((k0_ref, k1_ref, k2_ref)):
            lanes = slice(t * nq, (t + 1) * nq)
            bias = jnp.concatenate([bias_ref[2 * p, :, lanes], bias_ref[2 * p + 1, :, lanes]], axis=0)
            parts.append(lax.dot_general(qs, k_ref[:, cols], _NT, preferred_element_type=F32) + bias)
        parts.append(lax.dot_general(qs, kx_ref[:, cols], _NT, preferred_element_type=F32) + xshift_ref[...])
        return parts

    def finish(p, parts):
        cols = slice(p * LANES, (p + 1) * LANES)
        if not bounded:
            m = jnp.max(jnp.maximum(jnp.maximum(parts[0], parts[1]), jnp.maximum(parts[2], parts[3])),
                        axis=-1, keepdims=True)
        e_sum = None
        o = None
        for s, v_ref in zip(parts, (v0_ref, v1_ref, v2_ref, vx_ref)):
            e = jnp.exp2(s) if bounded else jnp.exp2(s - m)
            part_o = _dot(e.astype(BF16), v_ref[:, cols])
            e_sum = e if e_sum is None else e_sum + e
            o = part_o if o is None else o + part_o
        o = o * (1.0 / jnp.sum(e_sum, axis=-1, keepdims=True))
        o_ref[:, cols] = jnp.where(first_half, o[0:nq], o[nq:2 * nq]).astype(BF16)

    parts_next = scores(0)
    for p in range(n_pairs):
        parts = parts_next
        if p + 1 < n_pairs:
            parts_next = scores(p + 1)
        finish(p, parts)


def _na_attn(q, k, v, k_ctx, v_ctx, bias, xshift, bounded=False):
    s, d = q.shape
    chunk = NA_Q_ROWS * GRID_W
    n_steps = s // chunk

    def band(t):
        return pl.BlockSpec((chunk, d), lambda i: (jnp.clip(i - 1, 0, n_steps - 3) + t, 0))

    cur = pl.BlockSpec((chunk, d), lambda i: (i, 0))
    bias_spec = pl.BlockSpec((None,) + bias.shape[1:],
                             lambda i: (jnp.where(i == 0, 0, jnp.where(i == n_steps - 1, 2, 1)), 0, 0, 0),
                             pipeline_mode=pl.Buffered(1))
    return pl.pallas_call(
        functools.partial(_na_attn_kernel, bounded),
        grid=(n_steps,),
        in_specs=[cur, band(0), band(1), band(2), band(0), band(1), band(2), _const_spec(k_ctx.shape),
                  _const_spec(v_ctx.shape), bias_spec, _const_spec(xshift.shape)],
        out_specs=cur,
        out_shape=jax.ShapeDtypeStruct((s, d), BF16),
        compiler_params=_params(1),
        name="na_attn_bounded" if bounded else "na_attn",
    )(q, k, k, k, v, v, v, k_ctx, v_ctx, bias, xshift)


def _na_band_pattern(step, n_rows):
    n_steps = n_rows // NA_Q_ROWS
    band0 = NA_Q_ROWS * int(np.clip(step - 1, 0, n_steps - 3))
    r = step * NA_Q_ROWS + np.arange(NA_Q_ROWS)[:, None]
    key_row = band0 + np.arange(3 * NA_Q_ROWS)[None, :]
    win0 = np.clip(r - NA_ROWS // 2, 0, n_rows - NA_ROWS)
    valid = (key_row >= win0) & (key_row < win0 + NA_ROWS)
    return np.where(valid, key_row - r + (NA_ROWS - 1), 0), valid


def _na_bias_tables(rpb, n_rows, shift):
    n_steps = n_rows // NA_Q_ROWS
    assert n_steps >= 4
    patterns = [_na_band_pattern(t, n_rows) for t in range(n_steps)]
    for dr, valid in patterns[2:-1]:
        assert np.array_equal(dr, patterns[1][0]) and np.array_equal(valid, patterns[1][1])
    cols = np.arange(GRID_W)
    col_start = np.clip(cols - NA_COLS // 2, 0, GRID_W - NA_COLS)
    kc = np.arange(GRID_W)
    col_ok = (kc[None, :] >= col_start[:, None]) & (kc[None, :] < col_start[:, None] + NA_COLS)
    pad = GRID_W - NA_COLS
    padded = jnp.pad(rpb * LOG2_E - shift, ((0, 0), (0, 0), (pad, pad)))
    toeplitz = jnp.stack([padded[:, :, GRID_W - 1 - c:2 * GRID_W - 1 - c] for c in range(GRID_W)], axis=2)
    toeplitz = jnp.where(col_ok[None, None], toeplitz, NEG_BIG)
    n_band = 3 * NA_Q_ROWS
    lead = NA_Q_ROWS
    n_dr = toeplitz.shape[1]
    strip = jnp.transpose(toeplitz, (0, 2, 1, 3)).reshape(rpb.shape[0], GRID_W, n_dr * GRID_W)
    strip = jnp.pad(strip, ((0, 0), (0, 0), (lead * GRID_W, lead * GRID_W)), constant_values=NEG_BIG)
    plan = []
    for dr, valid in (patterns[0], patterns[1], patterns[-1]):
        for jq in range(NA_Q_ROWS):
            rows_ok = np.nonzero(valid[jq])[0]
            lo, hi = int(rows_ok[0]), int(rows_ok[-1]) + 1
            assert hi - lo == len(rows_ok)
            start = int(dr[jq, lo]) - lo + lead
            assert 0 <= start and start + n_band <= n_dr + 2 * lead
            assert all(dr[jq, i] == start - lead + i for i in range(lo, hi))
            plan.append((start, lo, hi))

    def assemble(strip_ref, out_ref):
        lanes = lax.broadcasted_iota(jnp.int32, (1, n_band * GRID_W), 1)
        for n, (start, lo, hi) in enumerate(plan):
            kind, jq = divmod(n, NA_Q_ROWS)
            window = strip_ref[:, start * GRID_W:(start + n_band) * GRID_W]
            row_ok = (lanes >= lo * GRID_W) & (lanes < hi * GRID_W)
            out_ref[kind, jq * GRID_W:(jq + 1) * GRID_W, :] = jnp.where(row_ok, window, NEG_BIG)

    n_heads = rpb.shape[0]
    return pl.pallas_call(
        assemble,
        grid=(n_heads,),
        in_specs=[pl.BlockSpec((None,) + strip.shape[1:], lambda h: (h, 0, 0))],
        out_specs=pl.BlockSpec((3, None, NA_Q_ROWS * GRID_W, n_band * GRID_W), lambda h: (0, h, 0, 0)),
        out_shape=jax.ShapeDtypeStruct((3, n_heads, NA_Q_ROWS * GRID_W, n_band * GRID_W), F32),
        compiler_params=_params(1),
        name="na_bias_tables",
    )(strip)


def _rope_tables(t):
    pos = jnp.arange(t)
    row = (pos // GRID_W).astype(F32)
    col = (pos % GRID_W).astype(F32)
    n = MLA_ROPE // 4
    freqs = ROPE_BASE ** (-jnp.arange(n, dtype=F32) / n)
    ang = jnp.concatenate([row[:, None] * freqs, col[:, None] * freqs], axis=-1)
    cos, sin = jnp.cos(ang), jnp.sin(ang)
    return jnp.concatenate([cos, cos, sin, sin], axis=-1)


_HALF_SPLIT = np.concatenate([np.arange(0, MLA_ROPE, 2), np.arange(1, MLA_ROPE, 2)])


def _rope_cols(w):
    hs = w[..., _HALF_SPLIT]
    return jnp.concatenate([hs, -hs[..., MLA_ROPE // 2:], hs[..., :MLA_ROPE // 2]], axis=-1)


def _rope_gain(g):
    hs = g[_HALF_SPLIT]
    return jnp.concatenate([hs, hs[MLA_ROPE // 2:], hs[:MLA_ROPE // 2]])


def _mla_weights(w_dq, g_dq, w_uq, w_dkv, g_dkv, w_uk, w_uv, g_qn, g_qr, g_kn, g_kr):
    w_uq_ext = jnp.concatenate([w_uq[..., :MLA_NOPE], _rope_cols(w_uq[..., MLA_NOPE:])], axis=-1)
    g_q = jnp.tile(jnp.concatenate([g_qn, _rope_gain(g_qr)]) * (MLA_SCALE * LOG2_E), MLA_HEADS)
    q_norm = jnp.sqrt(MLA_NOPE * jnp.max(g_qn * g_qn) + MLA_ROPE * jnp.max(g_qr * g_qr)) * (MLA_SCALE * LOG2_E)
    k_norm = jnp.sqrt(MLA_NOPE * jnp.max(g_kn * g_kn) + MLA_ROPE * jnp.max(g_kr * g_kr))
    bound = BOUND_SLACK * q_norm * k_norm
    return {
        "score_bound": bound,
        "q_padt": jnp.zeros((MLA_ROPE, LANES), F32).at[0, :].set(-bound),
        "w_dq": w_dq.astype(BF16),
        "g_dq": g_dq[None, :],
        "w_uqt": w_uq_ext.reshape(MLA_Q_LORA, -1).T.astype(BF16),
        "g_qt": jnp.broadcast_to(g_q[:, None], (g_q.shape[0], LANES)),
        "w_dkv": jnp.concatenate([w_dkv[:, :MLA_KV_LORA], _rope_cols(w_dkv[:, MLA_KV_LORA:])], axis=-1).astype(BF16),
        "g_dkv": g_dkv[None, :],
        "w_uk": w_uk.reshape(MLA_KV_LORA, -1).astype(BF16),
        "g_k": jnp.concatenate([g_kn, _rope_gain(g_kr)])[None, :],
        "w_uvt": w_uv.reshape(MLA_KV_LORA, -1).T.astype(BF16),
    }


def _block_diag(w):
    g, c, _ = w.shape
    out = jnp.zeros((g * c, g * c), w.dtype)
    for i in range(g):
        out = out.at[i * c:(i + 1) * c, i * c:(i + 1) * c].set(w[i])
    return out


def kernel(x, c, ctx, c_ctx, mod_w, mod_b, ffn_w_in, ffn_w_out, mla_w_dq, mla_g_dq, mla_w_uq, mla_w_dkv, mla_g_dkv, mla_w_uk, mla_w_uv, mla_g_qn, mla_g_qr, mla_g_kn, mla_g_kr, mla_w_o, pool_w, pool_scale, na_w_qkv, na_g_q, na_g_k, na_rpb, na_w_o, conv_w_in, conv_w, conv_w_out):
    assert x.shape[0] == 1 and x.shape[2] == D_MODEL and x.shape[1] % ROW_TILE == 0
    s = x.shape[1]
    d = D_MODEL
    xs = x[0]
    hc = ctx[0]
    n_ctx = hc.shape[0]

    cond = jnp.zeros((8, d), F32).at[0].set(c[0]).at[1].set(c_ctx)
    mods = _ada_params(cond, mod_w, mod_b)
    wts = {(0, 0): (ffn_w_in[0, 0].astype(BF16), ffn_w_out[0, 0].astype(BF16))}

    def ffn_x(xs, mx, layer, cast_next):
        cast = [(stack, key) for key in cast_next for stack in (ffn_w_in, ffn_w_out)]
        xs, slabs = _ffn(xs, mx, 0, *wts[(layer, 0)], cast=cast)
        for n, key in enumerate(cast_next):
            wts[key] = (slabs[2 * n], slabs[2 * n + 1])
        return xs

    mx = mods[0, 0].reshape(N_MOD, d)
    mc = mods[0, 1].reshape(N_MOD, d)
    xs = ffn_x(xs, mx, 0, [(0, 1), (1, 0)])
    hc = _ffn(hc, mc, 0, *wts[(0, 0)])[0]
    mw = _mla_weights(mla_w_dq[0], mla_g_dq[0], mla_w_uq[0], mla_w_dkv[0], mla_g_dkv[0], mla_w_uk[0], mla_w_uv[0],
                      mla_g_qn[0], mla_g_qr[0], mla_g_kn[0], mla_g_kr[0])
    no_rope = jnp.concatenate([jnp.ones((n_ctx, LANES // 2), F32), jnp.zeros((n_ctx, LANES // 2), F32)], axis=-1)
    q_x, k_x, vt_x = _mla_proj(xs, mx, _rope_tables(s), mw)
    q_c, k_c, vt_c = _mla_proj(hc, mc, no_rope, mw)
    o_x = lax.cond(2.0 * mw["score_bound"] <= SCORE_RANGE_LOG2,
                   lambda q, k, vt, ke, vte: _mla_attn(q, k, vt, extra=(ke, vte), bounded=True),
                   lambda q, k, vt, ke, vte: _mla_attn(q, k, vt, extra=(ke, vte)),
                   q_x, k_x, vt_x, k_c, vt_c)
    o_c = _mla_attn(q_c, k_c, vt_c)
    w_o = mla_w_o[0].astype(BF16)
    xs = _ffn(xs, mx, 6, *wts[(0, 1)], pro=(o_x, w_o, mx[5:6]))[0]
    hc = _ffn(hc, mc, 6, *wts[(0, 1)], pro=(o_c, w_o, mc[5:6]))[0]

    mx = mods[1, 0].reshape(N_MOD, d)
    mc = mods[1, 1].reshape(N_MOD, d)
    xs = ffn_x(xs, mx, 1, [(1, 1), (2, 0)])
    hc = _ffn(hc, mc, 0, *wts[(1, 0)])[0]
    w_p = _block_diag(pool_w[0]).astype(BF16)
    xs = _ffn(xs, mx, 6, *wts[(1, 1)], pro=(_pool(xs, mx), w_p, mx[5:6] * pool_scale[0][None, :]))[0]
    hc = _ffn(hc, mc, 6, *wts[(1, 1)], pro=(_pool(hc, mc), w_p, mc[5:6] * pool_scale[0][None, :]))[0]

    mx = mods[2, 0].reshape(N_MOD, d)
    mc = mods[2, 1].reshape(N_MOD, d)
    xs = ffn_x(xs, mx, 2, [(2, 1), (3, 0)])
    hc = _ffn(hc, mc, 0, *wts[(2, 0)])[0]
    w_qkv = na_w_qkv[0].astype(BF16)
    g_q2 = jnp.tile(na_g_q[0] * (NA_SCALE * LOG2_E), 2)[None, :]
    g_k2 = jnp.tile(na_g_k[0], 2)[None, :]
    q_n, k_n, v_n = _na_proj(xs, mx, w_qkv, g_q2, g_k2)
    _, k_nc, v_nc = _na_proj(hc, mc, w_qkv, g_q2, g_k2)
    qk_bound = BOUND_SLACK * (NA_SCALE * LOG2_E) * NA_HEAD_DIM * jnp.sqrt(
        jnp.max(na_g_q[0] * na_g_q[0]) * jnp.max(na_g_k[0] * na_g_k[0]))
    shift = qk_bound + jnp.maximum(jnp.max(na_rpb[0]) * LOG2_E, 0.0)
    bias = _na_bias_tables(na_rpb[0], s // GRID_W, shift)
    xshift = jnp.full((1, n_ctx), -shift, F32)
    o_n = lax.cond(qk_bound + shift <= SCORE_RANGE_LOG2,
                   lambda *a: _na_attn(*a, bounded=True), lambda *a: _na_attn(*a),
                   q_n, k_n, v_n, k_nc, v_nc, bias, xshift)
    xs = _ffn(xs, mx, 6, *wts[(2, 1)], pro=(o_n, na_w_o[0].astype(BF16), mx[5:6]))[0]

    mx = mods[3, 0].reshape(N_MOD, d)
    xs = ffn_x(xs, mx, 3, [(3, 1)])
    w_ci = conv_w_in[0].astype(BF16)
    y_c = _conv(xs, mx, w_ci[:, :d], w_ci[:, d:], conv_w[0])
    xs = _ffn(xs, mx, 6, *wts[(3, 1)], pro=(y_c, conv_w_out[0].astype(BF16), mx[5:6]))[0]
    return xs[None]
```

```python
import functools

import jax
import jax.numpy as jnp
import numpy as np
from jax import lax
from jax.experimental import pallas as pl
from jax.experimental.pallas import tpu as pltpu

F32 = jnp.float32
BF16 = jnp.bfloat16

D_MODEL = 1024
DEPTH = 4
GRID_W = 64
N_MOD = 9
D_FF = 2816
EPS = 1e-6
MLA_HEADS = 8
MLA_Q_LORA = 384
MLA_KV_LORA = 256
MLA_NOPE = 128
MLA_ROPE = 64
MLA_V = 128
MLA_SCALE = (MLA_NOPE + MLA_ROPE) ** -0.5
ROPE_BASE = 10000.0
POOL_WINDOWS = (2, 4, 8, 16)
POOL_GROUP = D_MODEL // len(POOL_WINDOWS)
NA_HEADS = 16
NA_HEAD_DIM = D_MODEL // NA_HEADS
NA_ROWS = 8
NA_COLS = 16
NA_SCALE = NA_HEAD_DIM ** -0.5

LANES = 128
MXU_COLS = 256
VMEM_LIMIT = 56 * 1024 * 1024
ROW_TILE = 512
FFN_ROW_TILE = 1024
MLA_Q_TILE = 2048
FF_CHUNK = MXU_COLS
POOL_HALO = 8
CONV_HALO = 16
NA_Q_ROWS = 4
NEG_BIG = -1e30
LOG2_E = 1.4426950408889634
SCORE_RANGE_LOG2 = 60.0
BOUND_SLACK = 1.01

_NT = (((1,), (1,)), ((), ()))


def _params(n_axes, flags=None):
    return pltpu.CompilerParams(dimension_semantics=("arbitrary",) * n_axes, vmem_limit_bytes=VMEM_LIMIT, flags=flags)


def _const_spec(shape):
    nd = len(shape)
    return pl.BlockSpec(shape, lambda *_: (0,) * nd, pipeline_mode=pl.Buffered(1))


def _rms(x):
    return x * lax.rsqrt(jnp.mean(x * x, axis=-1, keepdims=True) + EPS)


def _modulate(x, mod_ref, row):
    shift = mod_ref[row:row + 1, :]
    scale = mod_ref[row + 1:row + 2, :]
    return _rms(x) * (1.0 + scale) + shift


def _dot(a, b):
    return jnp.dot(a, b, preferred_element_type=F32)


def _ada_kernel(cond_ref, w_ref, b_ref, o_ref):
    cnd = cond_ref[...]
    s = (cnd * (1.0 / (1.0 + jnp.exp(-cnd)))).astype(BF16)
    o_ref[...] = _dot(s, w_ref[...].astype(BF16)) + b_ref[...]


def _ada_params(cond, mod_w, mod_b):
    depth, d, n = mod_w.shape
    tn = n // 8
    return pl.pallas_call(
        _ada_kernel,
        grid=(depth, n // tn),
        in_specs=[
            pl.BlockSpec((8, d), lambda i, j: (0, 0)),
            pl.BlockSpec((None, d, tn), lambda i, j: (i, 0, j)),
            pl.BlockSpec((None, 1, tn), lambda i, j: (i, 0, j)),
        ],
        out_specs=pl.BlockSpec((None, 8, tn), lambda i, j: (i, 0, j)),
        out_shape=jax.ShapeDtypeStruct((depth, 8, n), F32),
        compiler_params=_params(2),
        name="ada_params",
    )(cond, mod_w, mod_b.reshape(depth, 1, n))


def _ffn_kernel(has_pro, n_cast, mod_row, *refs):
    refs = list(refs)
    x_ref = refs.pop(0)
    if has_pro:
        y_ref, wp_ref, gp_ref = refs[:3]
        refs = refs[3:]
    mod_ref, wi_ref, wo_ref = refs[:3]
    cast_src = refs[3:3 + n_cast]
    o_ref = refs[3 + n_cast]
    cast_dst = refs[4 + n_cast:4 + 2 * n_cast]
    a_ref = refs[4 + 2 * n_cast]
    x = x_ref[...]
    if has_pro:
        x = x + gp_ref[...] * _dot(y_ref[...], wp_ref[...])
    h = _modulate(x, mod_ref, mod_row).astype(BF16)
    for j in range(D_FF // FF_CHUNK):
        g = _dot(h, wi_ref[:, j * FF_CHUNK:(j + 1) * FF_CHUNK])
        u = _dot(h, wi_ref[:, D_FF + j * FF_CHUNK:D_FF + (j + 1) * FF_CHUNK])
        a_ref[:, j * FF_CHUNK:(j + 1) * FF_CHUNK] = (g * (1.0 / (1.0 + jnp.exp(-g))) * u).astype(BF16)
    gate = mod_ref[mod_row + 2:mod_row + 3, :]
    o_ref[...] = x + (0.5 * gate) * _dot(a_ref[...], wo_ref[...])
    for src, dst in zip(cast_src, cast_dst):
        dst[...] = src[...].astype(BF16)


def _ffn(x, mod, mod_row, w_in, w_out, pro=None, cast=()):
    t, d = x.shape
    tm = min(FFN_ROW_TILE, t)
    n_steps = t // tm
    row = lambda i: (i, 0)
    in_specs = [pl.BlockSpec((tm, d), row)]
    args = [x]
    if pro is not None:
        y, w_p, g_p = pro
        in_specs += [pl.BlockSpec((tm, d), row), _const_spec(w_p.shape), _const_spec(g_p.shape)]
        args += [y, w_p, g_p]
    in_specs += [_const_spec(mod.shape), _const_spec(w_in.shape), _const_spec(w_out.shape)]
    args += [mod, w_in, w_out]
    out_specs = [pl.BlockSpec((tm, d), row)]
    out_shape = [jax.ShapeDtypeStruct((t, d), F32)]
    for stack, lead in cast:
        rows, cols = stack.shape[-2:]
        assert rows % (16 * n_steps) == 0
        blk = rows // n_steps
        in_specs.append(pl.BlockSpec((None,) * len(lead) + (blk, cols), lambda i, lead=lead: tuple(lead) + (i, 0)))
        args.append(stack)
        out_specs.append(pl.BlockSpec((blk, cols), row))
        out_shape.append(jax.ShapeDtypeStruct((rows, cols), BF16))
    outs = pl.pallas_call(
        functools.partial(_ffn_kernel, pro is not None, len(cast), mod_row),
        grid=(n_steps,),
        in_specs=in_specs,
        out_specs=out_specs,
        out_shape=out_shape,
        scratch_shapes=[pltpu.VMEM((tm, D_FF), BF16)],
        compiler_params=_params(1),
        name="ffn_pro" if pro is not None else "ffn",
    )(*args)
    return outs[0], list(outs[1:])


def _rope_pair(r2, gain, cs, first_half):
    ms = jnp.sum(jnp.where(first_half, r2 * r2, 0.0), axis=-1, keepdims=True) * (1.0 / MLA_ROPE)
    t = r2 * lax.rsqrt(ms + EPS) * gain * cs
    return t + pltpu.roll(t, MLA_ROPE, axis=1)


def _mla_proj_kernel(x_ref, mod_ref, cs_ref, cst_ref, wdq_ref, gdq_ref, wuqt_ref, gqt_ref, qpadt_ref, wdkv_ref,
                     gdkv_ref, wuk_ref, gk_ref, wuvt_ref, qt_ref, k_ref, vt_ref):
    h = _modulate(x_ref[...], mod_ref, 3).astype(BF16)
    cs = cs_ref[...]
    lane = lax.broadcasted_iota(jnp.int32, (1, LANES), 1)
    first_half = lane < MLA_ROPE
    tm = x_ref.shape[0]

    def lanes_of(tab_ref, rows):
        return jnp.concatenate([tab_ref[rows, :]] * (tm // LANES), axis=1)

    cq = (_rms(_dot(h, wdq_ref[...])) * gdq_ref[...]).astype(BF16)
    qt = lax.dot_general(wuqt_ref[...], cq, _NT, preferred_element_type=F32)
    cst = cst_ref[...]
    for hd in range(MLA_HEADS):
        r0 = hd * 2 * LANES
        nope = qt[r0:r0 + MLA_NOPE, :]
        inv = lax.rsqrt(jnp.mean(nope * nope, axis=0, keepdims=True) + EPS)
        qt_ref[hd, 0:MLA_NOPE, :] = (nope * inv * lanes_of(gqt_ref, slice(r0, r0 + MLA_NOPE))).astype(BF16)
        r1 = r0 + MLA_NOPE
        rope = qt[r1:r1 + MLA_ROPE, :]
        swap = qt[r1 + MLA_ROPE:r1 + 2 * MLA_ROPE, :]
        inv = lax.rsqrt(jnp.mean(rope * rope, axis=0, keepdims=True) + EPS)
        rot = (rope * lanes_of(gqt_ref, slice(r1, r1 + MLA_ROPE)) * cst[0:MLA_ROPE, :]
               + swap * lanes_of(gqt_ref, slice(r1 + MLA_ROPE, r1 + 2 * MLA_ROPE)) * cst[MLA_ROPE:, :]) * inv
        qt_ref[hd, MLA_NOPE:MLA_NOPE + MLA_ROPE, :] = rot.astype(BF16)
        qt_ref[hd, MLA_NOPE + MLA_ROPE:, :] = lanes_of(qpadt_ref, slice(None)).astype(BF16)
    kv = _dot(h, wdkv_ref[...])
    ckv = (_rms(kv[:, :MLA_KV_LORA]) * gdkv_ref[...]).astype(BF16)
    rot = _rope_pair(kv[:, MLA_KV_LORA:], gk_ref[:, LANES:2 * LANES], cs, first_half)
    kr = jnp.where(first_half, rot, jnp.where(lane == MLA_ROPE, 1.0, 0.0)).astype(BF16)
    kn = _dot(ckv, wuk_ref[...])
    vt = lax.dot_general(wuvt_ref[...], ckv, _NT, preferred_element_type=F32)
    for hd in range(MLA_HEADS):
        blk = kn[:, hd * LANES:(hd + 1) * LANES]
        k_ref[hd, :, 0:LANES] = (_rms(blk) * gk_ref[:, 0:LANES]).astype(BF16)
        k_ref[hd, :, LANES:2 * LANES] = kr
        vt_ref[hd] = vt[hd * MLA_V:(hd + 1) * MLA_V, :].astype(BF16)


def _mla_proj(x, mod, cs, w):
    t, d = x.shape
    tm = min(ROW_TILE, t)
    consts = [w["w_dq"], w["g_dq"], w["w_uqt"], w["g_qt"], w["q_padt"], w["w_dkv"], w["g_dkv"], w["w_uk"], w["g_k"], w["w_uvt"]]
    return pl.pallas_call(
        _mla_proj_kernel,
        grid=(t // tm,),
        in_specs=[pl.BlockSpec((tm, d), lambda i: (i, 0)), _const_spec(mod.shape),
                  pl.BlockSpec((tm, LANES), lambda i: (i, 0)), pl.BlockSpec((LANES, tm), lambda i: (0, i))]
        + [_const_spec(a.shape) for a in consts],
        out_specs=[
            pl.BlockSpec((MLA_HEADS, 2 * LANES, tm), lambda i: (0, 0, i)),
            pl.BlockSpec((MLA_HEADS, tm, 2 * LANES), lambda i: (0, i, 0)),
            pl.BlockSpec((MLA_HEADS, None, MLA_V, tm), lambda i: (0, i, 0, 0)),
        ],
        out_shape=[
            jax.ShapeDtypeStruct((MLA_HEADS, 2 * LANES, t), BF16),
            jax.ShapeDtypeStruct((MLA_HEADS, t, 2 * LANES), BF16),
            jax.ShapeDtypeStruct((MLA_HEADS, t // tm, MLA_V, tm), BF16),
        ],
        compiler_params=_params(1),
        name="mla_proj",
    )(x, mod, cs, cs.T, *consts)


def _mla_attn_kernel(n_main, sub, tv, has_extra, *refs):
    if has_extra:
        q_ref, k_ref, vt_ref, ke_ref, vte_ref, o_ref, acc_ref, s_ref = refs
    else:
        q_ref, k_ref, vt_ref, o_ref, acc_ref, s_ref = refs
    qt = q_ref[...]
    tq = qt.shape[1]
    tk = sub * tv
    acc_ref[...] = jnp.zeros_like(acc_ref)

    def scores(j):
        k = k_ref[pl.ds(pl.multiple_of(j * tk, tk), tk), :]
        return _dot(k, qt)

    def main_vts(j):
        return [vt_ref[j * sub + c] for c in range(sub)]

    def produce(slot, j):
        sc = scores(j)
        s_ref[slot] = sc
        return jnp.max(sc, axis=0, keepdims=True)

    def update(carry, s, s_max, vts):
        m, l = carry
        m_new = jnp.maximum(m, s_max)
        alpha = jnp.exp2(m - m_new)
        p = jnp.exp2(s - m_new)
        l = alpha * l + jnp.sum(p, axis=0, keepdims=True)
        pb = p.astype(BF16)
        pv = _dot(vts[0], pb[0:vts[0].shape[1], :])
        for c in range(1, len(vts)):
            pv = pv + _dot(vts[c], pb[c * tv:(c + 1) * tv, :])
        acc_ref[...] = alpha * acc_ref[...] + pv
        return m_new, l

    carry = (jnp.full((1, tq), NEG_BIG, F32), jnp.zeros((1, tq), F32))
    max0 = produce(0, 0)
    if n_main > 1:
        assert n_main % 2 == 0

        def body(jj, state):
            carry, max0 = state
            j = 2 * jj
            max1 = produce(1, j + 1)
            carry = update(carry, s_ref[0], max0, main_vts(j))
            max0 = produce(0, j + 2)
            return update(carry, s_ref[1], max1, main_vts(j + 1)), max0

        carry, max0 = lax.fori_loop(0, n_main // 2 - 1, body, (carry, max0))
        max1 = produce(1, n_main - 1)
        carry = update(carry, s_ref[0], max0, main_vts(n_main - 2))
        carry = update(carry, s_ref[1], max1, main_vts(n_main - 1))
    else:
        carry = update(carry, s_ref[0], max0, main_vts(0))
    if has_extra:
        s_e = _dot(ke_ref[...], qt)
        carry = update(carry, s_e, jnp.max(s_e, axis=0, keepdims=True), [vte_ref[...]])
    o_ref[...] = (acc_ref[...] * (1.0 / carry[1])).T.astype(BF16)


def _mla_attn_bounded_kernel(n_main, sub, tv, has_extra, *refs):
    if has_extra:
        q_ref, k_ref, vt_ref, ke_ref, vte_ref, o_ref, acc_ref, l_ref = refs
    else:
        q_ref, k_ref, vt_ref, o_ref, acc_ref, l_ref = refs
    qt = q_ref[...]
    tk = sub * tv
    acc_ref[...] = jnp.zeros_like(acc_ref)
    l_ref[...] = jnp.zeros_like(l_ref)

    def accumulate(k, vts):
        p = jnp.exp2(_dot(k, qt))
        l_ref[...] += jnp.sum(p, axis=0, keepdims=True)
        pb = p.astype(BF16)
        pv = _dot(vts[0], pb[0:vts[0].shape[1], :])
        for c in range(1, len(vts)):
            pv = pv + _dot(vts[c], pb[c * tv:(c + 1) * tv, :])
        acc_ref[...] += pv

    def body(j, _):
        k = k_ref[pl.ds(pl.multiple_of(j * tk, tk), tk), :]
        accumulate(k, [vt_ref[j * sub + c] for c in range(sub)])
        return 0

    lax.fori_loop(0, n_main, body, 0, unroll=True)
    if has_extra:
        accumulate(ke_ref[...], [vte_ref[...]])
    o_ref[...] = (acc_ref[...] * (1.0 / l_ref[...])).T.astype(BF16)


def _mla_attn(q, k, vt, extra=None, bounded=False):
    nh, dk, tq_all = q.shape
    tk_all = k.shape[1]
    n_v, tv = vt.shape[1], vt.shape[3]
    sub = 2 if n_v % 2 == 0 else 1
    tq = min(MLA_Q_TILE, tq_all)
    kv_mode = None if bounded else pl.Buffered(1)
    in_specs = [
        pl.BlockSpec((None, dk, tq), lambda h, i: (h, 0, i)),
        pl.BlockSpec((None, tk_all, dk), lambda h, i: (h, 0, 0), pipeline_mode=kv_mode),
        pl.BlockSpec((None, n_v, MLA_V, tv), lambda h, i: (h, 0, 0, 0), pipeline_mode=kv_mode),
    ]
    if bounded:
        body = functools.partial(_mla_attn_bounded_kernel, n_v // sub, sub, tv, extra is not None)
        scratch = [pltpu.VMEM((MLA_V, tq), F32), pltpu.VMEM((1, tq), F32)]
    else:
        body = functools.partial(_mla_attn_kernel, n_v // sub, sub, tv, extra is not None)
        scratch = [pltpu.VMEM((MLA_V, tq), F32), pltpu.VMEM((2, sub * tv, tq), F32)]
    args = [q, k, vt]
    if extra is not None:
        k_e, vt_e = extra
        in_specs += [pl.BlockSpec((None,) + k_e.shape[1:], lambda h, i: (h, 0, 0)),
                     pl.BlockSpec((None, None) + vt_e.shape[2:], lambda h, i: (h, 0, 0, 0))]
        args += [k_e, vt_e]
    return pl.pallas_call(
        body,
        grid=(nh, tq_all // tq),
        in_specs=in_specs,
        out_specs=pl.BlockSpec((tq, MLA_V), lambda h, i: (i, h)),
        out_shape=jax.ShapeDtypeStruct((tq_all, nh * MLA_V), BF16),
        scratch_shapes=scratch,
        compiler_params=_params(2),
        name="mla_attn_bounded" if bounded else "mla_attn",
    )(*args)


def _pool_kernel(t_total, x_ref, xp_ref, xn_ref, mod_ref, o_ref, ext_ref, lvl_a_ref, lvl_b_ref):
    tm = x_ref.shape[0]
    base = pl.program_id(0) * tm
    hc = _modulate(x_ref[...], mod_ref, 3)
    halo_rows = lax.broadcasted_iota(jnp.int32, (POOL_HALO, 1), 0)
    hp = jnp.where(base - POOL_HALO + halo_rows >= 0, _modulate(xp_ref[...], mod_ref, 3), 0.0)
    hn = jnp.where(base + tm + halo_rows < t_total, _modulate(xn_ref[...], mod_ref, 3), 0.0)
    ext_ref[0:POOL_HALO, :] = hp
    ext_ref[POOL_HALO:POOL_HALO + tm, :] = hc
    ext_ref[POOL_HALO + tm:, :] = hn

    tok = base + lax.broadcasted_iota(jnp.int32, (tm, 1), 0)
    for g, win in enumerate(POOL_WINDOWS):
        half = win // 2
        cols = slice(g * POOL_GROUP, (g + 1) * POOL_GROUP)
        cur, length, step, nxt = ext_ref, tm + 2 * POOL_HALO, 1, 0
        while step < half:
            length -= step
            dst = (lvl_a_ref, lvl_b_ref)[nxt]
            dst[0:length, cols] = cur[0:length, cols] + cur[step:step + length, cols]
            cur, step, nxt = dst, 2 * step, 1 - nxt
        win_sum = cur[POOL_HALO - half:POOL_HALO - half + tm, cols] + cur[POOL_HALO:POOL_HALO + tm, cols]
        cnt = (jnp.minimum(tok + half, t_total) - jnp.maximum(tok - half, 0)).astype(F32)
        o_ref[:, cols] = (win_sum / cnt - ext_ref[POOL_HALO:POOL_HALO + tm, cols]).astype(BF16)


def _pool(x, mod):
    t, d = x.shape
    tm = min(ROW_TILE, t)
    per = tm // POOL_HALO
    last = t // POOL_HALO - 1
    return pl.pallas_call(
        functools.partial(_pool_kernel, t),
        grid=(t // tm,),
        in_specs=[
            pl.BlockSpec((tm, d), lambda i: (i, 0)),
            pl.BlockSpec((POOL_HALO, d), lambda i: (jnp.maximum(i * per - 1, 0), 0)),
            pl.BlockSpec((POOL_HALO, d), lambda i: (jnp.minimum((i + 1) * per, last), 0)),
            _const_spec(mod.shape),
        ],
        out_specs=pl.BlockSpec((tm, d), lambda i: (i, 0)),
        out_shape=jax.ShapeDtypeStruct((t, d), BF16),
        scratch_shapes=[pltpu.VMEM((tm + 2 * POOL_HALO, d), F32)] * 3,
        compiler_params=_params(1),
        name="pool",
    )(x, x, x, mod)


def _conv_kernel(t_total, x_ref, xp_ref, xn_ref, mod_ref, wb_ref, wcu_ref, wconv_ref, o_ref, ext_ref, cu_ref):
    tm, d = x_ref.shape
    base = pl.program_id(0) * tm
    ext_ref[0:CONV_HALO, :] = _modulate(xp_ref[...], mod_ref, 3).astype(BF16)
    ext_ref[CONV_HALO:CONV_HALO + tm, :] = _modulate(x_ref[...], mod_ref, 3).astype(BF16)
    ext_ref[CONV_HALO + tm:, :] = _modulate(xn_ref[...], mod_ref, 3).astype(BF16)
    ext = ext_ref[...]
    cu = _dot(ext, wcu_ref[:, 0:d]) * _dot(ext, wcu_ref[:, d:2 * d])
    tok = base - CONV_HALO + lax.broadcasted_iota(jnp.int32, (tm + 2 * CONV_HALO, 1), 0)
    cu_ref[...] = jnp.where((tok >= 0) & (tok < t_total), cu, 0.0)
    z = (wconv_ref[0:1, :] * cu_ref[CONV_HALO - 1:CONV_HALO - 1 + tm, :]
         + wconv_ref[1:2, :] * cu_ref[CONV_HALO:CONV_HALO + tm, :]
         + wconv_ref[2:3, :] * cu_ref[CONV_HALO + 1:CONV_HALO + 1 + tm, :])
    b = _dot(ext_ref[CONV_HALO:CONV_HALO + tm, :], wb_ref[...])
    o_ref[...] = (b * z).astype(BF16)


def _conv(x, mod, w_b, w_cu, w_conv):
    t, d = x.shape
    tm = min(FFN_ROW_TILE, t)
    per = tm // CONV_HALO
    last = t // CONV_HALO - 1
    return pl.pallas_call(
        functools.partial(_conv_kernel, t),
        grid=(t // tm,),
        in_specs=[
            pl.BlockSpec((tm, d), lambda i: (i, 0)),
            pl.BlockSpec((CONV_HALO, d), lambda i: (jnp.maximum(i * per - 1, 0), 0)),
            pl.BlockSpec((CONV_HALO, d), lambda i: (jnp.minimum((i + 1) * per, last), 0)),
            _const_spec(mod.shape), _const_spec(w_b.shape), _const_spec(w_cu.shape), _const_spec(w_conv.shape),
        ],
        out_specs=pl.BlockSpec((tm, d), lambda i: (i, 0)),
        out_shape=jax.ShapeDtypeStruct((t, d), BF16),
        scratch_shapes=[pltpu.VMEM((tm + 2 * CONV_HALO, d), BF16), pltpu.VMEM((tm + 2 * CONV_HALO, d), F32)],
        compiler_params=_params(1),
        name="conv",
    )(x, x, x, mod, w_b, w_cu, w_conv)


def _head_rms(v, first_half):
    outs = []
    for c in range(v.shape[1] // LANES):
        blk = v[:, c * LANES:(c + 1) * LANES]
        sq = blk * blk
        tot = jnp.sum(sq, axis=-1, keepdims=True)
        lo = jnp.sum(jnp.where(first_half, sq, 0.0), axis=-1, keepdims=True)
        ms = jnp.where(first_half, lo, tot - lo) * (1.0 / NA_HEAD_DIM)
        outs.append(blk * lax.rsqrt(ms + EPS))
    return outs


def _na_proj_kernel(x_ref, mod_ref, w_ref, gq_ref, gk_ref, q_ref, k_ref, v_ref):
    d = x_ref.shape[1]
    h = _modulate(x_ref[...], mod_ref, 3).astype(BF16)
    first_half = lax.broadcasted_iota(jnp.int32, (1, LANES), 1) < NA_HEAD_DIM
    for c, blk in enumerate(_head_rms(_dot(h, w_ref[:, 0:d]), first_half)):
        q_ref[:, c * LANES:(c + 1) * LANES] = (blk * gq_ref[...]).astype(BF16)
    for c, blk in enumerate(_head_rms(_dot(h, w_ref[:, d:2 * d]), first_half)):
        k_ref[:, c * LANES:(c + 1) * LANES] = (blk * gk_ref[...]).astype(BF16)
    v_ref[...] = _dot(h, w_ref[:, 2 * d:3 * d]).astype(BF16)


def _na_proj(x, mod, w_qkv, g_q2, g_k2):
    t, d = x.shape
    tm = min(FFN_ROW_TILE, t)
    spec = pl.BlockSpec((tm, d), lambda i: (i, 0))
    return pl.pallas_call(
        _na_proj_kernel,
        grid=(t // tm,),
        in_specs=[spec, _const_spec(mod.shape), _const_spec(w_qkv.shape), _const_spec(g_q2.shape),
                  _const_spec(g_k2.shape)],
        out_specs=[spec, spec, spec],
        out_shape=[jax.ShapeDtypeStruct((t, d), BF16)] * 3,
        compiler_params=_params(1),
        name="na_proj",
    )(x, mod, w_qkv, g_q2, g_k2)


def _na_attn_kernel(bounded, q_ref, k0_ref, k1_ref, k2_ref, v0_ref, v1_ref, v2_ref, kx_ref, vx_ref, bias_ref,
                    xshift_ref, o_ref):
    nq = q_ref.shape[0]
    first_half = lax.broadcasted_iota(jnp.int32, (1, LANES), 1) < NA_HEAD_DIM
    n_pairs = NA_HEADS // 2

    def scores(p):
        cols = slice(p * LANES, (p + 1) * LANES)
        qp = q_ref[:, cols]
        zero = jnp.zeros_like(qp)
        qs = jnp.concatenate([jnp.where(first_half, qp, zero), jnp.where(first_half, zero, qp)], axis=0)
        parts = []
        for t, k_ref in enumerate((k0_ref, k1_ref, k2_ref)):
            lanes = slice(t * nq, (t + 1) * nq)
            bias = jnp.concatenate([bias_ref[2 * p, :, lanes], bias_ref[2 * p + 1, :, lanes]], axis=0)
            parts.append(lax.dot_general(qs, k_ref[:, cols], _NT, preferred_element_type=F32) + bias)
        parts.append(lax.dot_general(qs, kx_ref[:, cols], _NT, preferred_element_type=F32) + xshift_ref[...])
        return parts

    def finish(p, parts):
        cols = slice(p * LANES, (p + 1) * LANES)
        if not bounded:
            m = jnp.max(jnp.maximum(jnp.maximum(parts[0], parts[1]), jnp.maximum(parts[2], parts[3])),
                        axis=-1, keepdims=True)
        e_sum = None
        o = None
        for s, v_ref in zip(parts, (v0_ref, v1_ref, v2_ref, vx_ref)):
            e = jnp.exp2(s) if bounded else jnp.exp2(s - m)
            part_o = _dot(e.astype(BF16), v_ref[:, cols])
            e_sum = e if e_sum is None else e_sum + e
            o = part_o if o is None else o + part_o
        o = o * (1.0 / jnp.sum(e_sum, axis=-1, keepdims=True))
        o_ref[:, cols] = jnp.where(first_half, o[0:nq], o[nq:2 * nq]).astype(BF16)

    parts_next = scores(0)
    for p in range(n_pairs):
        parts = parts_next
        if p + 1 < n_pairs:
            parts_next = scores(p + 1)
        finish(p, parts)


def _na_attn(q, k, v, k_ctx, v_ctx, bias, xshift, bounded=False):
    s, d = q.shape
    chunk = NA_Q_ROWS * GRID_W
    n_steps = s // chunk

    def band(t):
        return pl.BlockSpec((chunk, d), lambda i: (jnp.clip(i - 1, 0, n_steps - 3) + t, 0))

    cur = pl.BlockSpec((chunk, d), lambda i: (i, 0))
    bias_spec = pl.BlockSpec((None,) + bias.shape[1:],
                             lambda i: (jnp.where(i == 0, 0, jnp.where(i == n_steps - 1, 2, 1)), 0, 0, 0),
                             pipeline_mode=pl.Buffered(1))
    return pl.pallas_call(
        functools.partial(_na_attn_kernel, bounded),
        grid=(n_steps,),
        in_specs=[cur, band(0), band(1), band(2), band(0), band(1), band(2), _const_spec(k_ctx.shape),
                  _const_spec(v_ctx.shape), bias_spec, _const_spec(xshift.shape)],
        out_specs=cur,
        out_shape=jax.ShapeDtypeStruct((s, d), BF16),
        compiler_params=_params(1),
        name="na_attn_bounded" if bounded else "na_attn",
    )(q, k, k, k, v, v, v, k_ctx, v_ctx, bias, xshift)


def _na_band_pattern(step, n_rows):
    n_steps = n_rows // NA_Q_ROWS
    band0 = NA_Q_ROWS * int(np.clip(step - 1, 0, n_steps - 3))
    r = step * NA_Q_ROWS + np.arange(NA_Q_ROWS)[:, None]
    key_row = band0 + np.arange(3 * NA_Q_ROWS)[None, :]
    win0 = np.clip(r - NA_ROWS // 2, 0, n_rows - NA_ROWS)
    valid = (key_row >= win0) & (key_row < win0 + NA_ROWS)
    return np.where(valid, key_row - r + (NA_ROWS - 1), 0), valid


def _na_bias_tables(rpb, n_rows, shift):
    n_steps = n_rows // NA_Q_ROWS
    assert n_steps >= 4
    patterns = [_na_band_pattern(t, n_rows) for t in range(n_steps)]
    for dr, valid in patterns[2:-1]:
        assert np.array_equal(dr, patterns[1][0]) and np.array_equal(valid, patterns[1][1])
    cols = np.arange(GRID_W)
    col_start = np.clip(cols - NA_COLS // 2, 0, GRID_W - NA_COLS)
    kc = np.arange(GRID_W)
    col_ok = (kc[None, :] >= col_start[:, None]) & (kc[None, :] < col_start[:, None] + NA_COLS)
    pad = GRID_W - NA_COLS
    padded = jnp.pad(rpb * LOG2_E - shift, ((0, 0), (0, 0), (pad, pad)))
    toeplitz = jnp.stack([padded[:, :, GRID_W - 1 - c:2 * GRID_W - 1 - c] for c in range(GRID_W)], axis=2)
    toeplitz = jnp.where(col_ok[None, None], toeplitz, NEG_BIG)
    n_band = 3 * NA_Q_ROWS
    lead = NA_Q_ROWS
    n_dr = toeplitz.shape[1]
    strip = jnp.transpose(toeplitz, (0, 2, 1, 3)).reshape(rpb.shape[0], GRID_W, n_dr * GRID_W)
    strip = jnp.pad(strip, ((0, 0), (0, 0), (lead * GRID_W, lead * GRID_W)), constant_values=NEG_BIG)
    plan = []
    for dr, valid in (patterns[0], patterns[1], patterns[-1]):
        for jq in range(NA_Q_ROWS):
            rows_ok = np.nonzero(valid[jq])[0]
            lo, hi = int(rows_ok[0]), int(rows_ok[-1]) + 1
            assert hi - lo == len(rows_ok)
            start = int(dr[jq, lo]) - lo + lead
            assert 0 <= start and start + n_band <= n_dr + 2 * lead
            assert all(dr[jq, i] == start - lead + i for i in range(lo, hi))
            plan.append((start, lo, hi))

    def assemble(strip_ref, out_ref):
        lanes = lax.broadcasted_iota(jnp.int32, (1, n_band * GRID_W), 1)
        for n, (start, lo, hi) in enumerate(plan):
            kind, jq = divmod(n, NA_Q_ROWS)
            window = strip_ref[:, start * GRID_W:(start + n_band) * GRID_W]
            row_ok = (lanes >= lo * GRID_W) & (lanes < hi * GRID_W)
            out_ref[kind, jq * GRID_W:(jq + 1) * GRID_W, :] = jnp.where(row_ok, window, NEG_BIG)

    n_heads = rpb.shape[0]
    return pl.pallas_call(
        assemble,
        grid=(n_heads,),
        in_specs=[pl.BlockSpec((None,) + strip.shape[1:], lambda h: (h, 0, 0))],
        out_specs=pl.BlockSpec((3, None, NA_Q_ROWS * GRID_W, n_band * GRID_W), lambda h: (0, h, 0, 0)),
        out_shape=jax.ShapeDtypeStruct((3, n_heads, NA_Q_ROWS * GRID_W, n_band * GRID_W), F32),
        compiler_params=_params(1),
        name="na_bias_tables",
    )(strip)


def _rope_tables(t):
    pos = jnp.arange(t)
    row = (pos // GRID_W).astype(F32)
    col = (pos % GRID_W).astype(F32)
    n = MLA_ROPE // 4
    freqs = ROPE_BASE ** (-jnp.arange(n, dtype=F32) / n)
    ang = jnp.concatenate([row[:, None] * freqs, col[:, None] * freqs], axis=-1)
    cos, sin = jnp.cos(ang), jnp.sin(ang)
    return jnp.concatenate([cos, cos, sin, sin], axis=-1)


_HALF_SPLIT = np.concatenate([np.arange(0, MLA_ROPE, 2), np.arange(1, MLA_ROPE, 2)])


def _rope_cols(w):
    hs = w[..., _HALF_SPLIT]
    return jnp.concatenate([hs, -hs[..., MLA_ROPE // 2:], hs[..., :MLA_ROPE // 2]], axis=-1)


def _rope_gain(g):
    hs = g[_HALF_SPLIT]
    return jnp.concatenate([hs, hs[MLA_ROPE // 2:], hs[:MLA_ROPE // 2]])


def _mla_weights(w_dq, g_dq, w_uq, w_dkv, g_dkv, w_uk, w_uv, g_qn, g_qr, g_kn, g_kr):
    w_uq_ext = jnp.concatenate([w_uq[..., :MLA_NOPE], _rope_cols(w_uq[..., MLA_NOPE:])], axis=-1)
    g_q = jnp.tile(jnp.concatenate([g_qn, _rope_gain(g_qr)]) * (MLA_SCALE * LOG2_E), MLA_HEADS)
    q_norm = jnp.sqrt(MLA_NOPE * jnp.max(g_qn * g_qn) + MLA_ROPE * jnp.max(g_qr * g_qr)) * (MLA_SCALE * LOG2_E)
    k_norm = jnp.sqrt(MLA_NOPE * jnp.max(g_kn * g_kn) + MLA_ROPE * jnp.max(g_kr * g_kr))
    bound = BOUND_SLACK * q_norm * k_norm
    return {
        "score_bound": bound,
        "q_padt": jnp.zeros((MLA_ROPE, LANES), F32).at[0, :].set(-bound),
        "w_dq": w_dq.astype(BF16),
        "g_dq": g_dq[None, :],
        "w_uqt": w_uq_ext.reshape(MLA_Q_LORA, -1).T.astype(BF16),
        "g_qt": jnp.broadcast_to(g_q[:, None], (g_q.shape[0], LANES)),
        "w_dkv": jnp.concatenate([w_dkv[:, :MLA_KV_LORA], _rope_cols(w_dkv[:, MLA_KV_LORA:])], axis=-1).astype(BF16),
        "g_dkv": g_dkv[None, :],
        "w_uk": w_uk.reshape(MLA_KV_LORA, -1).astype(BF16),
        "g_k": jnp.concatenate([g_kn, _rope_gain(g_kr)])[None, :],
        "w_uvt": w_uv.reshape(MLA_KV_LORA, -1).T.astype(BF16),
    }


def _block_diag(w):
    g, c, _ = w.shape
    out = jnp.zeros((g * c, g * c), w.dtype)
    for i in range(g):
        out = out.at[i * c:(i + 1) * c, i * c:(i + 1) * c].set(w[i])
    return out


def kernel(x, c, ctx, c_ctx, mod_w, mod_b, ffn_w_in, ffn_w_out, mla_w_dq, mla_g_dq, mla_w_uq, mla_w_dkv, mla_g_dkv, mla_w_uk, mla_w_uv, mla_g_qn, mla_g_qr, mla_g_kn, mla_g_kr, mla_w_o, pool_w, pool_scale, na_w_qkv, na_g_q, na_g_k, na_rpb, na_w_o, conv_w_in, conv_w, conv_w_out):
    assert x.shape[0] == 1 and x.shape[2] == D_MODEL and x.shape[1] % ROW_TILE == 0
    s = x.shape[1]
    d = D_MODEL
    xs = x[0]
    hc = ctx[0]
    n_ctx = hc.shape[0]

    cond = jnp.zeros((8, d), F32).at[0].set(c[0]).at[1].set(c_ctx)
    mods = _ada_params(cond, mod_w, mod_b)
    wts = {(0, 0): (ffn_w_in[0, 0].astype(BF16), ffn_w_out[0, 0].astype(BF16))}

    def ffn_x(xs, mx, layer, cast_next):
        cast = [(stack, key) for key in cast_next for stack in (ffn_w_in, ffn_w_out)]
        xs, slabs = _ffn(xs, mx, 0, *wts[(layer, 0)], cast=cast)
        for n, key in enumerate(cast_next):
            wts[key] = (slabs[2 * n], slabs[2 * n + 1])
        return xs

    mx = mods[0, 0].reshape(N_MOD, d)
    mc = mods[0, 1].reshape(N_MOD, d)
    xs = ffn_x(xs, mx, 0, [(0, 1), (1, 0)])
    hc = _ffn(hc, mc, 0, *wts[(0, 0)])[0]
    mw = _mla_weights(mla_w_dq[0], mla_g_dq[0], mla_w_uq[0], mla_w_dkv[0], mla_g_dkv[0], mla_w_uk[0], mla_w_uv[0],
                      mla_g_qn[0], mla_g_qr[0], mla_g_kn[0], mla_g_kr[0])
    no_rope = jnp.concatenate([jnp.ones((n_ctx, LANES // 2), F32), jnp.zeros((n_ctx, LANES // 2), F32)], axis=-1)
    q_x, k_x, vt_x = _mla_proj(xs, mx, _rope_tables(s), mw)
    q_c, k_c, vt_c = _mla_proj(hc, mc, no_rope, mw)
    o_x = lax.cond(2.0 * mw["score_bound"] <= SCORE_RANGE_LOG2,
                   lambda q, k, vt, ke, vte: _mla_attn(q, k, vt, extra=(ke, vte), bounded=True),
                   lambda q, k, vt, ke, vte: _mla_attn(q, k, vt, extra=(ke, vte)),
                   q_x, k_x, vt_x, k_c, vt_c)
    o_c = _mla_attn(q_c, k_c, vt_c)
    w_o = mla_w_o[0].astype(BF16)
    xs = _ffn(xs, mx, 6, *wts[(0, 1)], pro=(o_x, w_o, mx[5:6]))[0]
    hc = _ffn(hc, mc, 6, *wts[(0, 1)], pro=(o_c, w_o, mc[5:6]))[0]

    mx = mods[1, 0].reshape(N_MOD, d)
    mc = mods[1, 1].reshape(N_MOD, d)
    xs = ffn_x(xs, mx, 1, [(1, 1), (2, 0)])
    hc = _ffn(hc, mc, 0, *wts[(1, 0)])[0]
    w_p = _block_diag(pool_w[0]).astype(BF16)
    xs = _ffn(xs, mx, 6, *wts[(1, 1)], pro=(_pool(xs, mx), w_p, mx[5:6] * pool_scale[0][None, :]))[0]
    hc = _ffn(hc, mc, 6, *wts[(1, 1)], pro=(_pool(hc, mc), w_p, mc[5:6] * pool_scale[0][None, :]))[0]

    mx = mods[2, 0].reshape(N_MOD, d)
    mc = mods[2, 1].reshape(N_MOD, d)
    xs = ffn_x(xs, mx, 2, [(2, 1), (3, 0)])
    hc = _ffn(hc, mc, 0, *wts[(2, 0)])[0]
    w_qkv = na_w_qkv[0].astype(BF16)
    g_q2 = jnp.tile(na_g_q[0] * (NA_SCALE * LOG2_E), 2)[None, :]
    g_k2 = jnp.tile(na_g_k[0], 2)[None, :]
    q_n, k_n, v_n = _na_proj(xs, mx, w_qkv, g_q2, g_k2)
    _, k_nc, v_nc = _na_proj(hc, mc, w_qkv, g_q2, g_k2)
    qk_bound = BOUND_SLACK * (NA_SCALE * LOG2_E) * NA_HEAD_DIM * jnp.sqrt(
        jnp.max(na_g_q[0] * na_g_q[0]) * jnp.max(na_g_k[0] * na_g_k[0]))
    shift = qk_bound + jnp.maximum(jnp.max(na_rpb[0]) * LOG2_E, 0.0)
    bias = _na_bias_tables(na_rpb[0], s // GRID_W, shift)
    xshift = jnp.full((1, n_ctx), -shift, F32)
    o_n = lax.cond(qk_bound + shift <= SCORE_RANGE_LOG2,
                   lambda *a: _na_attn(*a, bounded=True), lambda *a: _na_attn(*a),
                   q_n, k_n, v_n, k_nc, v_nc, bias, xshift)
    xs = _ffn(xs, mx, 6, *wts[(2, 1)], pro=(o_n, na_w_o[0].astype(BF16), mx[5:6]))[0]

    mx = mods[3, 0].reshape(N_MOD, d)
    xs = ffn_x(xs, mx, 3, [(3, 1)])
    w_ci = conv_w_in[0].astype(BF16)
    y_c = _conv(xs, mx, w_ci[:, :d], w_ci[:, d:], conv_w[0])
    xs = _ffn(xs, mx, 6, *wts[(3, 1)], pro=(y_c, conv_w_out[0].astype(BF16), mx[5:6]))[0]
    return xs[None]
```

```python
import functools

import jax
import jax.numpy as jnp
import numpy as np
from jax import lax
from jax.experimental import pallas as pl
from jax.experimental.pallas import tpu as pltpu

F32 = jnp.float32
BF16 = jnp.bfloat16

D_MODEL = 1024
DEPTH = 4
GRID_W = 64
N_MOD = 9
D_FF = 2816
EPS = 1e-6
MLA_HEADS = 8
MLA_Q_LORA = 384
MLA_KV_LORA = 256
MLA_NOPE = 128
MLA_ROPE = 64
MLA_V = 128
MLA_SCALE = (MLA_NOPE + MLA_ROPE) ** -0.5
ROPE_BASE = 10000.0
POOL_WINDOWS = (2, 4, 8, 16)
POOL_GROUP = D_MODEL // len(POOL_WINDOWS)
NA_HEADS = 16
NA_HEAD_DIM = D_MODEL // NA_HEADS
NA_ROWS = 8
NA_COLS = 16
NA_SCALE = NA_HEAD_DIM ** -0.5

LANES = 128
MXU_COLS = 256
VMEM_LIMIT = 56 * 1024 * 1024
ROW_TILE = 512
FFN_ROW_TILE = 1024
MLA_Q_TILE = 2048
FF_CHUNK = MXU_COLS
POOL_HALO = 8
CONV_HALO = 16
NA_Q_ROWS = 4
NEG_BIG = -1e30
LOG2_E = 1.4426950408889634
SCORE_RANGE_LOG2 = 60.0
BOUND_SLACK = 1.01

_NT = (((1,), (1,)), ((), ()))


def _params(n_axes, flags=None, fusible=None):
    return pltpu.CompilerParams(dimension_semantics=("arbitrary",) * n_axes, vmem_limit_bytes=VMEM_LIMIT, flags=flags,
                                allow_input_fusion=fusible)


def _const_spec(shape):
    nd = len(shape)
    return pl.BlockSpec(shape, lambda *_: (0,) * nd, pipeline_mode=pl.Buffered(1))


def _rms(x):
    return x * lax.rsqrt(jnp.mean(x * x, axis=-1, keepdims=True) + EPS)


def _modulate(x, mod_ref, row):
    shift = mod_ref[row:row + 1, :]
    scale = mod_ref[row + 1:row + 2, :]
    return _rms(x) * (1.0 + scale) + shift


def _dot(a, b):
    return jnp.dot(a, b, preferred_element_type=F32)


def _ada_kernel(cond_ref, w_ref, b_ref, o_ref):
    cnd = cond_ref[...]
    s = (cnd * (1.0 / (1.0 + jnp.exp(-cnd)))).astype(BF16)
    o_ref[...] = _dot(s, w_ref[...].astype(BF16)) + b_ref[...]


def _ada_params(cond, mod_w, mod_b):
    depth, d, n = mod_w.shape
    tn = n // 8
    return pl.pallas_call(
        _ada_kernel,
        grid=(depth, n // tn),
        in_specs=[
            pl.BlockSpec((8, d), lambda i, j: (0, 0)),
            pl.BlockSpec((None, d, tn), lambda i, j: (i, 0, j)),
            pl.BlockSpec((None, 1, tn), lambda i, j: (i, 0, j)),
        ],
        out_specs=pl.BlockSpec((None, 8, tn), lambda i, j: (i, 0, j)),
        out_shape=jax.ShapeDtypeStruct((depth, 8, n), F32),
        compiler_params=_params(2),
        name="ada_params",
    )(cond, mod_w, mod_b.reshape(depth, 1, n))


def _ffn_kernel(has_pro, n_cast, mod_row, *refs):
    refs = list(refs)
    x_ref = refs.pop(0)
    if has_pro:
        y_ref, wp_ref, gp_ref = refs[:3]
        refs = refs[3:]
    mod_ref, wi_ref, wo_ref = refs[:3]
    cast_src = refs[3:3 + n_cast]
    o_ref = refs[3 + n_cast]
    cast_dst = refs[4 + n_cast:4 + 2 * n_cast]
    a_ref = refs[4 + 2 * n_cast]
    x = x_ref[...]
    if has_pro:
        x = x + gp_ref[...] * _dot(y_ref[...], wp_ref[...])
    h = _modulate(x, mod_ref, mod_row).astype(BF16)
    for j in range(D_FF // FF_CHUNK):
        g = _dot(h, wi_ref[:, j * FF_CHUNK:(j + 1) * FF_CHUNK])
        u = _dot(h, wi_ref[:, D_FF + j * FF_CHUNK:D_FF + (j + 1) * FF_CHUNK])
        a_ref[:, j * FF_CHUNK:(j + 1) * FF_CHUNK] = (g * (1.0 / (1.0 + jnp.exp(-g))) * u).astype(BF16)
    gate = mod_ref[mod_row + 2:mod_row + 3, :]
    o_ref[...] = x + (0.5 * gate) * _dot(a_ref[...], wo_ref[...])
    for src, dst in zip(cast_src, cast_dst):
        dst[...] = src[...].astype(BF16)


def _ffn(x, mod, mod_row, w_in, w_out, pro=None, cast=()):
    t, d = x.shape
    tm = min(FFN_ROW_TILE, t)
    n_steps = t // tm
    row = lambda i: (i, 0)
    in_specs = [pl.BlockSpec((tm, d), row)]
    args = [x]
    if pro is not None:
        y, w_p, g_p = pro
        in_specs += [pl.BlockSpec((tm, d), row), _const_spec(w_p.shape), _const_spec(g_p.shape)]
        args += [y, w_p, g_p]
    in_specs += [_const_spec(mod.shape), _const_spec(w_in.shape), _const_spec(w_out.shape)]
    args += [mod, w_in, w_out]
    fusible = ([False, False, True, False] if pro is not None else [False]) + [False, True, True] + [False] * len(cast)
    out_specs = [pl.BlockSpec((tm, d), row)]
    out_shape = [jax.ShapeDtypeStruct((t, d), F32)]
    for stack, lead in cast:
        rows, cols = stack.shape[-2:]
        assert rows % (16 * n_steps) == 0
        blk = rows // n_steps
        in_specs.append(pl.BlockSpec((None,) * len(lead) + (blk, cols), lambda i, lead=lead: tuple(lead) + (i, 0)))
        args.append(stack)
        out_specs.append(pl.BlockSpec((blk, cols), row))
        out_shape.append(jax.ShapeDtypeStruct((rows, cols), BF16))
    outs = pl.pallas_call(
        functools.partial(_ffn_kernel, pro is not None, len(cast), mod_row),
        grid=(n_steps,),
        in_specs=in_specs,
        out_specs=out_specs,
        out_shape=out_shape,
        scratch_shapes=[pltpu.VMEM((tm, D_FF), BF16)],
        compiler_params=_params(1, fusible=fusible),
        name="ffn_pro" if pro is not None else "ffn",
    )(*args)
    return outs[0], list(outs[1:])


def _rope_pair(r2, gain, cs, first_half):
    ms = jnp.sum(jnp.where(first_half, r2 * r2, 0.0), axis=-1, keepdims=True) * (1.0 / MLA_ROPE)
    t = r2 * lax.rsqrt(ms + EPS) * gain * cs
    return t + pltpu.roll(t, MLA_ROPE, axis=1)


def _mla_proj_kernel(x_ref, mod_ref, cs_ref, cst_ref, wdq_ref, gdq_ref, wuqt_ref, gqt_ref, qpadt_ref, wdkv_ref,
                     gdkv_ref, wuk_ref, gk_ref, wuvt_ref, qt_ref, k_ref, vt_ref):
    h = _modulate(x_ref[...], mod_ref, 3).astype(BF16)
    cs = cs_ref[...]
    lane = lax.broadcasted_iota(jnp.int32, (1, LANES), 1)
    first_half = lane < MLA_ROPE
    tm = x_ref.shape[0]

    def lanes_of(tab_ref, rows):
        return jnp.concatenate([tab_ref[rows, :]] * (tm // LANES), axis=1)

    cq = (_rms(_dot(h, wdq_ref[...])) * gdq_ref[...]).astype(BF16)
    qt = lax.dot_general(wuqt_ref[...], cq, _NT, preferred_element_type=F32)
    cst = cst_ref[...]
    for hd in range(MLA_HEADS):
        r0 = hd * 2 * LANES
        nope = qt[r0:r0 + MLA_NOPE, :]
        inv = lax.rsqrt(jnp.mean(nope * nope, axis=0, keepdims=True) + EPS)
        qt_ref[hd, 0:MLA_NOPE, :] = (nope * inv * lanes_of(gqt_ref, slice(r0, r0 + MLA_NOPE))).astype(BF16)
        r1 = r0 + MLA_NOPE
        rope = qt[r1:r1 + MLA_ROPE, :]
        swap = qt[r1 + MLA_ROPE:r1 + 2 * MLA_ROPE, :]
        inv = lax.rsqrt(jnp.mean(rope * rope, axis=0, keepdims=True) + EPS)
        rot = (rope * lanes_of(gqt_ref, slice(r1, r1 + MLA_ROPE)) * cst[0:MLA_ROPE, :]
               + swap * lanes_of(gqt_ref, slice(r1 + MLA_ROPE, r1 + 2 * MLA_ROPE)) * cst[MLA_ROPE:, :]) * inv
        qt_ref[hd, MLA_NOPE:MLA_NOPE + MLA_ROPE, :] = rot.astype(BF16)
        qt_ref[hd, MLA_NOPE + MLA_ROPE:, :] = lanes_of(qpadt_ref, slice(None)).astype(BF16)
    kv = _dot(h, wdkv_ref[...])
    ckv = (_rms(kv[:, :MLA_KV_LORA]) * gdkv_ref[...]).astype(BF16)
    rot = _rope_pair(kv[:, MLA_KV_LORA:], gk_ref[:, LANES:2 * LANES], cs, first_half)
    kr = jnp.where(first_half, rot, jnp.where(lane == MLA_ROPE, 1.0, 0.0)).astype(BF16)
    kn = _dot(ckv, wuk_ref[...])
    vt = lax.dot_general(wuvt_ref[...], ckv, _NT, preferred_element_type=F32)
    for hd in range(MLA_HEADS):
        blk = kn[:, hd * LANES:(hd + 1) * LANES]
        k_ref[hd, :, 0:LANES] = (_rms(blk) * gk_ref[:, 0:LANES]).astype(BF16)
        k_ref[hd, :, LANES:2 * LANES] = kr
        vt_ref[hd] = vt[hd * MLA_V:(hd + 1) * MLA_V, :].astype(BF16)


def _mla_proj(x, mod, cs, w):
    t, d = x.shape
    tm = min(ROW_TILE, t)
    consts = [w["w_dq"], w["g_dq"], w["w_uqt"], w["g_qt"], w["q_padt"], w["w_dkv"], w["g_dkv"], w["w_uk"], w["g_k"], w["w_uvt"]]
    return pl.pallas_call(
        _mla_proj_kernel,
        grid=(t // tm,),
        in_specs=[pl.BlockSpec((tm, d), lambda i: (i, 0)), _const_spec(mod.shape),
                  pl.BlockSpec((tm, LANES), lambda i: (i, 0)), pl.BlockSpec((LANES, tm), lambda i: (0, i))]
        + [_const_spec(a.shape) for a in consts],
        out_specs=[
            pl.BlockSpec((MLA_HEADS, 2 * LANES, tm), lambda i: (0, 0, i)),
            pl.BlockSpec((MLA_HEADS, tm, 2 * LANES), lambda i: (0, i, 0)),
            pl.BlockSpec((MLA_HEADS, None, MLA_V, tm), lambda i: (0, i, 0, 0)),
        ],
        out_shape=[
            jax.ShapeDtypeStruct((MLA_HEADS, 2 * LANES, t), BF16),
            jax.ShapeDtypeStruct((MLA_HEADS, t, 2 * LANES), BF16),
            jax.ShapeDtypeStruct((MLA_HEADS, t // tm, MLA_V, tm), BF16),
        ],
        compiler_params=_params(1),
        name="mla_proj",
    )(x, mod, cs, cs.T, *consts)


def _mla_attn_kernel(n_main, sub, tv, has_extra, *refs):
    if has_extra:
        q_ref, k_ref, vt_ref, ke_ref, vte_ref, o_ref, acc_ref, s_ref = refs
    else:
        q_ref, k_ref, vt_ref, o_ref, acc_ref, s_ref = refs
    qt = q_ref[...]
    tq = qt.shape[1]
    tk = sub * tv
    acc_ref[...] = jnp.zeros_like(acc_ref)

    def scores(j):
        k = k_ref[pl.ds(pl.multiple_of(j * tk, tk), tk), :]
        return _dot(k, qt)

    def main_vts(j):
        return [vt_ref[j * sub + c] for c in range(sub)]

    def produce(slot, j):
        sc = scores(j)
        s_ref[slot] = sc
        return jnp.max(sc, axis=0, keepdims=True)

    def update(carry, s, s_max, vts):
        m, l = carry
        m_new = jnp.maximum(m, s_max)
        alpha = jnp.exp2(m - m_new)
        p = jnp.exp2(s - m_new)
        l = alpha * l + jnp.sum(p, axis=0, keepdims=True)
        pb = p.astype(BF16)
        pv = _dot(vts[0], pb[0:vts[0].shape[1], :])
        for c in range(1, len(vts)):
            pv = pv + _dot(vts[c], pb[c * tv:(c + 1) * tv, :])
        acc_ref[...] = alpha * acc_ref[...] + pv
        return m_new, l

    carry = (jnp.full((1, tq), NEG_BIG, F32), jnp.zeros((1, tq), F32))
    max0 = produce(0, 0)
    if n_main > 1:
        assert n_main % 2 == 0

        def body(jj, state):
            carry, max0 = state
            j = 2 * jj
            max1 = produce(1, j + 1)
            carry = update(carry, s_ref[0], max0, main_vts(j))
            max0 = produce(0, j + 2)
            return update(carry, s_ref[1], max1, main_vts(j + 1)), max0

        carry, max0 = lax.fori_loop(0, n_main // 2 - 1, body, (carry, max0))
        max1 = produce(1, n_main - 1)
        carry = update(carry, s_ref[0], max0, main_vts(n_main - 2))
        carry = update(carry, s_ref[1], max1, main_vts(n_main - 1))
    else:
        carry = update(carry, s_ref[0], max0, main_vts(0))
    if has_extra:
        s_e = _dot(ke_ref[...], qt)
        carry = update(carry, s_e, jnp.max(s_e, axis=0, keepdims=True), [vte_ref[...]])
    o_ref[...] = (acc_ref[...] * (1.0 / carry[1])).T.astype(BF16)


def _mla_attn_bounded_kernel(n_main, sub, tv, has_extra, *refs):
    if has_extra:
        q_ref, k_ref, vt_ref, ke_ref, vte_ref, o_ref, acc_ref, l_ref = refs
    else:
        q_ref, k_ref, vt_ref, o_ref, acc_ref, l_ref = refs
    qt = q_ref[...]
    tk = sub * tv
    acc_ref[...] = jnp.zeros_like(acc_ref)
    l_ref[...] = jnp.zeros_like(l_ref)

    def accumulate(k, vts):
        p = jnp.exp2(_dot(k, qt))
        l_ref[...] += jnp.sum(p, axis=0, keepdims=True)
        pb = p.astype(BF16)
        pv = _dot(vts[0], pb[0:vts[0].shape[1], :])
        for c in range(1, len(vts)):
            pv = pv + _dot(vts[c], pb[c * tv:(c + 1) * tv, :])
        acc_ref[...] += pv

    def body(j, _):
        k = k_ref[pl.ds(pl.multiple_of(j * tk, tk), tk), :]
        accumulate(k, [vt_ref[j * sub + c] for c in range(sub)])
        return 0

    lax.fori_loop(0, n_main, body, 0, unroll=True)
    if has_extra:
        accumulate(ke_ref[...], [vte_ref[...]])
    o_ref[...] = (acc_ref[...] * (1.0 / l_ref[...])).T.astype(BF16)


def _mla_attn(q, k, vt, extra=None, bounded=False):
    nh, dk, tq_all = q.shape
    tk_all = k.shape[1]
    n_v, tv = vt.shape[1], vt.shape[3]
    sub = 2 if n_v % 2 == 0 else 1
    tq = min(MLA_Q_TILE, tq_all)
    kv_mode = None if bounded else pl.Buffered(1)
    in_specs = [
        pl.BlockSpec((None, dk, tq), lambda h, i: (h, 0, i)),
        pl.BlockSpec((None, tk_all, dk), lambda h, i: (h, 0, 0), pipeline_mode=kv_mode),
        pl.BlockSpec((None, n_v, MLA_V, tv), lambda h, i: (h, 0, 0, 0), pipeline_mode=kv_mode),
    ]
    if bounded:
        body = functools.partial(_mla_attn_bounded_kernel, n_v // sub, sub, tv, extra is not None)
        scratch = [pltpu.VMEM((MLA_V, tq), F32), pltpu.VMEM((1, tq), F32)]
    else:
        body = functools.partial(_mla_attn_kernel, n_v // sub, sub, tv, extra is not None)
        scratch = [pltpu.VMEM((MLA_V, tq), F32), pltpu.VMEM((2, sub * tv, tq), F32)]
    args = [q, k, vt]
    if extra is not None:
        k_e, vt_e = extra
        in_specs += [pl.BlockSpec((None,) + k_e.shape[1:], lambda h, i: (h, 0, 0)),
                     pl.BlockSpec((None, None) + vt_e.shape[2:], lambda h, i: (h, 0, 0, 0))]
        args += [k_e, vt_e]
    return pl.pallas_call(
        body,
        grid=(nh, tq_all // tq),
        in_specs=in_specs,
        out_specs=pl.BlockSpec((tq, MLA_V), lambda h, i: (i, h)),
        out_shape=jax.ShapeDtypeStruct((tq_all, nh * MLA_V), BF16),
        scratch_shapes=scratch,
        compiler_params=_params(2),
        name="mla_attn_bounded" if bounded else "mla_attn",
    )(*args)


def _pool_kernel(t_total, x_ref, xp_ref, xn_ref, mod_ref, o_ref, ext_ref, lvl_a_ref, lvl_b_ref):
    tm = x_ref.shape[0]
    base = pl.program_id(0) * tm
    hc = _modulate(x_ref[...], mod_ref, 3)
    halo_rows = lax.broadcasted_iota(jnp.int32, (POOL_HALO, 1), 0)
    hp = jnp.where(base - POOL_HALO + halo_rows >= 0, _modulate(xp_ref[...], mod_ref, 3), 0.0)
    hn = jnp.where(base + tm + halo_rows < t_total, _modulate(xn_ref[...], mod_ref, 3), 0.0)
    ext_ref[0:POOL_HALO, :] = hp
    ext_ref[POOL_HALO:POOL_HALO + tm, :] = hc
    ext_ref[POOL_HALO + tm:, :] = hn

    tok = base + lax.broadcasted_iota(jnp.int32, (tm, 1), 0)
    for g, win in enumerate(POOL_WINDOWS):
        half = win // 2
        cols = slice(g * POOL_GROUP, (g + 1) * POOL_GROUP)
        cur, length, step, nxt = ext_ref, tm + 2 * POOL_HALO, 1, 0
        while step < half:
            length -= step
            dst = (lvl_a_ref, lvl_b_ref)[nxt]
            dst[0:length, cols] = cur[0:length, cols] + cur[step:step + length, cols]
            cur, step, nxt = dst, 2 * step, 1 - nxt
        win_sum = cur[POOL_HALO - half:POOL_HALO - half + tm, cols] + cur[POOL_HALO:POOL_HALO + tm, cols]
        cnt = (jnp.minimum(tok + half, t_total) - jnp.maximum(tok - half, 0)).astype(F32)
        o_ref[:, cols] = (win_sum / cnt - ext_ref[POOL_HALO:POOL_HALO + tm, cols]).astype(BF16)


def _pool(x, mod):
    t, d = x.shape
    tm = min(ROW_TILE, t)
    per = tm // POOL_HALO
    last = t // POOL_HALO - 1
    return pl.pallas_call(
        functools.partial(_pool_kernel, t),
        grid=(t // tm,),
        in_specs=[
            pl.BlockSpec((tm, d), lambda i: (i, 0)),
            pl.BlockSpec((POOL_HALO, d), lambda i: (jnp.maximum(i * per - 1, 0), 0)),
            pl.BlockSpec((POOL_HALO, d), lambda i: (jnp.minimum((i + 1) * per, last), 0)),
            _const_spec(mod.shape),
        ],
        out_specs=pl.BlockSpec((tm, d), lambda i: (i, 0)),
        out_shape=jax.ShapeDtypeStruct((t, d), BF16),
        scratch_shapes=[pltpu.VMEM((tm + 2 * POOL_HALO, d), F32)] * 3,
        compiler_params=_params(1),
        name="pool",
    )(x, x, x, mod)


def _conv_kernel(t_total, x_ref, xp_ref, xn_ref, mod_ref, wb_ref, wcu_ref, wconv_ref, o_ref, ext_ref, cu_ref):
    tm, d = x_ref.shape
    base = pl.program_id(0) * tm
    ext_ref[0:CONV_HALO, :] = _modulate(xp_ref[...], mod_ref, 3).astype(BF16)
    ext_ref[CONV_HALO:CONV_HALO + tm, :] = _modulate(x_ref[...], mod_ref, 3).astype(BF16)
    ext_ref[CONV_HALO + tm:, :] = _modulate(xn_ref[...], mod_ref, 3).astype(BF16)
    ext = ext_ref[...]
    cu = _dot(ext, wcu_ref[:, 0:d]) * _dot(ext, wcu_ref[:, d:2 * d])
    tok = base - CONV_HALO + lax.broadcasted_iota(jnp.int32, (tm + 2 * CONV_HALO, 1), 0)
    cu_ref[...] = jnp.where((tok >= 0) & (tok < t_total), cu, 0.0)
    z = (wconv_ref[0:1, :] * cu_ref[CONV_HALO - 1:CONV_HALO - 1 + tm, :]
         + wconv_ref[1:2, :] * cu_ref[CONV_HALO:CONV_HALO + tm, :]
         + wconv_ref[2:3, :] * cu_ref[CONV_HALO + 1:CONV_HALO + 1 + tm, :])
    b = _dot(ext_ref[CONV_HALO:CONV_HALO + tm, :], wb_ref[...])
    o_ref[...] = (b * z).astype(BF16)


def _conv(x, mod, w_b, w_cu, w_conv):
    t, d = x.shape
    tm = min(FFN_ROW_TILE, t)
    per = tm // CONV_HALO
    last = t // CONV_HALO - 1
    return pl.pallas_call(
        functools.partial(_conv_kernel, t),
        grid=(t // tm,),
        in_specs=[
            pl.BlockSpec((tm, d), lambda i: (i, 0)),
            pl.BlockSpec((CONV_HALO, d), lambda i: (jnp.maximum(i * per - 1, 0), 0)),
            pl.BlockSpec((CONV_HALO, d), lambda i: (jnp.minimum((i + 1) * per, last), 0)),
            _const_spec(mod.shape), _const_spec(w_b.shape), _const_spec(w_cu.shape), _const_spec(w_conv.shape),
        ],
        out_specs=pl.BlockSpec((tm, d), lambda i: (i, 0)),
        out_shape=jax.ShapeDtypeStruct((t, d), BF16),
        scratch_shapes=[pltpu.VMEM((tm + 2 * CONV_HALO, d), BF16), pltpu.VMEM((tm + 2 * CONV_HALO, d), F32)],
        compiler_params=_params(1),
        name="conv",
    )(x, x, x, mod, w_b, w_cu, w_conv)


def _head_rms(v, first_half):
    outs = []
    for c in range(v.shape[1] // LANES):
        blk = v[:, c * LANES:(c + 1) * LANES]
        sq = blk * blk
        tot = jnp.sum(sq, axis=-1, keepdims=True)
        lo = jnp.sum(jnp.where(first_half, sq, 0.0), axis=-1, keepdims=True)
        ms = jnp.where(first_half, lo, tot - lo) * (1.0 / NA_HEAD_DIM)
        outs.append(blk * lax.rsqrt(ms + EPS))
    return outs


def _na_proj_kernel(x_ref, mod_ref, w_ref, gq_ref, gk_ref, q_ref, k_ref, v_ref):
    d = x_ref.shape[1]
    h = _modulate(x_ref[...], mod_ref, 3).astype(BF16)
    first_half = lax.broadcasted_iota(jnp.int32, (1, LANES), 1) < NA_HEAD_DIM
    for c, blk in enumerate(_head_rms(_dot(h, w_ref[:, 0:d]), first_half)):
        q_ref[:, c * LANES:(c + 1) * LANES] = (blk * gq_ref[...]).astype(BF16)
    for c, blk in enumerate(_head_rms(_dot(h, w_ref[:, d:2 * d]), first_half)):
        k_ref[:, c * LANES:(c + 1) * LANES] = (blk * gk_ref[...]).astype(BF16)
    v_ref[...] = _dot(h, w_ref[:, 2 * d:3 * d]).astype(BF16)


def _na_proj(x, mod, w_qkv, g_q2, g_k2):
    t, d = x.shape
    tm = min(FFN_ROW_TILE, t)
    spec = pl.BlockSpec((tm, d), lambda i: (i, 0))
    return pl.pallas_call(
        _na_proj_kernel,
        grid=(t // tm,),
        in_specs=[spec, _const_spec(mod.shape), _const_spec(w_qkv.shape), _const_spec(g_q2.shape),
                  _const_spec(g_k2.shape)],
        out_specs=[spec, spec, spec],
        out_shape=[jax.ShapeDtypeStruct((t, d), BF16)] * 3,
        compiler_params=_params(1),
        name="na_proj",
    )(x, mod, w_qkv, g_q2, g_k2)


def _na_attn_kernel(bounded, q_ref, k0_ref, k1_ref, k2_ref, v0_ref, v1_ref, v2_ref, kx_ref, vx_ref, bias_ref,
                    xshift_ref, o_ref):
    nq = q_ref.shape[0]
    first_half = lax.broadcasted_iota(jnp.int32, (1, LANES), 1) < NA_HEAD_DIM
    n_pairs = NA_HEADS // 2

    def scores(p):
        cols = slice(p * LANES, (p + 1) * LANES)
        qp = q_ref[:, cols]
        zero = jnp.zeros_like(qp)
        qs = jnp.concatenate([jnp.where(first_half, qp, zero), jnp.where(first_half, zero, qp)], axis=0)
        parts = []
        for t, k_ref in enumerate((k0_ref, k1_ref, k2_ref)):
            lanes = slice(t * nq, (t + 1) * nq)
            bias = jnp.concatenate([bias_ref[2 * p, :, lanes], bias_ref[2 * p + 1, :, lanes]], axis=0)
            parts.append(lax.dot_general(qs, k_ref[:, cols], _NT, preferred_element_type=F32) + bias)
        parts.append(lax.dot_general(qs, kx_ref[:, cols], _NT, preferred_element_type=F32) + xshift_ref[...])
        return parts

    def finish(p, parts):
        cols = slice(p * LANES, (p + 1) * LANES)
        if not bounded:
            m = jnp.max(jnp.maximum(jnp.maximum(parts[0], parts[1]), jnp.maximum(parts[2], parts[3])),
                        axis=-1, keepdims=True)
        e_sum = None
        o = None
        for s, v_ref in zip(parts, (v0_ref, v1_ref, v2_ref, vx_ref)):
            e = jnp.exp2(s) if bounded else jnp.exp2(s - m)
            part_o = _dot(e.astype(BF16), v_ref[:, cols])
            e_sum = e if e_sum is None else e_sum + e
            o = part_o if o is None else o + part_o
        o = o * (1.0 / jnp.sum(e_sum, axis=-1, keepdims=True))
        o_ref[:, cols] = jnp.where(first_half, o[0:nq], o[nq:2 * nq]).astype(BF16)

    parts_next = scores(0)
    for p in range(n_pairs):
        parts = parts_next
        if p + 1 < n_pairs:
            parts_next = scores(p + 1)
        finish(p, parts)


def _na_attn(q, k, v, k_ctx, v_ctx, bias, xshift, bounded=False):
    s, d = q.shape
    chunk = NA_Q_ROWS * GRID_W
    n_steps = s // chunk

    def band(t):
        return pl.BlockSpec((chunk, d), lambda i: (jnp.clip(i - 1, 0, n_steps - 3) + t, 0))

    cur = pl.BlockSpec((chunk, d), lambda i: (i, 0))
    bias_spec = pl.BlockSpec((None,) + bias.shape[1:],
                             lambda i: (jnp.where(i == 0, 0, jnp.where(i == n_steps - 1, 2, 1)), 0, 0, 0),
                             pipeline_mode=pl.Buffered(1))
    return pl.pallas_call(
        functools.partial(_na_attn_kernel, bounded),
        grid=(n_steps,),
        in_specs=[cur, band(0), band(1), band(2), band(0), band(1), band(2), _const_spec(k_ctx.shape),
                  _const_spec(v_ctx.shape), bias_spec, _const_spec(xshift.shape)],
        out_specs=cur,
        out_shape=jax.ShapeDtypeStruct((s, d), BF16),
        compiler_params=_params(1),
        name="na_attn_bounded" if bounded else "na_attn",
    )(q, k, k, k, v, v, v, k_ctx, v_ctx, bias, xshift)


def _na_band_pattern(step, n_rows):
    n_steps = n_rows // NA_Q_ROWS
    band0 = NA_Q_ROWS * int(np.clip(step - 1, 0, n_steps - 3))
    r = step * NA_Q_ROWS + np.arange(NA_Q_ROWS)[:, None]
    key_row = band0 + np.arange(3 * NA_Q_ROWS)[None, :]
    win0 = np.clip(r - NA_ROWS // 2, 0, n_rows - NA_ROWS)
    valid = (key_row >= win0) & (key_row < win0 + NA_ROWS)
    return np.where(valid, key_row - r + (NA_ROWS - 1), 0), valid


def _na_bias_tables(rpb, n_rows, shift):
    n_steps = n_rows // NA_Q_ROWS
    assert n_steps >= 4
    patterns = [_na_band_pattern(t, n_rows) for t in range(n_steps)]
    for dr, valid in patterns[2:-1]:
        assert np.array_equal(dr, patterns[1][0]) and np.array_equal(valid, patterns[1][1])
    cols = np.arange(GRID_W)
    col_start = np.clip(cols - NA_COLS // 2, 0, GRID_W - NA_COLS)
    kc = np.arange(GRID_W)
    col_ok = (kc[None, :] >= col_start[:, None]) & (kc[None, :] < col_start[:, None] + NA_COLS)
    pad = GRID_W - NA_COLS
    padded = jnp.pad(rpb * LOG2_E - shift, ((0, 0), (0, 0), (pad, pad)))
    toeplitz = jnp.stack([padded[:, :, GRID_W - 1 - c:2 * GRID_W - 1 - c] for c in range(GRID_W)], axis=2)
    toeplitz = jnp.where(col_ok[None, None], toeplitz, NEG_BIG)
    n_band = 3 * NA_Q_ROWS
    lead = NA_Q_ROWS
    n_dr = toeplitz.shape[1]
    strip = jnp.transpose(toeplitz, (0, 2, 1, 3)).reshape(rpb.shape[0], GRID_W, n_dr * GRID_W)
    strip = jnp.pad(strip, ((0, 0), (0, 0), (lead * GRID_W, lead * GRID_W)), constant_values=NEG_BIG)
    plan = []
    for dr, valid in (patterns[0], patterns[1], patterns[-1]):
        for jq in range(NA_Q_ROWS):
            rows_ok = np.nonzero(valid[jq])[0]
            lo, hi = int(rows_ok[0]), int(rows_ok[-1]) + 1
            assert hi - lo == len(rows_ok)
            start = int(dr[jq, lo]) - lo + lead
            assert 0 <= start and start + n_band <= n_dr + 2 * lead
            assert all(dr[jq, i] == start - lead + i for i in range(lo, hi))
            plan.append((start, lo, hi))

    def assemble(strip_ref, out_ref):
        lanes = lax.broadcasted_iota(jnp.int32, (1, n_band * GRID_W), 1)
        for n, (start, lo, hi) in enumerate(plan):
            kind, jq = divmod(n, NA_Q_ROWS)
            window = strip_ref[:, start * GRID_W:(start + n_band) * GRID_W]
            row_ok = (lanes >= lo * GRID_W) & (lanes < hi * GRID_W)
            out_ref[kind, jq * GRID_W:(jq + 1) * GRID_W, :] = jnp.where(row_ok, window, NEG_BIG)

    n_heads = rpb.shape[0]
    return pl.pallas_call(
        assemble,
        grid=(n_heads,),
        in_specs=[pl.BlockSpec((None,) + strip.shape[1:], lambda h: (h, 0, 0))],
        out_specs=pl.BlockSpec((3, None, NA_Q_ROWS * GRID_W, n_band * GRID_W), lambda h: (0, h, 0, 0)),
        out_shape=jax.ShapeDtypeStruct((3, n_heads, NA_Q_ROWS * GRID_W, n_band * GRID_W), F32),
        compiler_params=_params(1),
        name="na_bias_tables",
    )(strip)


def _rope_tables(t):
    pos = jnp.arange(t)
    row = (pos // GRID_W).astype(F32)
    col = (pos % GRID_W).astype(F32)
    n = MLA_ROPE // 4
    freqs = ROPE_BASE ** (-jnp.arange(n, dtype=F32) / n)
    ang = jnp.concatenate([row[:, None] * freqs, col[:, None] * freqs], axis=-1)
    cos, sin = jnp.cos(ang), jnp.sin(ang)
    return jnp.concatenate([cos, cos, sin, sin], axis=-1)


_HALF_SPLIT = np.concatenate([np.arange(0, MLA_ROPE, 2), np.arange(1, MLA_ROPE, 2)])


def _rope_cols(w):
    hs = w[..., _HALF_SPLIT]
    return jnp.concatenate([hs, -hs[..., MLA_ROPE // 2:], hs[..., :MLA_ROPE // 2]], axis=-1)


def _rope_gain(g):
    hs = g[_HALF_SPLIT]
    return jnp.concatenate([hs, hs[MLA_ROPE // 2:], hs[:MLA_ROPE // 2]])


def _mla_weights(w_dq, g_dq, w_uq, w_dkv, g_dkv, w_uk, w_uv, g_qn, g_qr, g_kn, g_kr):
    w_uq_ext = jnp.concatenate([w_uq[..., :MLA_NOPE], _rope_cols(w_uq[..., MLA_NOPE:])], axis=-1)
    g_q = jnp.tile(jnp.concatenate([g_qn, _rope_gain(g_qr)]) * (MLA_SCALE * LOG2_E), MLA_HEADS)
    q_norm = jnp.sqrt(MLA_NOPE * jnp.max(g_qn * g_qn) + MLA_ROPE * jnp.max(g_qr * g_qr)) * (MLA_SCALE * LOG2_E)
    k_norm = jnp.sqrt(MLA_NOPE * jnp.max(g_kn * g_kn) + MLA_ROPE * jnp.max(g_kr * g_kr))
    bound = BOUND_SLACK * q_norm * k_norm
    return {
        "score_bound": bound,
        "q_padt": jnp.zeros((MLA_ROPE, LANES), F32).at[0, :].set(-bound),
        "w_dq": w_dq.astype(BF16),
        "g_dq": g_dq[None, :],
        "w_uqt": w_uq_ext.reshape(MLA_Q_LORA, -1).T.astype(BF16),
        "g_qt": jnp.broadcast_to(g_q[:, None], (g_q.shape[0], LANES)),
        "w_dkv": jnp.concatenate([w_dkv[:, :MLA_KV_LORA], _rope_cols(w_dkv[:, MLA_KV_LORA:])], axis=-1).astype(BF16),
        "g_dkv": g_dkv[None, :],
        "w_uk": w_uk.reshape(MLA_KV_LORA, -1).astype(BF16),
        "g_k": jnp.concatenate([g_kn, _rope_gain(g_kr)])[None, :],
        "w_uvt": w_uv.reshape(MLA_KV_LORA, -1).T.astype(BF16),
    }


def _block_diag(w):
    g, c, _ = w.shape
    out = jnp.zeros((g * c, g * c), w.dtype)
    for i in range(g):
        out = out.at[i * c:(i + 1) * c, i * c:(i + 1) * c].set(w[i])
    return out


def kernel(x, c, ctx, c_ctx, mod_w, mod_b, ffn_w_in, ffn_w_out, mla_w_dq, mla_g_dq, mla_w_uq, mla_w_dkv, mla_g_dkv, mla_w_uk, mla_w_uv, mla_g_qn, mla_g_qr, mla_g_kn, mla_g_kr, mla_w_o, pool_w, pool_scale, na_w_qkv, na_g_q, na_g_k, na_rpb, na_w_o, conv_w_in, conv_w, conv_w_out):
    assert x.shape[0] == 1 and x.shape[2] == D_MODEL and x.shape[1] % ROW_TILE == 0
    s = x.shape[1]
    d = D_MODEL
    xs = x[0]
    hc = ctx[0]
    n_ctx = hc.shape[0]

    cond = jnp.zeros((8, d), F32).at[0].set(c[0]).at[1].set(c_ctx)
    mods = _ada_params(cond, mod_w, mod_b)
    wts = {(0, 0): (ffn_w_in[0, 0].astype(BF16), ffn_w_out[0, 0].astype(BF16))}

    def ffn_x(xs, mx, layer, cast_next):
        cast = [(stack, key) for key in cast_next for stack in (ffn_w_in, ffn_w_out)]
        xs, slabs = _ffn(xs, mx, 0, *wts[(layer, 0)], cast=cast)
        for n, key in enumerate(cast_next):
            wts[key] = (slabs[2 * n], slabs[2 * n + 1])
        return xs

    mx = mods[0, 0].reshape(N_MOD, d)
    mc = mods[0, 1].reshape(N_MOD, d)
    xs = ffn_x(xs, mx, 0, [(0, 1), (1, 0)])
    hc = _ffn(hc, mc, 0, *wts[(0, 0)])[0]
    mw = _mla_weights(mla_w_dq[0], mla_g_dq[0], mla_w_uq[0], mla_w_dkv[0], mla_g_dkv[0], mla_w_uk[0], mla_w_uv[0],
                      mla_g_qn[0], mla_g_qr[0], mla_g_kn[0], mla_g_kr[0])
    no_rope = jnp.concatenate([jnp.ones((n_ctx, LANES // 2), F32), jnp.zeros((n_ctx, LANES // 2), F32)], axis=-1)
    q_x, k_x, vt_x = _mla_proj(xs, mx, _rope_tables(s), mw)
    q_c, k_c, vt_c = _mla_proj(hc, mc, no_rope, mw)
    o_x = lax.cond(2.0 * mw["score_bound"] <= SCORE_RANGE_LOG2,
                   lambda q, k, vt, ke, vte: _mla_attn(q, k, vt, extra=(ke, vte), bounded=True),
                   lambda q, k, vt, ke, vte: _mla_attn(q, k, vt, extra=(ke, vte)),
                   q_x, k_x, vt_x, k_c, vt_c)
    o_c = _mla_attn(q_c, k_c, vt_c)
    w_o = mla_w_o[0].astype(BF16)
    xs = _ffn(xs, mx, 6, *wts[(0, 1)], pro=(o_x, w_o, mx[5:6]))[0]
    hc = _ffn(hc, mc, 6, *wts[(0, 1)], pro=(o_c, w_o, mc[5:6]))[0]

    mx = mods[1, 0].reshape(N_MOD, d)
    mc = mods[1, 1].reshape(N_MOD, d)
    xs = ffn_x(xs, mx, 1, [(1, 1), (2, 0)])
    hc = _ffn(hc, mc, 0, *wts[(1, 0)])[0]
    w_p = _block_diag(pool_w[0]).astype(BF16)
    xs = _ffn(xs, mx, 6, *wts[(1, 1)], pro=(_pool(xs, mx), w_p, mx[5:6] * pool_scale[0][None, :]))[0]
    hc = _ffn(hc, mc, 6, *wts[(1, 1)], pro=(_pool(hc, mc), w_p, mc[5:6] * pool_scale[0][None, :]))[0]

    mx = mods[2, 0].reshape(N_MOD, d)
    mc = mods[2, 1].reshape(N_MOD, d)
    xs = ffn_x(xs, mx, 2, [(2, 1), (3, 0)])
    hc = _ffn(hc, mc, 0, *wts[(2, 0)])[0]
    w_qkv = na_w_qkv[0].astype(BF16)
    g_q2 = jnp.tile(na_g_q[0] * (NA_SCALE * LOG2_E), 2)[None, :]
    g_k2 = jnp.tile(na_g_k[0], 2)[None, :]
    q_n, k_n, v_n = _na_proj(xs, mx, w_qkv, g_q2, g_k2)
    _, k_nc, v_nc = _na_proj(hc, mc, w_qkv, g_q2, g_k2)
    qk_bound = BOUND_SLACK * (NA_SCALE * LOG2_E) * NA_HEAD_DIM * jnp.sqrt(
        jnp.max(na_g_q[0] * na_g_q[0]) * jnp.max(na_g_k[0] * na_g_k[0]))
    shift = qk_bound + jnp.maximum(jnp.max(na_rpb[0]) * LOG2_E, 0.0)
    bias = _na_bias_tables(na_rpb[0], s // GRID_W, shift)
    xshift = jnp.full((1, n_ctx), -shift, F32)
    o_n = lax.cond(qk_bound + shift <= SCORE_RANGE_LOG2,
                   lambda *a: _na_attn(*a, bounded=True), lambda *a: _na_attn(*a),
                   q_n, k_n, v_n, k_nc, v_nc, bias, xshift)
    xs = _ffn(xs, mx, 6, *wts[(2, 1)], pro=(o_n, na_w_o[0].astype(BF16), mx[5:6]))[0]

    mx = mods[3, 0].reshape(N_MOD, d)
    xs = ffn_x(xs, mx, 3, [(3, 1)])
    w_ci = conv_w_in[0].astype(BF16)
    y_c = _conv(xs, mx, w_ci[:, :d], w_ci[:, d:], conv_w[0])
    xs = _ffn(xs, mx, 6, *wts[(3, 1)], pro=(y_c, conv_w_out[0].astype(BF16), mx[5:6]))[0]
    return xs[None]
```
